```python
import math
import jax
import jax.numpy as jnp
from jax import lax
import numpy as np

D_MODEL = 4096
BATCH = 8
SEQ = 2048
DEPTH = 2

CTX_LEN = 256
GRID_W = 64

F32 = jnp.float32
MIX_W = D_MODEL // 4
N_BRANCH = 4

ATT_HEAD_DIM = 128
ATT_HEADS = MIX_W // ATT_HEAD_DIM
ATT_KV_HEADS = 2
Q_BLOCK = 128
ROPE_THETA = 10000.0

GLA_HEADS = 4
GLA_DV = MIX_W // GLA_HEADS
GLA_DK = GLA_DV // 2
GLA_RANK = 16
GLA_GATE_NORM = 16.0

RET_HEADS = 4
RET_DV = MIX_W // RET_HEADS
RET_DK = RET_DV // 2

CHUNK = 64

S5_GROUP = 16
S5_GROUPS = MIX_W // S5_GROUP
S5_STATE = 64
S5_DT_MIN = 1e-3
S5_DT_MAX = 1e-1

FFN_DENSE = 2 * D_MODEL
N_EXPERTS = 8
TOP_K = 2
FFN_EXPERT = 3 * D_MODEL // 4
N_DENSE_LAYERS = (DEPTH + 1) // 2
N_MOE_LAYERS = DEPTH // 2

DEEPNORM_ALPHA = (2 * DEPTH) ** 0.25
DEEPNORM_BETA = (8 * DEPTH) ** -0.25
EPS = 1e-6

IN_SIZES = (ATT_HEADS * ATT_HEAD_DIM, ATT_KV_HEADS * ATT_HEAD_DIM, ATT_KV_HEADS * ATT_HEAD_DIM,
            GLA_HEADS * GLA_DK, GLA_HEADS * GLA_DK, GLA_HEADS * GLA_DV, GLA_HEADS * GLA_DV, 2 * GLA_RANK,
            RET_HEADS * RET_DK, RET_HEADS * RET_DK, RET_HEADS * RET_DV, RET_HEADS * RET_DV,
            MIX_W, N_BRANCH * D_MODEL)
IN_WIDTH = sum(IN_SIZES)

kernel_name = 'hybrid_dit_gla_gqa_retention_s5_moe'


def layer_norm(x, g, b):
    xf = x.astype(F32)
    mu = jnp.mean(xf, -1, keepdims=True)
    var = jnp.mean(jnp.square(xf - mu), -1, keepdims=True)
    return ((xf - mu) * lax.rsqrt(var + EPS) * g + b).astype(x.dtype)


def rms_norm(x, g):
    xf = x.astype(F32)
    return (xf * lax.rsqrt(jnp.mean(jnp.square(xf), -1, keepdims=True) + EPS) * g).astype(x.dtype)


def head_group_norm(x):
    xf = x.astype(F32)
    mu = jnp.mean(xf, -1, keepdims=True)
    var = jnp.mean(jnp.square(xf - mu), -1, keepdims=True)
    return ((xf - mu) * lax.rsqrt(var + EPS)).astype(x.dtype)


def split_heads(a, n):
    return a.reshape(a.shape[0], a.shape[1], n, a.shape[-1] // n)


def to_bhtd(a):
    return a.transpose(0, 2, 1, 3)


def split_columns(z):
    idx = []
    acc = 0
    for s in IN_SIZES[:-1]:
        acc += s
        idx.append(acc)
    return jnp.split(z, idx, axis=-1)


def axial_rope_tables(rows, cols, dim):
    nf = dim // 4
    inv = ROPE_THETA ** (-jnp.arange(nf, dtype=F32) / nf)
    ang = jnp.stack([rows[:, None] * inv, cols[:, None] * inv], axis=1)
    return jnp.cos(ang), jnp.sin(ang)


def apply_axial_rope(x, cos, sin):
    shp = x.shape
    nf = shp[-1] // 4
    xr = x.astype(F32).reshape(shp[:-1] + (2, 2, nf))
    x1, x2 = xr[..., 0, :], xr[..., 1, :]
    c = cos[None, :, None]
    s = sin[None, :, None]
    out = jnp.stack([x1 * c - x2 * s, x1 * s + x2 * c], axis=-2)
    return out.reshape(shp).astype(x.dtype)


def chunk_gated_recurrence(q, k, v, logg, s0, inclusive):
    bsz, nh, t, _ = q.shape
    n = t // CHUNK

    def chunks(a):
        return a.astype(F32).reshape(a.shape[:2] + (n, CHUNK) + a.shape[3:])

    qc, kc, vc = chunks(q), chunks(k), chunks(v)
    cum = jnp.cumsum(chunks(logg), axis=3)
    ref = cum[:, :, :, CHUNK // 2:CHUNK // 2 + 1]
    scores = jnp.einsum('bhnqd,bhnsd->bhnqs', qc * jnp.exp(cum - ref), kc * jnp.exp(ref - cum))
    pos = jnp.arange(CHUNK)
    mask = (pos[:, None] - pos[None, :]) >= (0 if inclusive else 1)
    o_intra = jnp.einsum('bhnqs,bhnsv->bhnqv', jnp.where(mask, scores, 0.0), vc)
    q_dec = qc * jnp.exp(cum)
    d_state = jnp.einsum('bhnsd,bhnsv->bhndv', kc * jnp.exp(cum[:, :, :, -1:] - cum), vc)
    chunk_decay = jnp.exp(cum[:, :, :, -1])

    def step(state, inp):
        qd, dec, ds = inp
        o = jnp.einsum('bhqd,bhdv->bhqv', qd, state)
        return dec[..., None] * state + ds, o

    s_fin, o_inter = lax.scan(step, s0.astype(F32),
                              (jnp.moveaxis(q_dec, 2, 0), jnp.moveaxis(chunk_decay, 2, 0),
                               jnp.moveaxis(d_state, 2, 0)))
    o = o_intra + jnp.moveaxis(o_inter, 0, 2)
    return o.reshape(bsz, nh, t, v.shape[-1]), s_fin


def context_state(k, v, logg, reverse):
    lg = logg.astype(F32)
    cum = jnp.cumsum(lg, axis=2)
    w = jnp.exp(cum - lg) if reverse else jnp.exp(cum[:, :, -1:] - cum)
    return jnp.einsum('bhtd,bhtv->bhdv', k.astype(F32) * w, v.astype(F32))


def bidirectional_recurrence(lat, ctx, ctx_out):
    q, k, v, g_f, g_b = lat
    qc, kc, vc, gc_f, gc_b = ctx

    def flip(a):
        return jnp.flip(a, axis=2)

    o_ctx = None
    if ctx_out:
        zero = jnp.zeros(kc.shape[:2] + (kc.shape[-1], vc.shape[-1]), F32)
        oc_f, s_f = chunk_gated_recurrence(qc, kc, vc, gc_f, zero, True)
        oc_b, s_b = chunk_gated_recurrence(flip(qc), flip(kc), flip(vc), flip(gc_b), zero, False)
        o_ctx = oc_f + flip(oc_b)
    else:
        s_f = context_state(kc, vc, gc_f, False)
        s_b = context_state(kc, vc, gc_b, True)
    o_f, _ = chunk_gated_recurrence(q, k, v, g_f, s_f, True)
    o_b, _ = chunk_gated_recurrence(flip(q), flip(k), flip(v), flip(g_b), s_b, False)
    return o_f + flip(o_b), o_ctx


def gqa_branch(q, k, v, qc, kc, vc, q_norm, k_norm, cos, sin, ctx_out):
    bsz, t = q.shape[:2]
    n_rep = ATT_HEADS // ATT_KV_HEADS
    scale = ATT_HEAD_DIM ** -0.5
    q = apply_axial_rope(rms_norm(split_heads(q, ATT_HEADS), q_norm), cos, sin)
    k = apply_axial_rope(rms_norm(split_heads(k, ATT_KV_HEADS), k_norm), cos, sin)
    kc = rms_norm(split_heads(kc, ATT_KV_HEADS), k_norm)
    vc = split_heads(vc, ATT_KV_HEADS)
    k_all = jnp.concatenate([kc, k], axis=1)
    v_all = jnp.concatenate([vc, split_heads(v, ATT_KV_HEADS)], axis=1)

    def attend(qb, keys, vals):
        s = jnp.einsum('bqkgd,bskd->bkgqs', qb, keys).astype(F32) * scale
        p = jax.nn.softmax(s, axis=-1).astype(vals.dtype)
        return jnp.einsum('bkgqs,bskd->bqkgd', p, vals)

    qb = q.reshape(bsz, t // Q_BLOCK, Q_BLOCK, ATT_KV_HEADS, n_rep, ATT_HEAD_DIM).swapaxes(0, 1)
    o = lax.map(lambda blk: attend(blk, k_all, v_all), qb)
    o = o.swapaxes(0, 1).reshape(bsz, t, ATT_HEADS * ATT_HEAD_DIM)
    o_c = None
    if ctx_out:
        tc = qc.shape[1]
        qcb = rms_norm(split_heads(qc, ATT_HEADS), q_norm).reshape(bsz, tc, ATT_KV_HEADS, n_rep, ATT_HEAD_DIM)
        o_c = attend(qcb, kc, vc).reshape(bsz, tc, ATT_HEADS * ATT_HEAD_DIM)
    return o, o_c


def gla_branch(parts, parts_c, w2, b2, norm, ctx_out):
    def prep(p):
        q, k, v, _, lr = p
        bsz, t = q.shape[:2]
        logits = jnp.einsum('btrj,rjk->btrk', lr.reshape(bsz, t, 2, GLA_RANK), w2) + b2
        logg = jax.nn.log_sigmoid(logits.astype(F32)) / GLA_GATE_NORM
        return (to_bhtd(split_heads(q, GLA_HEADS)) * GLA_DK ** -0.5,
                to_bhtd(split_heads(k, GLA_HEADS)),
                to_bhtd(split_heads(v, GLA_HEADS)),
                to_bhtd(split_heads(logg[:, :, 0], GLA_HEADS)),
                to_bhtd(split_heads(logg[:, :, 1], GLA_HEADS)))

    def finish(o, g):
        o = rms_norm(to_bhtd(o), norm)
        return (o.reshape(g.shape) * jax.nn.silu(g.astype(F32))).astype(g.dtype)

    o, oc = bidirectional_recurrence(prep(parts), prep(parts_c), ctx_out)
    return finish(o, parts[3]), (finish(oc, parts_c[3]) if ctx_out else None)


def retention_branch(parts, parts_c, cos, sin, ctx_out):
    log_gamma = jnp.log1p(-jnp.exp2(-5.0 - jnp.arange(RET_HEADS, dtype=F32)))

    def prep(p, rope):
        q, k, v, _ = p
        t = q.shape[1]
        q = split_heads(q, RET_HEADS)
        k = split_heads(k, RET_HEADS)
        if rope:
            q = apply_axial_rope(q, cos, sin)
            k = apply_axial_rope(k, cos, sin)
        lg = jnp.broadcast_to(log_gamma[None, :, None, None], (1, RET_HEADS, t, 1))
        return (to_bhtd(q), to_bhtd(k) * RET_DK ** -0.5, to_bhtd(split_heads(v, RET_HEADS)), lg, lg)

    def finish(o, g):
        o = head_group_norm(to_bhtd(o))
        return (o.reshape(g.shape) * jax.nn.silu(g.astype(F32))).astype(g.dtype)

    o, oc = bidirectional_recurrence(prep(parts, True), prep(parts_c, False), ctx_out)
    return finish(o, parts[3]), (finish(oc, parts_c[3]) if ctx_out else None)


def s5_discretize(lam_re, lam_im, log_step, b_re, b_im):
    lam_re, lam_im, b_re, b_im = (a.astype(F32) for a in (lam_re, lam_im, b_re, b_im))
    dt = jnp.exp(log_step.astype(F32))[:, None]
    mag = jnp.exp(lam_re * dt)
    a_re = mag * jnp.cos(lam_im * dt)
    a_im = mag * jnp.sin(lam_im * dt)
    den = lam_re * lam_re + lam_im * lam_im
    z_re = ((a_re - 1.0) * lam_re + a_im * lam_im) / den
    z_im = (a_im * lam_re - (a_re - 1.0) * lam_im) / den
    bb_re = z_re[..., None] * b_re - z_im[..., None] * b_im
    bb_im = z_re[..., None] * b_im + z_im[..., None] * b_re
    return a_re, a_im, bb_re, bb_im


def s5_scan(u, a_re, a_im, bb_re, bb_im, s0, reverse):
    if reverse:
        u = jnp.flip(u, axis=1)
    x_re = jnp.einsum('btgh,gph->btgp', u, bb_re)
    x_im = jnp.einsum('btgh,gph->btgp', u, bb_im)
    if s0 is not None:
        s_re, s_im = s0
        x_re = x_re.at[:, 0].add(a_re * s_re - a_im * s_im)
        x_im = x_im.at[:, 0].add(a_re * s_im + a_im * s_re)
    t = u.shape[1]
    ar = jnp.broadcast_to(a_re, (1, t) + a_re.shape)
    ai = jnp.broadcast_to(a_im, (1, t) + a_im.shape)

    def combine(e1, e2):
        a1r, a1i, b1r, b1i = e1
        a2r, a2i, b2r, b2i = e2
        return (a2r * a1r - a2i * a1i, a2r * a1i + a2i * a1r,
                a2r * b1r - a2i * b1i + b2r, a2r * b1i + a2i * b1r + b2i)

    _, _, st_re, st_im = lax.associative_scan(combine, (ar, ai, x_re, x_im), axis=1)
    if reverse:
        st_re, st_im = jnp.flip(st_re, axis=1), jnp.flip(st_im, axis=1)
    return st_re, st_im


def s5_readout(s_re, s_im, c_re, c_im):
    return jnp.einsum('btgp,ghp->btgh', s_re, c_re) - jnp.einsum('btgp,ghp->btgh', s_im, c_im)


def s5_branch(u, u_c, lam_re, lam_im, log_step, b_re, b_im, c_re, c_im, d, glu_w, glu_b, ctx_out):
    def groups(a):
        return a.astype(F32).reshape(a.shape[0], a.shape[1], S5_GROUPS, S5_GROUP)

    ug, ucg = groups(u), groups(u_c)
    dd = d.astype(F32)
    y = ug * dd
    yc = ucg * dd if ctx_out else None
    for r in range(2):
        reverse = r == 1
        a_re, a_im, bb_re, bb_im = s5_discretize(lam_re[r], lam_im[r], log_step[r], b_re[r], b_im[r])
        cr, ci = c_re[r].astype(F32), c_im[r].astype(F32)
        sc_re, sc_im = s5_scan(ucg, a_re, a_im, bb_re, bb_im, None, reverse)
        edge = 0 if reverse else -1
        s_re, s_im = s5_scan(ug, a_re, a_im, bb_re, bb_im, (sc_re[:, edge], sc_im[:, edge]), reverse)
        y = y + s5_readout(s_re, s_im, cr, ci)
        if ctx_out:
            yc = yc + s5_readout(sc_re, sc_im, cr, ci)

    def glu(z, like):
        z = jax.nn.gelu(z.reshape(like.shape)).astype(like.dtype)
        return z * jax.nn.sigmoid(z @ glu_w + glu_b)

    return glu(y, u), (glu(yc, u_c) if ctx_out else None)


def merge_branches(ys, gate_logits, w_branch, w_out):
    g = jax.nn.sigmoid(split_heads(gate_logits, N_BRANCH))
    acc = g[:, :, 0] * (ys[0] @ w_branch[0])
    for i in range(1, N_BRANCH):
        acc = acc + g[:, :, i] * (ys[i] @ w_branch[i])
    return acc @ w_out


def token_mixer(h, hc, cos, sin, w_in, b_in, q_norm, k_norm, gla_w2, gla_b2, gla_norm,
                lam_re, lam_im, log_step, s5_b_re, s5_b_im, s5_c_re, s5_c_im, s5_d, glu_w, glu_b,
                w_branch, w_out, ctx_out):
    z = split_columns(h @ w_in + b_in)
    zc = split_columns(hc @ w_in + b_in)
    y_att, yc_att = gqa_branch(z[0], z[1], z[2], zc[0], zc[1], zc[2], q_norm, k_norm, cos, sin, ctx_out)
    y_gla, yc_gla = gla_branch(z[3:8], zc[3:8], gla_w2, gla_b2, gla_norm, ctx_out)
    y_ret, yc_ret = retention_branch(z[8:12], zc[8:12], cos, sin, ctx_out)
    y_s5, yc_s5 = s5_branch(z[12], zc[12], lam_re, lam_im, log_step, s5_b_re, s5_b_im, s5_c_re, s5_c_im,
                            s5_d, glu_w, glu_b, ctx_out)
    m = merge_branches((y_gla, y_att, y_ret, y_s5), z[13], w_branch, w_out)
    mc = merge_branches((yc_gla, yc_att, yc_ret, yc_s5), zc[13], w_branch, w_out) if ctx_out else None
    return m, mc


def swiglu(t, w1, w3, w2):
    return (jax.nn.silu(t @ w1) * (t @ w3)) @ w2


def moe_swiglu(t, router, w1, w3, w2):
    shp = t.shape
    tok = t.reshape(-1, shp[-1])
    logits = (tok @ router).astype(F32)
    top_v, top_i = lax.top_k(logits, TOP_K)
    top_w = jax.nn.softmax(top_v, axis=-1)
    gates = jnp.sum(jax.nn.one_hot(top_i, N_EXPERTS, dtype=F32) * top_w[..., None], axis=1)
    out = jnp.zeros(tok.shape, F32)
    for e in range(N_EXPERTS):
        out = out + gates[:, e:e + 1] * swiglu(tok, w1[e], w3[e], w2[e]).astype(F32)
    return out.reshape(shp).astype(t.dtype)


def channel_mixer(t, layer, ffn_w1, ffn_w3, ffn_w2, moe_router, moe_w1, moe_w3, moe_w2):
    i = layer // 2
    if layer % 2 == 0:
        return swiglu(t, ffn_w1[i], ffn_w3[i], ffn_w2[i])
    return moe_swiglu(t, moe_router[i], moe_w1[i], moe_w3[i], moe_w2[i])


def setup_inputs(seed: int = 0) -> dict:
    key = jax.random.key(seed)
    keys = iter(jax.random.split(key, 40))

    def nrm(shape, scale):
        return jax.random.normal(next(keys), shape, jnp.float32) * scale

    def gain(shape):
        return 1.0 + nrm(shape, 0.02)

    D, L = D_MODEL, DEPTH
    s5_shape = (L, 2, S5_GROUPS, S5_STATE)
    return {
        'x': nrm((BATCH, SEQ, D), 1.0),
        'c': nrm((BATCH, D), 1.0),
        'ctx': nrm((BATCH, CTX_LEN, D), 1.0),
        'c_ctx': nrm((D,), 1.0),
        'w_ada': nrm((L, D, 6 * D), 0.5 * D ** -0.5),
        'b_ada': nrm((L, 6 * D), 0.02),
        'w_in': nrm((L, D, IN_WIDTH), D ** -0.5),
        'b_in': nrm((L, IN_WIDTH), 0.02),
        'attn_q_norm': gain((L, ATT_HEAD_DIM)),
        'attn_k_norm': gain((L, ATT_HEAD_DIM)),
        'gla_w2': nrm((L, 2, GLA_RANK, GLA_HEADS * GLA_DK), GLA_RANK ** -0.5),
        'gla_b2': nrm((L, 2, GLA_HEADS * GLA_DK), 0.02),
        'gla_norm': gain((L, GLA_DV)),
        's5_lambda_re': -0.5 + nrm(s5_shape, 0.01),
        's5_lambda_im': math.pi * jnp.arange(S5_STATE, dtype=jnp.float32) + nrm(s5_shape, 0.01),
        's5_log_step': jax.random.uniform(next(keys), (L, 2, S5_GROUPS), jnp.float32,
                                          math.log(S5_DT_MIN), math.log(S5_DT_MAX)),
        's5_b_re': nrm((L, 2, S5_GROUPS, S5_STATE, S5_GROUP), (2 * S5_GROUP) ** -0.5),
        's5_b_im': nrm((L, 2, S5_GROUPS, S5_STATE, S5_GROUP), (2 * S5_GROUP) ** -0.5),
        's5_c_re': nrm((L, 2, S5_GROUPS, S5_GROUP, S5_STATE), (2 * S5_STATE) ** -0.5),
        's5_c_im': nrm((L, 2, S5_GROUPS, S5_GROUP, S5_STATE), (2 * S5_STATE) ** -0.5),
        's5_d': nrm((L, S5_GROUPS, S5_GROUP), 1.0),
        's5_glu_w': nrm((L, MIX_W, MIX_W), MIX_W ** -0.5),
        's5_glu_b': nrm((L, MIX_W), 0.02),
        'w_branch': nrm((L, N_BRANCH, MIX_W, D), MIX_W ** -0.5),
        'w_out': nrm((L, D, D), DEEPNORM_BETA * D ** -0.5),
        'ln1_g': gain((L, D)),
        'ln1_b': nrm((L, D), 0.02),
        'ln2_g': gain((L, D)),
        'ln2_b': nrm((L, D), 0.02),
        'ffn_w1': nrm((N_DENSE_LAYERS, D, FFN_DENSE), D ** -0.5),
        'ffn_w3': nrm((N_DENSE_LAYERS, D, FFN_DENSE), D ** -0.5),
        'ffn_w2': nrm((N_DENSE_LAYERS, FFN_DENSE, D), DEEPNORM_BETA * FFN_DENSE ** -0.5),
        'moe_router': nrm((N_MOE_LAYERS, D, N_EXPERTS), D ** -0.5),
        'moe_w1': nrm((N_MOE_LAYERS, N_EXPERTS, D, FFN_EXPERT), D ** -0.5),
        'moe_w3': nrm((N_MOE_LAYERS, N_EXPERTS, D, FFN_EXPERT), D ** -0.5),
        'moe_w2': nrm((N_MOE_LAYERS, N_EXPERTS, FFN_EXPERT, D), DEEPNORM_BETA * FFN_EXPERT ** -0.5),
    }


def reference(x, c, ctx, c_ctx, w_ada, b_ada, w_in, b_in, attn_q_norm, attn_k_norm, gla_w2, gla_b2,
              gla_norm, s5_lambda_re, s5_lambda_im, s5_log_step, s5_b_re, s5_b_im, s5_c_re, s5_c_im,
              s5_d, s5_glu_w, s5_glu_b, w_branch, w_out, ln1_g, ln1_b, ln2_g, ln2_b,
              ffn_w1, ffn_w3, ffn_w2, moe_router, moe_w1, moe_w3, moe_w2):
    n_lat = x.shape[1]
    ROWS = n_lat // GRID_W
    rows = jnp.repeat(jnp.arange(ROWS, dtype=F32), GRID_W)
    cols = (jnp.arange(ROWS * GRID_W) % GRID_W).astype(F32)
    cos, sin = axial_rope_tables(rows, cols, ATT_HEAD_DIM)
    sc = jax.nn.silu(c)
    sc_ctx = jax.nn.silu(c_ctx)
    xc = ctx
    for l in range(DEPTH):
        ctx_out = l < DEPTH - 1
        mod = jnp.split((sc @ w_ada[l] + b_ada[l])[:, None, :], 6, axis=-1)
        mod_c = jnp.split(sc_ctx @ w_ada[l] + b_ada[l], 6, axis=-1)
        h = x * (1 + mod[1]) + mod[0]
        hc = xc * (1 + mod_c[1]) + mod_c[0]
        m, mc = token_mixer(h, hc, cos, sin, w_in[l], b_in[l], attn_q_norm[l], attn_k_norm[l],
                            gla_w2[l], gla_b2[l], gla_norm[l], s5_lambda_re[l], s5_lambda_im[l],
                            s5_log_step[l], s5_b_re[l], s5_b_im[l], s5_c_re[l], s5_c_im[l], s5_d[l],
                            s5_glu_w[l], s5_glu_b[l], w_branch[l], w_out[l], ctx_out)
        x = layer_norm(DEEPNORM_ALPHA * x + mod[2] * m, ln1_g[l], ln1_b[l])
        h = x * (1 + mod[4]) + mod[3]
        f = channel_mixer(h, l, ffn_w1, ffn_w3, ffn_w2, moe_router, moe_w1, moe_w3, moe_w2)
        x = layer_norm(DEEPNORM_ALPHA * x + mod[5] * f, ln2_g[l], ln2_b[l])
        if ctx_out:
            xc = layer_norm(DEEPNORM_ALPHA * xc + mod_c[2] * mc, ln1_g[l], ln1_b[l])
            hc = xc * (1 + mod_c[4]) + mod_c[3]
            fc = channel_mixer(hc, l, ffn_w1, ffn_w3, ffn_w2, moe_router, moe_w1, moe_w3, moe_w2)
            xc = layer_norm(DEEPNORM_ALPHA * xc + mod_c[5] * fc, ln2_g[l], ln2_b[l])
    return x
```

```python
import functools
import math

import jax
import jax.numpy as jnp
from jax import lax
from jax.experimental import pallas as pl
from jax.experimental.pallas import tpu as pltpu

F32 = jnp.float32
BF16 = jnp.bfloat16

V7X_VMEM_BYTES = 64 * 1024 * 1024
VMEM_LIMIT = V7X_VMEM_BYTES - 8 * 1024 * 1024
TILE_BUDGET = 44 * 1024 * 1024
LANES = 128

HEAD_DIM = 128
ATT_KV_HEADS = 2
REC_HEADS = 4
GLA_RANK = 16
GLA_GATE_NORM = 16.0
CHUNK = 64
S5_GROUP = 16
S5_STATE = 64
N_BRANCH = 4
TOP_K = 2
ROPE_THETA = 10000.0
GRID_W = 64
EPS = 1e-6
MOD_ROWS = 16


def _cparams(sem):
    return pltpu.CompilerParams(dimension_semantics=sem, vmem_limit_bytes=VMEM_LIMIT)


def _tile(n, prefs):
    for t in prefs:
        if n % t == 0:
            return t
    return n


def _mm_tiles(m, k, n, a_bytes, w_bytes, o_bytes, n_w=1):
    for tm, tn in ((1024, 1024), (1024, 512), (512, 1024), (512, 512), (256, 512), (256, 256), (128, 256),
                   (128, 128), (64, 128), (32, 128), (16, 128), (8, 128)):
        if m % tm or n % tn:
            continue
        need = 2 * (tm * k * a_bytes + n_w * k * tn * w_bytes + tm * tn * o_bytes) + n_w * tm * tn * 4
        if need <= TILE_BUDGET:
            return tm, tn
    raise ValueError(f"no matmul tiling for {(m, k, n)}")


class _Layout:
    def __init__(self, d):
        mw = d // 4
        hk = REC_HEADS * HEAD_DIM
        kv = ATT_KV_HEADS * HEAD_DIM
        src_sizes = (mw, kv, kv, hk, hk, mw, mw, 2 * GLA_RANK, hk, hk, mw, mw, mw, N_BRANCH * d)
        names = ("att_q", "att_k", "att_v", "gla_q", "gla_k", "gla_v", "gla_g", "gla_lr",
                 "ret_q", "ret_k", "ret_v", "ret_g", "s5_u", "gates")
        self.src = {}
        off = 0
        for nm, sz in zip(names, src_sizes):
            self.src[nm] = (off, sz)
            off += sz
        self.in_width = off
        order = ("s5_u", "att_q", "gla_v", "gla_g", "ret_v", "ret_g", "gla_q", "gla_k", "ret_q", "ret_k",
                 "att_k", "att_v", "gla_lr")
        self.off = {}
        self.plan = []
        pos = 0
        for nm in order:
            width = max(self.src[nm][1], LANES)
            self.off[nm] = pos
            self.plan.append((nm, self.src[nm][1]))
            if width > self.src[nm][1]:
                self.plan.append((None, width - self.src[nm][1]))
            pos += width
        small = -(-pos // 1024) * 1024
        if small > pos:
            self.plan.append((None, small - pos))
        self.small = small
        self.off["gates"] = small
        self.plan.append(("gates", self.src["gates"][1]))
        self.width = small + self.src["gates"][1]

    def blk(self, name, width):
        off = self.off[name]
        assert off % width == 0, (name, off, width)
        return off // width

    def permute(self, w):
        parts = []
        for nm, sz in self.plan:
            if nm is None:
                parts.append(jnp.zeros(w.shape[:-1] + (sz,), w.dtype))
            else:
                o = self.src[nm][0]
                parts.append(w[..., o:o + sz])
        return jnp.concatenate(parts, axis=-1)


def _mm_kernel(*refs, has_bias, a_silu, gate_lane):
    it = iter(refs)
    a_ref, w_ref = next(it), next(it)
    b_ref = next(it) if has_bias else None
    if gate_lane is not None:
        prev_ref, gates_ref = next(it), next(it)
    o_ref = next(it)
    a = a_ref[...]
    if a_silu:
        a = a * jax.nn.sigmoid(a)
    acc = jnp.dot(a.astype(BF16), w_ref[...].astype(BF16), preferred_element_type=F32)
    if has_bias:
        acc = acc + b_ref[...]
    if gate_lane is not None:
        acc = prev_ref[...] + gates_ref[:, gate_lane:gate_lane + 1] * acc
    o_ref[...] = acc.astype(o_ref.dtype)


def _matmul(a, w, bias=None, *, n_out=None, out_dtype=F32, a_silu=False, w_index=None, accum=None, name):
    m, k = a.shape
    n = n_out or w.shape[-1]
    tm, tn = _mm_tiles(m, k, n, a.dtype.itemsize, w.dtype.itemsize, jnp.dtype(out_dtype).itemsize)
    if w_index is None:
        w_spec = pl.BlockSpec((k, tn), lambda i, j: (0, j))
    else:
        w_spec = pl.BlockSpec((None, k, tn), lambda i, j: (w_index, 0, j))
    in_specs = [pl.BlockSpec((tm, k), lambda i, j: (i, 0)), w_spec]
    args = [a, w]
    if bias is not None:
        in_specs.append(pl.BlockSpec((1, tn), lambda i, j: (0, j)))
        args.append(bias.reshape(1, -1))
    aliases = {}
    gate_lane = None
    if accum is not None:
        prev, gates, gate_lane = accum
        aliases = {len(args): 0}
        in_specs += [pl.BlockSpec((tm, tn), lambda i, j: (i, j)), pl.BlockSpec((tm, LANES), lambda i, j: (i, 0))]
        args += [prev, gates]
    return pl.pallas_call(
        functools.partial(_mm_kernel, has_bias=bias is not None, a_silu=a_silu, gate_lane=gate_lane),
        out_shape=jax.ShapeDtypeStruct((m, n), out_dtype),
        grid=(m // tm, n // tn),
        in_specs=in_specs,
        out_specs=pl.BlockSpec((tm, tn), lambda i, j: (i, j)),
        input_output_aliases=aliases,
        compiler_params=_cparams(("parallel", "arbitrary")),
        name=name,
    )(*args)


def _group_map(rows_per_group, tm, g0, chunk):
    tiles = rows_per_group // tm if rows_per_group else None

    def index_map(i):
        return ((i // tiles + g0) if tiles else g0, 0, chunk)
    return index_map


def _modulate_kernel(x_ref, sc_ref, sh_ref, o_ref):
    o_ref[...] = (x_ref[...] * (1.0 + sc_ref[0]) + sh_ref[0]).astype(o_ref.dtype)


def _modulate(x, mods, *, rows_per_group, g0, scale_chunk, shift_chunk, name):
    m, d = x.shape
    tm = _tile(rows_per_group or m, (512, 256, 128, 64, 32, 16, 8))
    return pl.pallas_call(
        _modulate_kernel,
        out_shape=jax.ShapeDtypeStruct((m, d), BF16),
        grid=(m // tm,),
        in_specs=[pl.BlockSpec((tm, d), lambda i: (i, 0)),
                  pl.BlockSpec((1, 1, d), _group_map(rows_per_group, tm, g0, scale_chunk)),
                  pl.BlockSpec((1, 1, d), _group_map(rows_per_group, tm, g0, shift_chunk))],
        out_specs=pl.BlockSpec((tm, d), lambda i: (i, 0)),
        compiler_params=_cparams(("parallel",)),
        name=name,
    )(x, mods, mods)


def _ln_res_kernel(*refs, alpha, emit_h):
    if emit_h:
        x_ref, m_ref, gate_ref, g_ref, b_ref, sc_ref, sh_ref, xo_ref, ho_ref = refs
    else:
        x_ref, m_ref, gate_ref, g_ref, b_ref, xo_ref = refs
    y = alpha * x_ref[...] + gate_ref[0] * m_ref[...]
    mu = jnp.mean(y, axis=-1, keepdims=True)
    dlt = y - mu
    var = jnp.mean(dlt * dlt, axis=-1, keepdims=True)
    xn = dlt * lax.rsqrt(var + EPS) * g_ref[...] + b_ref[...]
    xo_ref[...] = xn
    if emit_h:
        ho_ref[...] = (xn * (1.0 + sc_ref[0]) + sh_ref[0]).astype(ho_ref.dtype)


def _ln_res(x, mres, mods, ln_g, ln_b, *, alpha, rows_per_group, g0, gate_chunk, next_mods=None,
            scale_chunk=None, shift_chunk=None, name):
    m, d = x.shape
    tm = _tile(rows_per_group or m, (256, 128, 64, 32, 16, 8))
    emit_h = next_mods is not None
    row = pl.BlockSpec((tm, d), lambda i: (i, 0))
    vec = pl.BlockSpec((1, d), lambda i: (0, 0))
    in_specs = [row, row, pl.BlockSpec((1, 1, d), _group_map(rows_per_group, tm, g0, gate_chunk)), vec, vec]
    args = [x, mres, mods, ln_g.reshape(1, d), ln_b.reshape(1, d)]
    out_shape = [jax.ShapeDtypeStruct((m, d), F32)]
    out_specs = [row]
    if emit_h:
        in_specs += [pl.BlockSpec((1, 1, d), _group_map(rows_per_group, tm, g0, scale_chunk)),
                     pl.BlockSpec((1, 1, d), _group_map(rows_per_group, tm, g0, shift_chunk))]
        args += [next_mods, next_mods]
        out_shape.append(jax.ShapeDtypeStruct((m, d), BF16))
        out_specs.append(row)
    out = pl.pallas_call(
        functools.partial(_ln_res_kernel, alpha=alpha, emit_h=emit_h),
        out_shape=out_shape, grid=(m // tm,), in_specs=in_specs, out_specs=out_specs,
        compiler_params=_cparams(("parallel",)), name=name,
    )(*args)
    return out if emit_h else (out[0], None)


def _rope_tables(t):
    nf = HEAD_DIM // 4
    rows = jnp.repeat(jnp.arange(t // GRID_W, dtype=F32), GRID_W)
    cols = (jnp.arange(t) % GRID_W).astype(F32)
    inv = ROPE_THETA ** (-jnp.arange(nf, dtype=F32) / nf)
    ar, ac = rows[:, None] * inv, cols[:, None] * inv
    cr, sr, cc, sc = jnp.cos(ar), jnp.sin(ar), jnp.cos(ac), jnp.sin(ac)
    zero = jnp.zeros_like(sr)
    cos = jnp.concatenate([cr, cr, cc, cc], axis=-1)
    sin_a = jnp.concatenate([-sr, zero, -sc, zero], axis=-1)
    sin_b = jnp.concatenate([zero, sr, zero, sc], axis=-1)
    return cos, sin_a, sin_b


def _rope(x, cos, sin_a, sin_b):
    return x * cos + pltpu.roll(x, 96, 1) * sin_a + pltpu.roll(x, 32, 1) * sin_b


def _rms(x, gain):
    return x * lax.rsqrt(jnp.mean(x * x, axis=-1, keepdims=True) + EPS) * gain


def _attn_kernel(*refs, n_rep, tc, t, has_lat):
    it = iter(refs)
    q_ref, kc_ref, vc_ref, qg_ref, kg_ref = (next(it) for _ in range(5))
    if has_lat:
        kl_ref, vl_ref, cosk, sak, sbk, cosq, saq, sbq = (next(it) for _ in range(8))
    o_ref, k_scr, v_scr = next(it), next(it), next(it)

    @pl.when(pl.program_id(2) == 0)
    def _():
        k_scr[0:tc, :] = _rms(kc_ref[...], kg_ref[...]).astype(BF16)
        v_scr[0:tc, :] = vc_ref[...].astype(BF16)
        if has_lat:
            kl = _rope(_rms(kl_ref[...], kg_ref[...]), cosk[...], sak[...], sbk[...])
            k_scr[tc:tc + t, :] = kl.astype(BF16)
            v_scr[tc:tc + t, :] = vl_ref[...].astype(BF16)

    scale = HEAD_DIM ** -0.5
    keys = k_scr[...]
    vals = v_scr[...]
    for r in range(n_rep):
        q = _rms(q_ref[:, r * HEAD_DIM:(r + 1) * HEAD_DIM], qg_ref[...])
        if has_lat:
            q = _rope(q, cosq[...], saq[...], sbq[...])
        s = lax.dot_general((q * scale).astype(BF16), keys, (((1,), (1,)), ((), ())), preferred_element_type=F32)
        e = jnp.exp(s - jnp.max(s, axis=-1, keepdims=True))
        den = jnp.sum(e, axis=-1, keepdims=True)
        o = jnp.dot(e.astype(BF16), vals, preferred_element_type=F32) / den
        o_ref[:, r * HEAD_DIM:(r + 1) * HEAD_DIM] = o.astype(o_ref.dtype)


def _attention(zq, zc, zl, lay, q_norm, k_norm, rope, *, batch, tq_rows, tc, t, name):
    has_lat = zl is not None
    n_rep = (lay.src["att_q"][1] // HEAD_DIM) // ATT_KV_HEADS
    qw = n_rep * HEAD_DIM
    tq = _tile(tq_rows, (256, 128, 64, 32, 16, 8))
    nq = tq_rows // tq
    qb, kb, vb = lay.blk("att_q", qw), lay.blk("att_k", HEAD_DIM), lay.blk("att_v", HEAD_DIM)
    vec = pl.BlockSpec((1, HEAD_DIM), lambda b, g, i: (0, 0))
    in_specs = [pl.BlockSpec((tq, qw), lambda b, g, i: (b * nq + i, qb + g)),
                pl.BlockSpec((tc, HEAD_DIM), lambda b, g, i: (b, kb + g)),
                pl.BlockSpec((tc, HEAD_DIM), lambda b, g, i: (b, vb + g)), vec, vec]
    args = [zq, zc, zc, q_norm.reshape(1, -1), k_norm.reshape(1, -1)]
    n_keys = tc
    if has_lat:
        full = pl.BlockSpec((t, HEAD_DIM), lambda b, g, i: (0, 0))
        part = pl.BlockSpec((tq, HEAD_DIM), lambda b, g, i: (i, 0))
        in_specs += [pl.BlockSpec((t, HEAD_DIM), lambda b, g, i: (b, kb + g)),
                     pl.BlockSpec((t, HEAD_DIM), lambda b, g, i: (b, vb + g)), full, full, full, part, part, part]
        args += [zl, zl, *rope, *rope]
        n_keys = tc + t
    return pl.pallas_call(
        functools.partial(_attn_kernel, n_rep=n_rep, tc=tc, t=t, has_lat=has_lat),
        out_shape=jax.ShapeDtypeStruct((batch * tq_rows, lay.src["att_q"][1]), BF16),
        grid=(batch, ATT_KV_HEADS, nq),
        in_specs=in_specs,
        out_specs=pl.BlockSpec((tq, qw), lambda b, g, i: (b * nq + i, g)),
        scratch_shapes=[pltpu.VMEM((n_keys, HEAD_DIM), BF16), pltpu.VMEM((n_keys, HEAD_DIM), BF16)],
        compiler_params=_cparams(("parallel", "parallel", "arbitrary")),
        name=name,
    )(*args)


def _split3(x):
    hi = x.astype(BF16)
    r1 = x - hi.astype(F32)
    mid = r1.astype(BF16)
    lo = (r1 - mid.astype(F32)).astype(BF16)
    return hi, mid, lo


def _chunk_step(q_s, k_s, v_ref, g_s, o_s, st_s, c, *, forward, want_out):
    r0 = pl.multiple_of(c * CHUNK, CHUNK)
    rows = pl.ds(r0, CHUNK)
    q, k, g = q_s[rows, :], k_s[rows, :], g_s[rows, :]
    v = v_ref[rows, :].astype(BF16)
    ri = lax.broadcasted_iota(jnp.int32, (CHUNK, CHUNK), 0)
    ci = lax.broadcasted_iota(jnp.int32, (CHUNK, CHUNK), 1)
    tri = (ri >= ci).astype(BF16)
    g1, g2, g3 = _split3(g)
    cum = (jnp.dot(tri, g1, preferred_element_type=F32) + jnp.dot(tri, g2, preferred_element_type=F32)
           + jnp.dot(tri, g3, preferred_element_type=F32))
    if forward:
        a = cum
        ref, tot = a[CHUNK // 2:CHUNK // 2 + 1, :], a[CHUNK - 1:CHUNK, :]
        mask = ri >= ci
    else:
        a = cum[CHUNK - 1:CHUNK, :] - cum + g
        ref, tot = a[CHUNK // 2 - 1:CHUNK // 2, :], a[0:1, :]
        mask = ci > ri
    st = st_s[...]
    if want_out:
        qe = (q * jnp.exp(a - ref)).astype(BF16)
        ke = (k * jnp.exp(ref - a)).astype(BF16)
        s = lax.dot_general(qe, ke, (((1,), (1,)), ((), ())), preferred_element_type=F32)
        s = jnp.where(mask, s, 0.0).astype(BF16)
        o = jnp.dot(s, v, preferred_element_type=F32)
        qd = (q * jnp.exp(a)).astype(BF16)
        o = o + lax.dot_general(qd, st.astype(BF16), (((1,), (1,)), ((), ())), preferred_element_type=F32)
        o_s[rows, :] = o
    kd = (k * jnp.exp(tot - a)).astype(BF16)
    dst = lax.dot_general(v, kd, (((0,), (0,)), ((), ())), preferred_element_type=F32)
    st_s[...] = st * jnp.exp(tot) + dst


def _rec_kernel(*refs, kind, ctx_out, t, tc):
    it = iter(refs)
    ql, kl, vl, gl, qc, kc, vc, gc = (next(it) for _ in range(8))
    if kind == "gla":
        lrl, lrc, w2, b2, norm = (next(it) for _ in range(5))
    else:
        lg, cos, sa, sb = (next(it) for _ in range(4))
    yl = next(it)
    yc = next(it) if ctx_out else None
    (qs_l, ks_l, gf_l, gb_l, of_l, ob_l, qs_c, ks_c, gf_c, gb_c, of_c, ob_c, st_f, st_b) = (next(it) for _ in range(14))
    scale = HEAD_DIM ** -0.5

    if kind == "gla":
        for lr, gf, gb in ((lrl, gf_l, gb_l), (lrc, gf_c, gb_c)):
            logits = jnp.dot(lr[...].astype(BF16), w2[...], preferred_element_type=F32) + b2[...]
            lsig = (jnp.minimum(logits, 0.0) - jnp.log1p(jnp.exp(-jnp.abs(logits)))) * (1.0 / GLA_GATE_NORM)
            gf[...] = lsig[:, :HEAD_DIM]
            gb[...] = lsig[:, HEAD_DIM:]
        qs_l[...] = ql[...] * scale
        ks_l[...] = kl[...]
        qs_c[...] = qc[...] * scale
        ks_c[...] = kc[...]
    else:
        for gf, gb in ((gf_l, gb_l), (gf_c, gb_c)):
            gf[...] = jnp.broadcast_to(lg[...], gf.shape)
            gb[...] = jnp.broadcast_to(lg[...], gb.shape)
        qs_l[...] = _rope(ql[...], cos[...], sa[...], sb[...])
        ks_l[...] = _rope(kl[...], cos[...], sa[...], sb[...]) * scale
        qs_c[...] = qc[...]
        ks_c[...] = kc[...] * scale
    st_f[...] = jnp.zeros_like(st_f)
    st_b[...] = jnp.zeros_like(st_b)

    def run(qs, ks, v_ref, gf, gb, of, ob, n, want_out):
        def body(i, carry):
            _chunk_step(qs, ks, v_ref, gf, of, st_f, i, forward=True, want_out=want_out)
            _chunk_step(qs, ks, v_ref, gb, ob, st_b, n - 1 - i, forward=False, want_out=want_out)
            return carry
        lax.fori_loop(0, n, body, 0)

    run(qs_c, ks_c, vc, gf_c, gb_c, of_c, ob_c, tc // CHUNK, ctx_out)
    run(qs_l, ks_l, vl, gf_l, gb_l, of_l, ob_l, t // CHUNK, True)

    def finish(of, ob, g_ref, y_ref):
        o = of[...] + ob[...]
        if kind == "gla":
            o = _rms(o, norm[...])
        else:
            mu = jnp.mean(o, axis=-1, keepdims=True)
            dlt = o - mu
            o = dlt * lax.rsqrt(jnp.mean(dlt * dlt, axis=-1, keepdims=True) + EPS)
        g = g_ref[...]
        y_ref[...] = (o * (g * jax.nn.sigmoid(g))).astype(y_ref.dtype)

    finish(of_l, ob_l, gl, yl)
    if ctx_out:
        finish(of_c, ob_c, gc, yc)


def _recurrence(zl, zc, lay, kind, params, *, batch, t, tc, ctx_out, name):
    dv = lay.src[kind + "_v"][1] // REC_HEADS
    qb, kb = lay.blk(kind + "_q", HEAD_DIM), lay.blk(kind + "_k", HEAD_DIM)
    vb, gb = lay.blk(kind + "_v", dv), lay.blk(kind + "_g", dv)

    def blocks(rows):
        return [pl.BlockSpec((rows, HEAD_DIM), lambda b, h: (b, qb + h)),
                pl.BlockSpec((rows, HEAD_DIM), lambda b, h: (b, kb + h)),
                pl.BlockSpec((rows, dv), lambda b, h: (b, vb + h)),
                pl.BlockSpec((rows, dv), lambda b, h: (b, gb + h))]

    in_specs = blocks(t) + blocks(tc)
    args = [zl] * 4 + [zc] * 4
    if kind == "gla":
        w2cat, b2cat, norm = params
        lrb = lay.blk("gla_lr", LANES)
        in_specs += [pl.BlockSpec((t, LANES), lambda b, h: (b, lrb)), pl.BlockSpec((tc, LANES), lambda b, h: (b, lrb)),
                     pl.BlockSpec((None, LANES, 2 * HEAD_DIM), lambda b, h: (h, 0, 0)),
                     pl.BlockSpec((None, 1, 2 * HEAD_DIM), lambda b, h: (h, 0, 0)),
                     pl.BlockSpec((1, dv), lambda b, h: (0, 0))]
        args += [zl, zc, w2cat, b2cat, norm]
    else:
        lg, rope = params
        full = pl.BlockSpec((t, HEAD_DIM), lambda b, h: (0, 0))
        in_specs += [pl.BlockSpec((None, 1, HEAD_DIM), lambda b, h: (h, 0, 0)), full, full, full]
        args += [lg, *rope]
    out_shape = [jax.ShapeDtypeStruct((batch * t, REC_HEADS * dv), BF16)]
    out_specs = [pl.BlockSpec((t, dv), lambda b, h: (b, h))]
    if ctx_out:
        out_shape.append(jax.ShapeDtypeStruct((batch * tc, REC_HEADS * dv), BF16))
        out_specs.append(pl.BlockSpec((tc, dv), lambda b, h: (b, h)))

    def per_seq(rows):
        return [pltpu.VMEM((rows, HEAD_DIM), F32)] * 4 + [pltpu.VMEM((rows, dv), F32)] * 2

    out = pl.pallas_call(
        functools.partial(_rec_kernel, kind=kind, ctx_out=ctx_out, t=t, tc=tc),
        out_shape=out_shape, grid=(batch, REC_HEADS), in_specs=in_specs, out_specs=out_specs,
        scratch_shapes=per_seq(t) + per_seq(tc) + [pltpu.VMEM((dv, HEAD_DIM), F32)] * 2,
        compiler_params=_cparams(("parallel", "parallel")), name=name,
    )(*args)
    return out[0], (out[1] if ctx_out else None)


S5_LANE_BLOCK = 1024
S5_TIME_BLOCK = 32


def _s5_prep_kernel(lr_ref, li_ref, ls_ref, br_ref, bi_ref, ar_ref, ai_ref, bbr_ref, bbi_ref):
    lam_r, lam_i = lr_ref[0], li_ref[0]
    dt = jnp.exp(ls_ref[0])
    mag = jnp.exp(lam_r * dt)
    a_r = mag * jnp.cos(lam_i * dt)
    a_i = mag * jnp.sin(lam_i * dt)
    den = lam_r * lam_r + lam_i * lam_i
    z_r = ((a_r - 1.0) * lam_r + a_i * lam_i) / den
    z_i = (a_i * lam_r - (a_r - 1.0) * lam_i) / den
    ar_ref[0] = jnp.broadcast_to(a_r, ar_ref.shape[1:])
    ai_ref[0] = jnp.broadcast_to(a_i, ai_ref.shape[1:])
    nblk, _, cols = br_ref.shape[1:]
    for j in range(nblk):
        zr = z_r[:, j * cols:(j + 1) * cols]
        zi = z_i[:, j * cols:(j + 1) * cols]
        bbr_ref[0, j] = (zr * br_ref[0, j] - zi * bi_ref[0, j]).astype(BF16)
        bbi_ref[0, j] = (zr * bi_ref[0, j] + zi * br_ref[0, j]).astype(BF16)


def _block_diag(w, n_blk):
    *lead, g, r, c = w.shape
    gl = g // n_blk
    w = w.reshape(*lead, n_blk, gl, r, c)
    eye = jnp.eye(gl, dtype=w.dtype)
    out = w[..., :, :, None, :] * eye[:, None, :, None]
    return out.reshape(*lead, n_blk, gl * r, gl * c)


def _s5_prep(lam_re, lam_im, log_step, b_re, b_im, batch):
    _, g, p = lam_re.shape
    n_blk = (g * p) // S5_LANE_BLOCK
    flat = lambda a: a.reshape(2, 1, g * p)
    ls = jnp.broadcast_to(log_step[:, :, None], (2, g, p))
    brd = _block_diag(jnp.swapaxes(b_re, -1, -2), n_blk)
    bid = _block_diag(jnp.swapaxes(b_im, -1, -2), n_blk)
    vec = pl.BlockSpec((1, 1, g * p), lambda d: (d, 0, 0))
    mat = pl.BlockSpec((1,) + brd.shape[1:], lambda d: (d, 0, 0, 0))
    tab = pl.BlockSpec((1, batch, g * p), lambda d: (d, 0, 0))
    return pl.pallas_call(
        _s5_prep_kernel,
        out_shape=[jax.ShapeDtypeStruct((2, batch, g * p), F32)] * 2 + [jax.ShapeDtypeStruct(brd.shape, BF16)] * 2,
        grid=(2,), in_specs=[vec, vec, vec, mat, mat], out_specs=[tab, tab, mat, mat],
        compiler_params=_cparams(("parallel",)), name="s5_prep",
    )(flat(lam_re), flat(lam_im), flat(ls), brd, bid)


def _s5_kernel(uf_ref, ub_ref, ar_ref, ai_ref, bbr_ref, bbi_ref, cr_ref, ci_ref, yf_ref, yb_ref,
               xr_scr, xi_scr, sr_scr, si_scr, *, batch, tb):
    n_blk, kin, _ = bbr_ref.shape[1:]
    kout = cr_ref.shape[3]

    @pl.when(pl.program_id(0) == 0)
    def _():
        sr_scr[...] = jnp.zeros_like(sr_scr)
        si_scr[...] = jnp.zeros_like(si_scr)

    for d, (u_ref, y_ref) in enumerate(((uf_ref, yf_ref), (ub_ref, yb_ref))):
        u = u_ref[...].astype(BF16)
        for j in range(n_blk):
            lanes = slice(j * S5_LANE_BLOCK, (j + 1) * S5_LANE_BLOCK)
            uj = u[:, j * kin:(j + 1) * kin]
            xr_scr[:, lanes] = jnp.dot(uj, bbr_ref[d, j], preferred_element_type=F32)
            xi_scr[:, lanes] = jnp.dot(uj, bbi_ref[d, j], preferred_element_type=F32)
        for j in range(n_blk):
            lanes = slice(j * S5_LANE_BLOCK, (j + 1) * S5_LANE_BLOCK)
            a_r, a_i = ar_ref[d, :, lanes], ai_ref[d, :, lanes]

            def step(s, carry, lanes=lanes, a_r=a_r, a_i=a_i, d=d):
                s_r, s_i = carry
                tt = s if d == 0 else tb - 1 - s
                rows = pl.ds(pl.multiple_of(tt * batch, batch), batch)
                n_r = a_r * s_r - a_i * s_i + xr_scr[rows, lanes]
                n_i = a_r * s_i + a_i * s_r + xi_scr[rows, lanes]
                xr_scr[rows, lanes] = n_r
                xi_scr[rows, lanes] = n_i
                return n_r, n_i

            s_r, s_i = lax.fori_loop(0, tb, step, (sr_scr[d, :, lanes], si_scr[d, :, lanes]), unroll=4)
            sr_scr[d, :, lanes] = s_r
            si_scr[d, :, lanes] = s_i
        for j in range(n_blk):
            lanes = slice(j * S5_LANE_BLOCK, (j + 1) * S5_LANE_BLOCK)
            y = (jnp.dot(xr_scr[:, lanes].astype(BF16), cr_ref[d, j], preferred_element_type=F32)
                 - jnp.dot(xi_scr[:, lanes].astype(BF16), ci_ref[d, j], preferred_element_type=F32))
            y_ref[:, j * kout:(j + 1) * kout] = y


def _s5_scan(u_tm, tabs, c_re_bd, c_im_bd, *, batch, t, tc):
    a_r, a_i, bbr, bbi = tabs
    rows, width = u_tm.shape
    tb = _tile(math.gcd(t, tc), (S5_TIME_BLOCK, 16, 8, 4, 2, 1))
    nbc, nbl = tc // tb, t // tb
    nb = nbc + nbl
    blk = tb * batch

    def bwd(i):
        return jnp.where(i < nbc, nbc - 1 - i, nb - 1 - (i - nbc))

    full4 = lambda a: pl.BlockSpec(a.shape, lambda i: (0, 0, 0, 0))
    full3 = lambda a: pl.BlockSpec(a.shape, lambda i: (0, 0, 0))
    n_state = a_r.shape[-1]
    return pl.pallas_call(
        functools.partial(_s5_kernel, batch=batch, tb=tb),
        out_shape=[jax.ShapeDtypeStruct((rows, width), F32)] * 2,
        grid=(nb,),
        in_specs=[pl.BlockSpec((blk, width), lambda i: (i, 0)), pl.BlockSpec((blk, width), lambda i: (bwd(i), 0)),
                  full3(a_r), full3(a_i), full4(bbr), full4(bbi), full4(c_re_bd), full4(c_im_bd)],
        out_specs=[pl.BlockSpec((blk, width), lambda i: (i, 0)), pl.BlockSpec((blk, width), lambda i: (bwd(i), 0))],
        scratch_shapes=[pltpu.VMEM((blk, n_state), F32)] * 2 + [pltpu.VMEM((2, batch, n_state), F32)] * 2,
        compiler_params=_cparams(("arbitrary",)), name="s5_scan",
    )(u_tm, u_tm, a_r, a_i, bbr, bbi, c_re_bd, c_im_bd)


def _glu_kernel(u_ref, yf_ref, yb_ref, d_ref, w_ref, b_ref, o_ref):
    y = u_ref[...] * d_ref[...] + yf_ref[...] + yb_ref[...]
    z = jax.nn.gelu(y, approximate=True)
    gate = jnp.dot(z.astype(BF16), w_ref[...], preferred_element_type=F32) + b_ref[...]
    o_ref[...] = (z * jax.nn.sigmoid(gate)).astype(o_ref.dtype)


def _s5_glu(u_tm, yf, yb, d, glu_w, glu_b):
    rows, width = u_tm.shape
    tm = _tile(rows, (512, 256, 128, 64, 32, 16, 8))
    row = pl.BlockSpec((tm, width), lambda i: (i, 0))
    vec = pl.BlockSpec((1, width), lambda i: (0, 0))
    return pl.pallas_call(
        _glu_kernel, out_shape=jax.ShapeDtypeStruct((rows, width), BF16), grid=(rows // tm,),
        in_specs=[row, row, row, vec, pl.BlockSpec((width, width), lambda i: (0, 0)), vec], out_specs=row,
        compiler_params=_cparams(("parallel",)), name="s5_glu",
    )(u_tm, yf, yb, d.reshape(1, width), glu_w, glu_b.reshape(1, width))


def _merge_kernel(y0, y1, y2, y3, g0, g1, g2, g3, wb_ref, o_ref):
    acc = None
    for i, (y_ref, g_ref) in enumerate(((y0, g0), (y1, g1), (y2, g2), (y3, g3))):
        term = jax.nn.sigmoid(g_ref[...]) * jnp.dot(y_ref[...], wb_ref[i], preferred_element_type=F32)
        acc = term if acc is None else acc + term
    o_ref[...] = acc.astype(o_ref.dtype)


def _merge(ys, z, lay, w_branch, *, name):
    m, mw = ys[0].shape
    d = w_branch.shape[-1]
    tm = _tile(m, (512, 256, 128, 64, 32, 16, 8))
    tn = _tile(d, (512, 256, 128))
    g_off = lay.off["gates"]
    assert g_off % tn == 0

    def gate_spec(i):
        return pl.BlockSpec((tm, tn), lambda r, c: (r, (g_off + i * d) // tn + c))

    yspec = pl.BlockSpec((tm, mw), lambda r, c: (r, 0))
    return pl.pallas_call(
        _merge_kernel, out_shape=jax.ShapeDtypeStruct((m, d), BF16), grid=(m // tm, d // tn),
        in_specs=[yspec] * N_BRANCH + [gate_spec(i) for i in range(N_BRANCH)]
        + [pl.BlockSpec((N_BRANCH, mw, tn), lambda r, c: (0, 0, c))],
        out_specs=pl.BlockSpec((tm, tn), lambda r, c: (r, c)),
        compiler_params=_cparams(("parallel", "arbitrary")), name=name,
    )(*ys, z, z, z, z, w_branch)


def _ffn_up_kernel(h_ref, w1_ref, w3_ref, o_ref):
    h = h_ref[...]
    a = jnp.dot(h, w1_ref[...], preferred_element_type=F32)
    b = jnp.dot(h, w3_ref[...], preferred_element_type=F32)
    o_ref[...] = (a * jax.nn.sigmoid(a) * b).astype(o_ref.dtype)


def _ffn_up(h, w1, w3, *, w_index=None, name):
    m, k = h.shape
    f = w1.shape[-1]
    tm, tn = _mm_tiles(m, k, f, 2, 2, 2, n_w=2)
    if w_index is None:
        w_spec = pl.BlockSpec((k, tn), lambda i, j: (0, j))
    else:
        w_spec = pl.BlockSpec((None, k, tn), lambda i, j: (w_index, 0, j))
    return pl.pallas_call(
        _ffn_up_kernel, out_shape=jax.ShapeDtypeStruct((m, f), BF16), grid=(m // tm, f // tn),
        in_specs=[pl.BlockSpec((tm, k), lambda i, j: (i, 0)), w_spec, w_spec],
        out_specs=pl.BlockSpec((tm, tn), lambda i, j: (i, j)),
        compiler_params=_cparams(("parallel", "arbitrary")), name=name,
    )(h, w1, w3)


def _router_kernel(x_ref, sc_ref, sh_ref, rh_ref, rl_ref, o_ref, *, n_experts):
    h = x_ref[...] * (1.0 + sc_ref[0]) + sh_ref[0]
    hh = h.astype(BF16)
    hl = (h - hh.astype(F32)).astype(BF16)
    logits = (jnp.dot(hh, rh_ref[...], preferred_element_type=F32) + jnp.dot(hh, rl_ref[...], preferred_element_type=F32)
              + jnp.dot(hl, rh_ref[...], preferred_element_type=F32))
    lane = lax.broadcasted_iota(jnp.int32, logits.shape, 1).astype(F32)
    neg = jnp.float32(-jnp.inf)
    x1 = jnp.where(lane < n_experts, logits, neg)
    m1 = jnp.max(x1, axis=-1, keepdims=True)
    i1 = jnp.min(jnp.where(x1 == m1, lane, float(LANES)), axis=-1, keepdims=True)
    x2 = jnp.where(lane == i1, neg, x1)
    m2 = jnp.max(x2, axis=-1, keepdims=True)
    i2 = jnp.min(jnp.where(x2 == m2, lane, float(LANES)), axis=-1, keepdims=True)
    e2 = jnp.exp(m2 - m1)
    w1 = 1.0 / (1.0 + e2)
    o_ref[...] = jnp.where(lane == i1, w1, 0.0) + jnp.where(lane == i2, e2 * w1, 0.0)


def _router(x, mods, router, *, rows_per_group, g0, scale_chunk, shift_chunk):
    m, d = x.shape
    n_experts = router.shape[-1]
    rp = jnp.zeros((d, LANES), F32).at[:, :n_experts].set(router)
    rh = rp.astype(BF16)
    rl = (rp - rh.astype(F32)).astype(BF16)
    tm = _tile(rows_per_group or m, (256, 128, 64, 32, 16, 8))
    mat = pl.BlockSpec((d, LANES), lambda i: (0, 0))
    return pl.pallas_call(
        functools.partial(_router_kernel, n_experts=n_experts),
        out_shape=jax.ShapeDtypeStruct((m, LANES), F32), grid=(m // tm,),
        in_specs=[pl.BlockSpec((tm, d), lambda i: (i, 0)),
                  pl.BlockSpec((1, 1, d), _group_map(rows_per_group, tm, g0, scale_chunk)),
                  pl.BlockSpec((1, 1, d), _group_map(rows_per_group, tm, g0, shift_chunk)), mat, mat],
        out_specs=pl.BlockSpec((tm, LANES), lambda i: (i, 0)),
        compiler_params=_cparams(("parallel",)), name="moe_router",
    )(x, mods, mods, rh, rl)


def kernel(x, c, ctx, c_ctx, w_ada, b_ada, w_in, b_in, attn_q_norm, attn_k_norm, gla_w2, gla_b2, gla_norm,
           s5_lambda_re, s5_lambda_im, s5_log_step, s5_b_re, s5_b_im, s5_c_re, s5_c_im, s5_d, s5_glu_w, s5_glu_b,
           w_branch, w_out, ln1_g, ln1_b, ln2_g, ln2_b, ffn_w1, ffn_w3, ffn_w2, moe_router, moe_w1, moe_w3, moe_w2):
    batch, t, d = x.shape
    tc = ctx.shape[1]
    depth = w_ada.shape[0]
    assert batch + 1 <= MOD_ROWS and t % CHUNK == 0 and tc % CHUNK == 0 and t % GRID_W == 0
    lay = _Layout(d)
    assert lay.in_width == w_in.shape[-1]
    alpha = (2 * depth) ** 0.25
    rope = _rope_tables(t)
    log_gamma = jnp.log1p(-jnp.exp2(-5.0 - jnp.arange(REC_HEADS, dtype=F32)))
    lg_tab = jnp.broadcast_to(log_gamma[:, None, None], (REC_HEADS, 1, HEAD_DIM))
    n_blk = (s5_lambda_re.shape[2] * S5_STATE) // S5_LANE_BLOCK

    xl = x.reshape(batch * t, d)
    xc = ctx.reshape(batch * tc, d)
    cvec = jnp.concatenate([c, c_ctx[None], jnp.zeros((MOD_ROWS - batch - 1, d), F32)], axis=0)

    def mods_of(l):
        return _matmul(cvec, w_ada[l], b_ada[l], a_silu=True, name=f"ada{l}").reshape(MOD_ROWS, 1, 6 * d)

    mods = mods_of(0)
    hl = _modulate(xl, mods, rows_per_group=t, g0=0, scale_chunk=1, shift_chunk=0, name="mod_lat")
    hc = _modulate(xc, mods, rows_per_group=0, g0=batch, scale_chunk=1, shift_chunk=0, name="mod_ctx")

    for l in range(depth):
        ctx_out = l < depth - 1
        w_in_p = lay.permute(w_in[l]).astype(BF16)
        b_in_p = lay.permute(b_in[l])
        zl = _matmul(hl, w_in_p, b_in_p, name=f"in_lat{l}")
        zc = _matmul(hc, w_in_p, b_in_p, n_out=None if ctx_out else lay.small, name=f"in_ctx{l}")

        y_att = _attention(zl, zc, zl, lay, attn_q_norm[l], attn_k_norm[l], rope, batch=batch, tq_rows=t, tc=tc, t=t,
                           name=f"attn_lat{l}")
        yc_att = (_attention(zc, zc, None, lay, attn_q_norm[l], attn_k_norm[l], None, batch=batch, tq_rows=tc, tc=tc,
                             t=t, name=f"attn_ctx{l}") if ctx_out else None)

        w2h = gla_w2[l].reshape(2, GLA_RANK, REC_HEADS, HEAD_DIM)
        w2cat = jnp.zeros((REC_HEADS, LANES, 2 * HEAD_DIM), F32)
        for r in range(2):
            w2cat = w2cat.at[:, r * GLA_RANK:(r + 1) * GLA_RANK, r * HEAD_DIM:(r + 1) * HEAD_DIM].set(
                jnp.transpose(w2h[r], (1, 0, 2)))
        b2cat = jnp.transpose(gla_b2[l].reshape(2, REC_HEADS, HEAD_DIM), (1, 0, 2)).reshape(REC_HEADS, 1, 2 * HEAD_DIM)
        y_gla, yc_gla = _recurrence(zl, zc, lay, "gla", (w2cat.astype(BF16), b2cat, gla_norm[l].reshape(1, -1)),
                                    batch=batch, t=t, tc=tc, ctx_out=ctx_out, name=f"gla{l}")
        y_ret, yc_ret = _recurrence(zl, zc, lay, "ret", (lg_tab, rope), batch=batch, t=t, tc=tc, ctx_out=ctx_out,
                                    name=f"ret{l}")

        mw = lay.src["s5_u"][1]
        u0 = lay.off["s5_u"]
        u_tm = jnp.concatenate([jnp.swapaxes(zc[:, u0:u0 + mw].reshape(batch, tc, mw), 0, 1),
                                jnp.swapaxes(zl[:, u0:u0 + mw].reshape(batch, t, mw), 0, 1)], axis=0)
        u_tm = u_tm.reshape((tc + t) * batch, mw)
        tabs = _s5_prep(s5_lambda_re[l], s5_lambda_im[l], s5_log_step[l], s5_b_re[l], s5_b_im[l], batch)
        c_re_bd = _block_diag(jnp.swapaxes(s5_c_re[l], -1, -2), n_blk).astype(BF16)
        c_im_bd = _block_diag(jnp.swapaxes(s5_c_im[l], -1, -2), n_blk).astype(BF16)
        yf, yb = _s5_scan(u_tm, tabs, c_re_bd, c_im_bd, batch=batch, t=t, tc=tc)
        y_s5_tm = _s5_glu(u_tm, yf, yb, s5_d[l], s5_glu_w[l].astype(BF16), s5_glu_b[l])
        y_s5_bm = jnp.swapaxes(y_s5_tm.reshape(tc + t, batch, mw), 0, 1)
        y_s5 = y_s5_bm[:, tc:].reshape(batch * t, mw)
        yc_s5 = y_s5_bm[:, :tc].reshape(batch * tc, mw)

        wb = w_branch[l].astype(BF16)
        wo = w_out[l].astype(BF16)
        ml = _matmul(_merge((y_gla, y_att, y_ret, y_s5), zl, lay, wb, name=f"merge_lat{l}"), wo, name=f"out_lat{l}")
        xl, hl = _ln_res(xl, ml, mods, ln1_g[l], ln1_b[l], alpha=alpha, rows_per_group=t, g0=0, gate_chunk=2,
                         next_mods=mods, scale_chunk=4, shift_chunk=3, name=f"ln1_lat{l}")
        if ctx_out:
            mc = _matmul(_merge((yc_gla, yc_att, yc_ret, yc_s5), zc, lay, wb, name=f"merge_ctx{l}"), wo,
                         name=f"out_ctx{l}")
            xc, hc = _ln_res(xc, mc, mods, ln1_g[l], ln1_b[l], alpha=alpha, rows_per_group=0, g0=batch, gate_chunk=2,
                             next_mods=mods, scale_chunk=4, shift_chunk=3, name=f"ln1_ctx{l}")

        i = l // 2
        if l % 2 == 0:
            w1, w3, w2 = ffn_w1[i].astype(BF16), ffn_w3[i].astype(BF16), ffn_w2[i].astype(BF16)
            fl = _matmul(_ffn_up(hl, w1, w3, name=f"ffn_up_lat{l}"), w2, name=f"ffn_dn_lat{l}")
            fc = (_matmul(_ffn_up(hc, w1, w3, name=f"ffn_up_ctx{l}"), w2, name=f"ffn_dn_ctx{l}") if ctx_out else None)
        else:
            w1, w3, w2 = moe_w1[i].astype(BF16), moe_w3[i].astype(BF16), moe_w2[i].astype(BF16)

            def moe(xs, hs, rows_per_group, g0):
                gates = _router(xs, mods, moe_router[i], rows_per_group=rows_per_group, g0=g0, scale_chunk=4,
                                shift_chunk=3)
                f = jnp.zeros(xs.shape, F32)
                for e in range(w1.shape[0]):
                    up = _ffn_up(hs, w1, w3, w_index=e, name=f"moe_up{l}_{e}")
                    f = _matmul(up, w2, w_index=e, accum=(f, gates, e), name=f"moe_dn{l}_{e}")
                return f

            fl = moe(xl, hl, t, 0)
            fc = moe(xc, hc, 0, batch) if ctx_out else None

        next_mods = mods_of(l + 1) if l + 1 < depth else None
        xl, hl = _ln_res(xl, fl, mods, ln2_g[l], ln2_b[l], alpha=alpha, rows_per_group=t, g0=0, gate_chunk=5,
                         next_mods=next_mods, scale_chunk=1, shift_chunk=0, name=f"ln2_lat{l}")
        if ctx_out:
            xc, hc = _ln_res(xc, fc, mods, ln2_g[l], ln2_b[l], alpha=alpha, rows_per_group=0, g0=batch, gate_chunk=5,
                             next_mods=next_mods, scale_chunk=1, shift_chunk=0, name=f"ln2_ctx{l}")
        mods = next_mods

    return xl.reshape(batch, t, d)
```

```python
import functools
import math

import jax
import jax.numpy as jnp
from jax import lax
from jax.experimental import pallas as pl
from jax.experimental.pallas import tpu as pltpu

F32 = jnp.float32
BF16 = jnp.bfloat16

V7X_VMEM_BYTES = 64 * 1024 * 1024
VMEM_LIMIT = V7X_VMEM_BYTES - 8 * 1024 * 1024
TILE_BUDGET = 44 * 1024 * 1024
LANES = 128

HEAD_DIM = 128
ATT_KV_HEADS = 2
REC_HEADS = 4
GLA_RANK = 16
GLA_GATE_NORM = 16.0
CHUNK = 64
REC_UNROLL = 4
S5_GROUP = 16
S5_STATE = 64
N_BRANCH = 4
TOP_K = 2
ROPE_THETA = 10000.0
GRID_W = 64
EPS = 1e-6
MOD_ROWS = 16


def _cparams(sem):
    return pltpu.CompilerParams(dimension_semantics=sem, vmem_limit_bytes=VMEM_LIMIT)


def _tile(n, prefs):
    for t in prefs:
        if n % t == 0:
            return t
    return n


def _mm_tiles(m, k, n, a_bytes, w_bytes, o_bytes, n_w=1):
    for tm, tn in ((1024, 1024), (1024, 512), (512, 1024), (512, 512), (256, 512), (256, 256), (128, 256),
                   (128, 128), (64, 128), (32, 128), (16, 128), (8, 128)):
        if m % tm or n % tn:
            continue
        need = 2 * (tm * k * a_bytes + n_w * k * tn * w_bytes + tm * tn * o_bytes) + n_w * tm * tn * 4
        if need <= TILE_BUDGET:
            return tm, tn
    raise ValueError(f"no matmul tiling for {(m, k, n)}")


class _Layout:
    def __init__(self, d):
        mw = d // 4
        hk = REC_HEADS * HEAD_DIM
        kv = ATT_KV_HEADS * HEAD_DIM
        src_sizes = (mw, kv, kv, hk, hk, mw, mw, 2 * GLA_RANK, hk, hk, mw, mw, mw, N_BRANCH * d)
        names = ("att_q", "att_k", "att_v", "gla_q", "gla_k", "gla_v", "gla_g", "gla_lr",
                 "ret_q", "ret_k", "ret_v", "ret_g", "s5_u", "gates")
        self.src = {}
        off = 0
        for nm, sz in zip(names, src_sizes):
            self.src[nm] = (off, sz)
            off += sz
        self.in_width = off
        order = ("s5_u", "att_q", "gla_v", "gla_g", "ret_v", "ret_g", "gla_q", "gla_k", "ret_q", "ret_k",
                 "att_k", "att_v", "gla_lr")
        self.off = {}
        self.plan = []
        pos = 0
        for nm in order:
            width = max(self.src[nm][1], LANES)
            self.off[nm] = pos
            self.plan.append((nm, self.src[nm][1]))
            if width > self.src[nm][1]:
                self.plan.append((None, width - self.src[nm][1]))
            pos += width
        small = -(-pos // 1024) * 1024
        if small > pos:
            self.plan.append((None, small - pos))
        self.small = small
        self.off["gates"] = small
        self.plan.append(("gates", self.src["gates"][1]))
        self.width = small + self.src["gates"][1]

    def blk(self, name, width):
        off = self.off[name]
        assert off % width == 0, (name, off, width)
        return off // width

    def permute(self, w):
        parts = []
        for nm, sz in self.plan:
            if nm is None:
                parts.append(jnp.zeros(w.shape[:-1] + (sz,), w.dtype))
            else:
                o = self.src[nm][0]
                parts.append(w[..., o:o + sz])
        return jnp.concatenate(parts, axis=-1)


def _mm_kernel(*refs, has_bias, a_silu):
    if has_bias:
        a_ref, w_ref, b_ref, o_ref = refs
    else:
        a_ref, w_ref, o_ref = refs
    a = a_ref[...]
    if a_silu:
        a = a * jax.nn.sigmoid(a)
    acc = jnp.dot(a.astype(BF16), w_ref[...].astype(BF16), preferred_element_type=F32)
    if has_bias:
        acc = acc + b_ref[...]
    o_ref[...] = acc.astype(o_ref.dtype)


def _matmul(a, w, bias=None, *, n_out=None, out_dtype=F32, a_silu=False, name):
    m, k = a.shape
    n = n_out or w.shape[-1]
    tm, tn = _mm_tiles(m, k, n, a.dtype.itemsize, w.dtype.itemsize, jnp.dtype(out_dtype).itemsize)
    in_specs = [pl.BlockSpec((tm, k), lambda i, j: (i, 0)), pl.BlockSpec((k, tn), lambda i, j: (0, j))]
    args = [a, w]
    if bias is not None:
        in_specs.append(pl.BlockSpec((1, tn), lambda i, j: (0, j)))
        args.append(bias.reshape(1, -1))
    return pl.pallas_call(
        functools.partial(_mm_kernel, has_bias=bias is not None, a_silu=a_silu),
        out_shape=jax.ShapeDtypeStruct((m, n), out_dtype),
        grid=(m // tm, n // tn),
        in_specs=in_specs,
        out_specs=pl.BlockSpec((tm, tn), lambda i, j: (i, j)),
        compiler_params=_cparams(("parallel", "arbitrary")),
        name=name,
    )(*args)


def _group_map(rows_per_group, tm, g0, chunk):
    tiles = rows_per_group // tm if rows_per_group else None

    def index_map(i):
        return ((i // tiles + g0) if tiles else g0, 0, chunk)
    return index_map


def _modulate_kernel(x_ref, sc_ref, sh_ref, o_ref):
    o_ref[...] = (x_ref[...] * (1.0 + sc_ref[0]) + sh_ref[0]).astype(o_ref.dtype)


def _modulate(x, mods, *, rows_per_group, g0, scale_chunk, shift_chunk, name):
    m, d = x.shape
    tm = _tile(rows_per_group or m, (512, 256, 128, 64, 32, 16, 8))
    return pl.pallas_call(
        _modulate_kernel,
        out_shape=jax.ShapeDtypeStruct((m, d), BF16),
        grid=(m // tm,),
        in_specs=[pl.BlockSpec((tm, d), lambda i: (i, 0)),
                  pl.BlockSpec((1, 1, d), _group_map(rows_per_group, tm, g0, scale_chunk)),
                  pl.BlockSpec((1, 1, d), _group_map(rows_per_group, tm, g0, shift_chunk))],
        out_specs=pl.BlockSpec((tm, d), lambda i: (i, 0)),
        compiler_params=_cparams(("parallel",)),
        name=name,
    )(x, mods, mods)


def _layer_norm(y, gain, bias):
    mu = jnp.mean(y, axis=-1, keepdims=True)
    dlt = y - mu
    var = jnp.mean(dlt * dlt, axis=-1, keepdims=True)
    return dlt * lax.rsqrt(var + EPS) * gain + bias


def _ln_res_kernel(*refs, alpha, emit_h):
    if emit_h:
        x_ref, m_ref, gate_ref, g_ref, b_ref, sc_ref, sh_ref, xo_ref, ho_ref = refs
    else:
        x_ref, m_ref, gate_ref, g_ref, b_ref, xo_ref = refs
    xn = _layer_norm(alpha * x_ref[...] + gate_ref[0] * m_ref[...], g_ref[...], b_ref[...])
    xo_ref[...] = xn
    if emit_h:
        ho_ref[...] = (xn * (1.0 + sc_ref[0]) + sh_ref[0]).astype(ho_ref.dtype)


def _ln_res(x, mres, mods, ln_g, ln_b, *, alpha, rows_per_group, g0, gate_chunk, next_mods=None,
            scale_chunk=None, shift_chunk=None, h_dtype=BF16, name):
    m, d = x.shape
    tm = _tile(rows_per_group or m, (256, 128, 64, 32, 16, 8))
    emit_h = next_mods is not None
    row = pl.BlockSpec((tm, d), lambda i: (i, 0))
    vec = pl.BlockSpec((1, d), lambda i: (0, 0))
    in_specs = [row, row, pl.BlockSpec((1, 1, d), _group_map(rows_per_group, tm, g0, gate_chunk)), vec, vec]
    args = [x, mres, mods, ln_g.reshape(1, d), ln_b.reshape(1, d)]
    out_shape = [jax.ShapeDtypeStruct((m, d), F32)]
    out_specs = [row]
    if emit_h:
        in_specs += [pl.BlockSpec((1, 1, d), _group_map(rows_per_group, tm, g0, scale_chunk)),
                     pl.BlockSpec((1, 1, d), _group_map(rows_per_group, tm, g0, shift_chunk))]
        args += [next_mods, next_mods]
        out_shape.append(jax.ShapeDtypeStruct((m, d), h_dtype))
        out_specs.append(row)
    out = pl.pallas_call(
        functools.partial(_ln_res_kernel, alpha=alpha, emit_h=emit_h),
        out_shape=out_shape, grid=(m // tm,), in_specs=in_specs, out_specs=out_specs,
        compiler_params=_cparams(("parallel",)), name=name,
    )(*args)
    return out if emit_h else (out[0], None)


def _rope_tables(t):
    nf = HEAD_DIM // 4
    rows = jnp.repeat(jnp.arange(t // GRID_W, dtype=F32), GRID_W)
    cols = (jnp.arange(t) % GRID_W).astype(F32)
    inv = ROPE_THETA ** (-jnp.arange(nf, dtype=F32) / nf)
    ar, ac = rows[:, None] * inv, cols[:, None] * inv
    cr, sr, cc, sc = jnp.cos(ar), jnp.sin(ar), jnp.cos(ac), jnp.sin(ac)
    zero = jnp.zeros_like(sr)
    cos = jnp.concatenate([cr, cr, cc, cc], axis=-1)
    sin_a = jnp.concatenate([-sr, zero, -sc, zero], axis=-1)
    sin_b = jnp.concatenate([zero, sr, zero, sc], axis=-1)
    return cos, sin_a, sin_b


def _rope(x, cos, sin_a, sin_b):
    return x * cos + pltpu.roll(x, 96, 1) * sin_a + pltpu.roll(x, 32, 1) * sin_b


def _rms(x, gain):
    return x * lax.rsqrt(jnp.mean(x * x, axis=-1, keepdims=True) + EPS) * gain


def _attn_kernel(*refs, n_rep, tc, t, has_lat):
    it = iter(refs)
    q_ref, kc_ref, vc_ref, qg_ref, kg_ref = (next(it) for _ in range(5))
    if has_lat:
        kl_ref, vl_ref, cosk, sak, sbk, cosq, saq, sbq = (next(it) for _ in range(8))
    o_ref, k_scr, v_scr = next(it), next(it), next(it)

    @pl.when(pl.program_id(2) == 0)
    def _():
        k_scr[0:tc, :] = _rms(kc_ref[...], kg_ref[...]).astype(BF16)
        v_scr[0:tc, :] = vc_ref[...].astype(BF16)
        if has_lat:
            kl = _rope(_rms(kl_ref[...], kg_ref[...]), cosk[...], sak[...], sbk[...])
            k_scr[tc:tc + t, :] = kl.astype(BF16)
            v_scr[tc:tc + t, :] = vl_ref[...].astype(BF16)

    scale = HEAD_DIM ** -0.5
    keys = k_scr[...]
    vals = v_scr[...]
    for r in range(n_rep):
        q = _rms(q_ref[:, r * HEAD_DIM:(r + 1) * HEAD_DIM], qg_ref[...])
        if has_lat:
            q = _rope(q, cosq[...], saq[...], sbq[...])
        s = lax.dot_general((q * scale).astype(BF16), keys, (((1,), (1,)), ((), ())), preferred_element_type=F32)
        e = jnp.exp(s - jnp.max(s, axis=-1, keepdims=True))
        den = jnp.sum(e, axis=-1, keepdims=True)
        o = jnp.dot(e.astype(BF16), vals, preferred_element_type=F32) / den
        o_ref[:, r * HEAD_DIM:(r + 1) * HEAD_DIM] = o.astype(o_ref.dtype)


def _attention(zq, zc, zl, lay, q_norm, k_norm, rope, *, batch, tq_rows, tc, t, name):
    has_lat = zl is not None
    n_rep = (lay.src["att_q"][1] // HEAD_DIM) // ATT_KV_HEADS
    qw = n_rep * HEAD_DIM
    tq = _tile(tq_rows, (256, 128, 64, 32, 16, 8))
    nq = tq_rows // tq
    qb, kb, vb = lay.blk("att_q", qw), lay.blk("att_k", HEAD_DIM), lay.blk("att_v", HEAD_DIM)
    vec = pl.BlockSpec((1, HEAD_DIM), lambda b, g, i: (0, 0))
    in_specs = [pl.BlockSpec((tq, qw), lambda b, g, i: (b * nq + i, qb + g)),
                pl.BlockSpec((tc, HEAD_DIM), lambda b, g, i: (b, kb + g)),
                pl.BlockSpec((tc, HEAD_DIM), lambda b, g, i: (b, vb + g)), vec, vec]
    args = [zq, zc, zc, q_norm.reshape(1, -1), k_norm.reshape(1, -1)]
    n_keys = tc
    if has_lat:
        full = pl.BlockSpec((t, HEAD_DIM), lambda b, g, i: (0, 0))
        part = pl.BlockSpec((tq, HEAD_DIM), lambda b, g, i: (i, 0))
        in_specs += [pl.BlockSpec((t, HEAD_DIM), lambda b, g, i: (b, kb + g)),
                     pl.BlockSpec((t, HEAD_DIM), lambda b, g, i: (b, vb + g)), full, full, full, part, part, part]
        args += [zl, zl, *rope, *rope]
        n_keys = tc + t
    return pl.pallas_call(
        functools.partial(_attn_kernel, n_rep=n_rep, tc=tc, t=t, has_lat=has_lat),
        out_shape=jax.ShapeDtypeStruct((batch * tq_rows, lay.src["att_q"][1]), BF16),
        grid=(batch, ATT_KV_HEADS, nq),
        in_specs=in_specs,
        out_specs=pl.BlockSpec((tq, qw), lambda b, g, i: (b * nq + i, g)),
        scratch_shapes=[pltpu.VMEM((n_keys, HEAD_DIM), BF16), pltpu.VMEM((n_keys, HEAD_DIM), BF16)],
        compiler_params=_cparams(("parallel", "parallel", "arbitrary")),
        name=name,
    )(*args)


def _split3(x):
    hi = x.astype(BF16)
    r1 = x - hi.astype(F32)
    mid = r1.astype(BF16)
    lo = (r1 - mid.astype(F32)).astype(BF16)
    return hi, mid, lo


def _chunk_step(q_s, k_s, v_ref, g_s, o_s, st_s, c, *, forward, want_out):
    r0 = pl.multiple_of(c * CHUNK, CHUNK)
    rows = pl.ds(r0, CHUNK)
    q, k, g = q_s[rows, :], k_s[rows, :], g_s[rows, :]
    v = v_ref[rows, :].astype(BF16)
    ri = lax.broadcasted_iota(jnp.int32, (CHUNK, CHUNK), 0)
    ci = lax.broadcasted_iota(jnp.int32, (CHUNK, CHUNK), 1)
    tri = (ri >= ci).astype(BF16)
    g1, g2, g3 = _split3(g)
    cum = (jnp.dot(tri, g1, preferred_element_type=F32) + jnp.dot(tri, g2, preferred_element_type=F32)
           + jnp.dot(tri, g3, preferred_element_type=F32))
    if forward:
        a = cum
        ref, tot = a[CHUNK // 2:CHUNK // 2 + 1, :], a[CHUNK - 1:CHUNK, :]
        mask = ri >= ci
    else:
        a = cum[CHUNK - 1:CHUNK, :] - cum + g
        ref, tot = a[CHUNK // 2 - 1:CHUNK // 2, :], a[0:1, :]
        mask = ci > ri
    st = st_s[...]
    if want_out:
        qe = (q * jnp.exp(a - ref)).astype(BF16)
        ke = (k * jnp.exp(ref - a)).astype(BF16)
        s = lax.dot_general(qe, ke, (((1,), (1,)), ((), ())), preferred_element_type=F32)
        s = jnp.where(mask, s, 0.0).astype(BF16)
        o = jnp.dot(s, v, preferred_element_type=F32)
        qd = (q * jnp.exp(a)).astype(BF16)
        o = o + lax.dot_general(qd, st.astype(BF16), (((1,), (1,)), ((), ())), preferred_element_type=F32)
        o_s[rows, :] = o
    kd = (k * jnp.exp(tot - a)).astype(BF16)
    dst = lax.dot_general(v, kd, (((0,), (0,)), ((), ())), preferred_element_type=F32)
    st_s[...] = st * jnp.exp(tot) + dst


def _rec_kernel(*refs, kind, ctx_out, t, tc):
    it = iter(refs)
    ql, kl, vl, gl, qc, kc, vc, gc = (next(it) for _ in range(8))
    if kind == "gla":
        lrl, lrc, w2, b2, norm = (next(it) for _ in range(5))
    else:
        lg, cos, sa, sb = (next(it) for _ in range(4))
    yl = next(it)
    yc = next(it) if ctx_out else None
    (qs_l, ks_l, gf_l, gb_l, of_l, ob_l, qs_c, ks_c, gf_c, gb_c, of_c, ob_c, st_f, st_b) = (next(it) for _ in range(14))
    scale = HEAD_DIM ** -0.5

    if kind == "gla":
        for lr, gf, gb in ((lrl, gf_l, gb_l), (lrc, gf_c, gb_c)):
            logits = jnp.dot(lr[...].astype(BF16), w2[...], preferred_element_type=F32) + b2[...]
            lsig = (jnp.minimum(logits, 0.0) - jnp.log1p(jnp.exp(-jnp.abs(logits)))) * (1.0 / GLA_GATE_NORM)
            gf[...] = lsig[:, :HEAD_DIM]
            gb[...] = lsig[:, HEAD_DIM:]
        qs_l[...] = ql[...] * scale
        ks_l[...] = kl[...]
        qs_c[...] = qc[...] * scale
        ks_c[...] = kc[...]
    else:
        for gf, gb in ((gf_l, gb_l), (gf_c, gb_c)):
            gf[...] = jnp.broadcast_to(lg[...], gf.shape)
            gb[...] = jnp.broadcast_to(lg[...], gb.shape)
        qs_l[...] = _rope(ql[...], cos[...], sa[...], sb[...])
        ks_l[...] = _rope(kl[...], cos[...], sa[...], sb[...]) * scale
        qs_c[...] = qc[...]
        ks_c[...] = kc[...] * scale
    st_f[...] = jnp.zeros_like(st_f)
    st_b[...] = jnp.zeros_like(st_b)

    def run(qs, ks, v_ref, gf, gb, of, ob, n, want_out):
        def body(i, carry):
            _chunk_step(qs, ks, v_ref, gf, of, st_f, i, forward=True, want_out=want_out)
            _chunk_step(qs, ks, v_ref, gb, ob, st_b, n - 1 - i, forward=False, want_out=want_out)
            return carry
        lax.fori_loop(0, n, body, 0, unroll=math.gcd(n, REC_UNROLL))

    run(qs_c, ks_c, vc, gf_c, gb_c, of_c, ob_c, tc // CHUNK, ctx_out)
    run(qs_l, ks_l, vl, gf_l, gb_l, of_l, ob_l, t // CHUNK, True)

    def finish(of, ob, g_ref, y_ref):
        o = of[...] + ob[...]
        if kind == "gla":
            o = _rms(o, norm[...])
        else:
            mu = jnp.mean(o, axis=-1, keepdims=True)
            dlt = o - mu
            o = dlt * lax.rsqrt(jnp.mean(dlt * dlt, axis=-1, keepdims=True) + EPS)
        g = g_ref[...]
        y_ref[...] = (o * (g * jax.nn.sigmoid(g))).astype(y_ref.dtype)

    finish(of_l, ob_l, gl, yl)
    if ctx_out:
        finish(of_c, ob_c, gc, yc)


def _recurrence(zl, zc, lay, kind, params, *, batch, t, tc, ctx_out, name):
    dv = lay.src[kind + "_v"][1] // REC_HEADS
    qb, kb = lay.blk(kind + "_q", HEAD_DIM), lay.blk(kind + "_k", HEAD_DIM)
    vb, gb = lay.blk(kind + "_v", dv), lay.blk(kind + "_g", dv)

    def blocks(rows):
        return [pl.BlockSpec((rows, HEAD_DIM), lambda b, h: (b, qb + h)),
                pl.BlockSpec((rows, HEAD_DIM), lambda b, h: (b, kb + h)),
                pl.BlockSpec((rows, dv), lambda b, h: (b, vb + h)),
                pl.BlockSpec((rows, dv), lambda b, h: (b, gb + h))]

    in_specs = blocks(t) + blocks(tc)
    args = [zl] * 4 + [zc] * 4
    if kind == "gla":
        w2cat, b2cat, norm = params
        lrb = lay.blk("gla_lr", LANES)
        in_specs += [pl.BlockSpec((t, LANES), lambda b, h: (b, lrb)), pl.BlockSpec((tc, LANES), lambda b, h: (b, lrb)),
                     pl.BlockSpec((None, LANES, 2 * HEAD_DIM), lambda b, h: (h, 0, 0)),
                     pl.BlockSpec((None, 1, 2 * HEAD_DIM), lambda b, h: (h, 0, 0)),
                     pl.BlockSpec((1, dv), lambda b, h: (0, 0))]
        args += [zl, zc, w2cat, b2cat, norm]
    else:
        lg, rope = params
        full = pl.BlockSpec((t, HEAD_DIM), lambda b, h: (0, 0))
        in_specs += [pl.BlockSpec((None, 1, HEAD_DIM), lambda b, h: (h, 0, 0)), full, full, full]
        args += [lg, *rope]
    out_shape = [jax.ShapeDtypeStruct((batch * t, REC_HEADS * dv), BF16)]
    out_specs = [pl.BlockSpec((t, dv), lambda b, h: (b, h))]
    if ctx_out:
        out_shape.append(jax.ShapeDtypeStruct((batch * tc, REC_HEADS * dv), BF16))
        out_specs.append(pl.BlockSpec((tc, dv), lambda b, h: (b, h)))

    def per_seq(rows):
        return [pltpu.VMEM((rows, HEAD_DIM), F32)] * 4 + [pltpu.VMEM((rows, dv), F32)] * 2

    out = pl.pallas_call(
        functools.partial(_rec_kernel, kind=kind, ctx_out=ctx_out, t=t, tc=tc),
        out_shape=out_shape, grid=(batch, REC_HEADS), in_specs=in_specs, out_specs=out_specs,
        scratch_shapes=per_seq(t) + per_seq(tc) + [pltpu.VMEM((dv, HEAD_DIM), F32)] * 2,
        compiler_params=_cparams(("parallel", "parallel")), name=name,
    )(*args)
    return out[0], (out[1] if ctx_out else None)


S5_LANE_BLOCK = 1024
S5_TIME_BLOCK = 32


def _s5_prep_kernel(lr_ref, li_ref, ls_ref, br_ref, bi_ref, ar_ref, ai_ref, bbr_ref, bbi_ref):
    lam_r, lam_i = lr_ref[0], li_ref[0]
    dt = jnp.exp(ls_ref[0])
    mag = jnp.exp(lam_r * dt)
    a_r = mag * jnp.cos(lam_i * dt)
    a_i = mag * jnp.sin(lam_i * dt)
    den = lam_r * lam_r + lam_i * lam_i
    z_r = ((a_r - 1.0) * lam_r + a_i * lam_i) / den
    z_i = (a_i * lam_r - (a_r - 1.0) * lam_i) / den
    ar_ref[0] = jnp.broadcast_to(a_r, ar_ref.shape[1:])
    ai_ref[0] = jnp.broadcast_to(a_i, ai_ref.shape[1:])
    nblk, _, cols = br_ref.shape[1:]
    for j in range(nblk):
        zr = z_r[:, j * cols:(j + 1) * cols]
        zi = z_i[:, j * cols:(j + 1) * cols]
        bbr_ref[0, j] = (zr * br_ref[0, j] - zi * bi_ref[0, j]).astype(BF16)
        bbi_ref[0, j] = (zr * bi_ref[0, j] + zi * br_ref[0, j]).astype(BF16)


def _block_diag(w, n_blk):
    *lead, g, r, c = w.shape
    gl = g // n_blk
    w = w.reshape(*lead, n_blk, gl, r, c)
    eye = jnp.eye(gl, dtype=w.dtype)
    out = w[..., :, :, None, :] * eye[:, None, :, None]
    return out.reshape(*lead, n_blk, gl * r, gl * c)


def _s5_prep(lam_re, lam_im, log_step, b_re, b_im, batch):
    _, g, p = lam_re.shape
    n_blk = (g * p) // S5_LANE_BLOCK
    flat = lambda a: a.reshape(2, 1, g * p)
    ls = jnp.broadcast_to(log_step[:, :, None], (2, g, p))
    brd = _block_diag(jnp.swapaxes(b_re, -1, -2), n_blk)
    bid = _block_diag(jnp.swapaxes(b_im, -1, -2), n_blk)
    vec = pl.BlockSpec((1, 1, g * p), lambda d: (d, 0, 0))
    mat = pl.BlockSpec((1,) + brd.shape[1:], lambda d: (d, 0, 0, 0))
    tab = pl.BlockSpec((1, batch, g * p), lambda d: (d, 0, 0))
    return pl.pallas_call(
        _s5_prep_kernel,
        out_shape=[jax.ShapeDtypeStruct((2, batch, g * p), F32)] * 2 + [jax.ShapeDtypeStruct(brd.shape, BF16)] * 2,
        grid=(2,), in_specs=[vec, vec, vec, mat, mat], out_specs=[tab, tab, mat, mat],
        compiler_params=_cparams(("parallel",)), name="s5_prep",
    )(flat(lam_re), flat(lam_im), flat(ls), brd, bid)


def _s5_kernel(uf_ref, ub_ref, ar_ref, ai_ref, bbr_ref, bbi_ref, cr_ref, ci_ref, yf_ref, yb_ref,
               xr_scr, xi_scr, sr_scr, si_scr, *, batch, tb):
    n_blk, kin, _ = bbr_ref.shape[1:]
    kout = cr_ref.shape[3]

    @pl.when(pl.program_id(0) == 0)
    def _():
        sr_scr[...] = jnp.zeros_like(sr_scr)
        si_scr[...] = jnp.zeros_like(si_scr)

    for d, (u_ref, y_ref) in enumerate(((uf_ref, yf_ref), (ub_ref, yb_ref))):
        u = u_ref[...].astype(BF16)
        for j in range(n_blk):
            lanes = slice(j * S5_LANE_BLOCK, (j + 1) * S5_LANE_BLOCK)
            uj = u[:, j * kin:(j + 1) * kin]
            xr_scr[:, lanes] = jnp.dot(uj, bbr_ref[d, j], preferred_element_type=F32)
            xi_scr[:, lanes] = jnp.dot(uj, bbi_ref[d, j], preferred_element_type=F32)
        for j in range(n_blk):
            lanes = slice(j * S5_LANE_BLOCK, (j + 1) * S5_LANE_BLOCK)
            a_r, a_i = ar_ref[d, :, lanes], ai_ref[d, :, lanes]

            def step(s, carry, lanes=lanes, a_r=a_r, a_i=a_i, d=d):
                s_r, s_i = carry
                tt = s if d == 0 else tb - 1 - s
                rows = pl.ds(pl.multiple_of(tt * batch, batch), batch)
                n_r = a_r * s_r - a_i * s_i + xr_scr[rows, lanes]
                n_i = a_r * s_i + a_i * s_r + xi_scr[rows, lanes]
                xr_scr[rows, lanes] = n_r
                xi_scr[rows, lanes] = n_i
                return n_r, n_i

            s_r, s_i = lax.fori_loop(0, tb, step, (sr_scr[d, :, lanes], si_scr[d, :, lanes]), unroll=4)
            sr_scr[d, :, lanes] = s_r
            si_scr[d, :, lanes] = s_i
        for j in range(n_blk):
            lanes = slice(j * S5_LANE_BLOCK, (j + 1) * S5_LANE_BLOCK)
            y = (jnp.dot(xr_scr[:, lanes].astype(BF16), cr_ref[d, j], preferred_element_type=F32)
                 - jnp.dot(xi_scr[:, lanes].astype(BF16), ci_ref[d, j], preferred_element_type=F32))
            y_ref[:, j * kout:(j + 1) * kout] = y


def _s5_scan(u_tm, tabs, c_re_bd, c_im_bd, *, batch, t, tc):
    a_r, a_i, bbr, bbi = tabs
    rows, width = u_tm.shape
    tb = _tile(math.gcd(t, tc), (S5_TIME_BLOCK, 16, 8, 4, 2, 1))
    nbc, nbl = tc // tb, t // tb
    nb = nbc + nbl
    blk = tb * batch

    def bwd(i):
        return jnp.where(i < nbc, nbc - 1 - i, nb - 1 - (i - nbc))

    full4 = lambda a: pl.BlockSpec(a.shape, lambda i: (0, 0, 0, 0))
    full3 = lambda a: pl.BlockSpec(a.shape, lambda i: (0, 0, 0))
    n_state = a_r.shape[-1]
    return pl.pallas_call(
        functools.partial(_s5_kernel, batch=batch, tb=tb),
        out_shape=[jax.ShapeDtypeStruct((rows, width), F32)] * 2,
        grid=(nb,),
        in_specs=[pl.BlockSpec((blk, width), lambda i: (i, 0)), pl.BlockSpec((blk, width), lambda i: (bwd(i), 0)),
                  full3(a_r), full3(a_i), full4(bbr), full4(bbi), full4(c_re_bd), full4(c_im_bd)],
        out_specs=[pl.BlockSpec((blk, width), lambda i: (i, 0)), pl.BlockSpec((blk, width), lambda i: (bwd(i), 0))],
        scratch_shapes=[pltpu.VMEM((blk, n_state), F32)] * 2 + [pltpu.VMEM((2, batch, n_state), F32)] * 2,
        compiler_params=_cparams(("arbitrary",)), name="s5_scan",
    )(u_tm, u_tm, a_r, a_i, bbr, bbi, c_re_bd, c_im_bd)


def _glu_kernel(u_ref, yf_ref, yb_ref, d_ref, w_ref, b_ref, o_ref):
    y = u_ref[...] * d_ref[...] + yf_ref[...] + yb_ref[...]
    z = jax.nn.gelu(y, approximate=True)
    gate = jnp.dot(z.astype(BF16), w_ref[...], preferred_element_type=F32) + b_ref[...]
    o_ref[...] = (z * jax.nn.sigmoid(gate)).astype(o_ref.dtype)


def _s5_glu(u_tm, yf, yb, d, glu_w, glu_b):
    rows, width = u_tm.shape
    tm = _tile(rows, (512, 256, 128, 64, 32, 16, 8))
    row = pl.BlockSpec((tm, width), lambda i: (i, 0))
    vec = pl.BlockSpec((1, width), lambda i: (0, 0))
    return pl.pallas_call(
        _glu_kernel, out_shape=jax.ShapeDtypeStruct((rows, width), BF16), grid=(rows // tm,),
        in_specs=[row, row, row, vec, pl.BlockSpec((width, width), lambda i: (0, 0)), vec], out_specs=row,
        compiler_params=_cparams(("parallel",)), name="s5_glu",
    )(u_tm, yf, yb, d.reshape(1, width), glu_w, glu_b.reshape(1, width))


def _merge_kernel(y0, y1, y2, y3, g0, g1, g2, g3, wb_ref, o_ref):
    acc = None
    for i, (y_ref, g_ref) in enumerate(((y0, g0), (y1, g1), (y2, g2), (y3, g3))):
        term = jax.nn.sigmoid(g_ref[...]) * jnp.dot(y_ref[...], wb_ref[i], preferred_element_type=F32)
        acc = term if acc is None else acc + term
    o_ref[...] = acc.astype(o_ref.dtype)


def _merge(ys, z, lay, w_branch, *, name):
    m, mw = ys[0].shape
    d = w_branch.shape[-1]
    tm = _tile(m, (512, 256, 128, 64, 32, 16, 8))
    tn = _tile(d, (512, 256, 128))
    g_off = lay.off["gates"]
    assert g_off % tn == 0

    def gate_spec(i):
        return pl.BlockSpec((tm, tn), lambda r, c: (r, (g_off + i * d) // tn + c))

    yspec = pl.BlockSpec((tm, mw), lambda r, c: (r, 0))
    return pl.pallas_call(
        _merge_kernel, out_shape=jax.ShapeDtypeStruct((m, d), BF16), grid=(m // tm, d // tn),
        in_specs=[yspec] * N_BRANCH + [gate_spec(i) for i in range(N_BRANCH)]
        + [pl.BlockSpec((N_BRANCH, mw, tn), lambda r, c: (0, 0, c))],
        out_specs=pl.BlockSpec((tm, tn), lambda r, c: (r, c)),
        compiler_params=_cparams(("parallel", "arbitrary")), name=name,
    )(*ys, z, z, z, z, w_branch)


def _ffn_up_kernel(h_ref, w1_ref, w3_ref, o_ref):
    h = h_ref[...]
    a = jnp.dot(h, w1_ref[...], preferred_element_type=F32)
    b = jnp.dot(h, w3_ref[...], preferred_element_type=F32)
    o_ref[...] = (a * jax.nn.sigmoid(a) * b).astype(o_ref.dtype)


def _ffn_up(h, w1, w3, *, name):
    m, k = h.shape
    f = w1.shape[-1]
    tm, tn = _mm_tiles(m, k, f, 2, 2, 2, n_w=2)
    w_spec = pl.BlockSpec((k, tn), lambda i, j: (0, j))
    return pl.pallas_call(
        _ffn_up_kernel, out_shape=jax.ShapeDtypeStruct((m, f), BF16), grid=(m // tm, f // tn),
        in_specs=[pl.BlockSpec((tm, k), lambda i, j: (i, 0)), w_spec, w_spec],
        out_specs=pl.BlockSpec((tm, tn), lambda i, j: (i, j)),
        compiler_params=_cparams(("parallel", "arbitrary")), name=name,
    )(h, w1, w3)


MOE_ROW_TILE = 512
MOE_GATHER_TILE = 256
MOE_COMBINE_TILE = 128
DMA_UNROLL = 8


def _router_kernel(h_ref, rh_ref, rl_ref, o_ref, *, n_experts):
    h = h_ref[...]
    hh = h.astype(BF16)
    hl = (h - hh.astype(F32)).astype(BF16)
    logits = (jnp.dot(hh, rh_ref[...], preferred_element_type=F32) + jnp.dot(hh, rl_ref[...], preferred_element_type=F32)
              + jnp.dot(hl, rh_ref[...], preferred_element_type=F32))
    lane = lax.broadcasted_iota(jnp.int32, logits.shape, 1).astype(F32)
    neg = jnp.float32(-jnp.inf)
    x1 = jnp.where(lane < n_experts, logits, neg)
    m1 = jnp.max(x1, axis=-1, keepdims=True)
    i1 = jnp.min(jnp.where(x1 == m1, lane, float(LANES)), axis=-1, keepdims=True)
    x2 = jnp.where(lane == i1, neg, x1)
    m2 = jnp.max(x2, axis=-1, keepdims=True)
    i2 = jnp.min(jnp.where(x2 == m2, lane, float(LANES)), axis=-1, keepdims=True)
    e2 = jnp.exp(m2 - m1)
    w1 = 1.0 / (1.0 + e2)
    o_ref[...] = (jnp.where(lane == 0.0, w1, 0.0) + jnp.where(lane == 1.0, e2 * w1, 0.0)
                  + jnp.where(lane == 2.0, i1, 0.0) + jnp.where(lane == 3.0, i2, 0.0))


def _router(h, router):
    m, d = h.shape
    n_experts = router.shape[-1]
    rp = jnp.zeros((d, LANES), F32).at[:, :n_experts].set(router)
    rh = rp.astype(BF16)
    rl = (rp - rh.astype(F32)).astype(BF16)
    tm = _tile(m, (256, 128, 64, 32, 16, 8))
    mat = pl.BlockSpec((d, LANES), lambda i: (0, 0))
    return pl.pallas_call(
        functools.partial(_router_kernel, n_experts=n_experts),
        out_shape=jax.ShapeDtypeStruct((m, LANES), F32), grid=(m // tm,),
        in_specs=[pl.BlockSpec((tm, d), lambda i: (i, 0)), mat, mat],
        out_specs=pl.BlockSpec((tm, LANES), lambda i: (i, 0)),
        compiler_params=_cparams(("parallel",)), name="moe_router",
    )(h, rh, rl)


def _moe_plan(expert_idx, n_experts, tile):
    m = expert_idx.shape[0]
    flat = expert_idx.reshape(-1)
    onehot = (flat[:, None] == jnp.arange(n_experts, dtype=jnp.int32)[None, :]).astype(jnp.int32)
    rank = jnp.take_along_axis(jnp.cumsum(onehot, axis=0) - onehot, flat[:, None], axis=1)[:, 0]
    count = jnp.sum(onehot, axis=0)
    group = ((count + tile - 1) // tile) * tile
    group_end = jnp.cumsum(group)
    dest = (group_end - group)[flat] + rank
    rows = m * TOP_K + n_experts * tile
    row_src = jnp.zeros((rows,), jnp.int32).at[dest].set(jnp.arange(m * TOP_K, dtype=jnp.int32) // TOP_K)
    n_tiles = rows // tile
    n_used = group_end[-1] // tile
    tile_id = jnp.arange(n_tiles, dtype=jnp.int32)
    owner = jnp.searchsorted(group_end, jnp.minimum(tile_id, n_used - 1) * tile, side="right").astype(jnp.int32)
    return dest.reshape(m, TOP_K), row_src, jnp.minimum(owner, n_experts - 1), n_used.reshape(1).astype(jnp.int32)


def _row_copy(src_hbm, src_row, buf, slot, dst_row, sem):
    return pltpu.make_async_copy(src_hbm.at[pl.ds(src_row, 1), :], buf.at[slot, pl.ds(dst_row, 1), :], sem.at[slot])


def _gather_rows(idx_ref, src_hbm, buf, sem, slot, n):
    def body(r, carry):
        _row_copy(src_hbm, idx_ref[0, r], buf, slot, r, sem).start()
        return carry
    lax.fori_loop(0, n, body, 0, unroll=DMA_UNROLL)


def _wait_rows(src_hbm, buf, sem, slot, n):
    def body(r, carry):
        _row_copy(src_hbm, 0, buf, slot, r, sem).wait()
        return carry
    lax.fori_loop(0, n, body, 0, unroll=DMA_UNROLL)


def _gather_kernel(cur_ref, nxt_ref, src_hbm, o_ref, buf, sem):
    s, n_steps = pl.program_id(0), pl.num_programs(0)
    n = buf.shape[1]

    @pl.when(s == 0)
    def _():
        _gather_rows(cur_ref, src_hbm, buf, sem, 0, n)

    @pl.when(s + 1 < n_steps)
    def _():
        _gather_rows(nxt_ref, src_hbm, buf, sem, (s + 1) % 2, n)

    slot = s % 2
    _wait_rows(src_hbm, buf, sem, slot, n)
    o_ref[...] = buf[slot].astype(o_ref.dtype)


def _moe_gather(h, row_src):
    d = h.shape[1]
    rows = row_src.shape[0]
    tg = _tile(rows, (MOE_GATHER_TILE, 128, 64, 32, 16, 8))
    steps = rows // tg
    idx = row_src.reshape(steps, 1, tg)
    smem = lambda f: pl.BlockSpec((None, 1, tg), f, memory_space=pltpu.SMEM)
    return pl.pallas_call(
        _gather_kernel, out_shape=jax.ShapeDtypeStruct((rows, d), BF16), grid=(steps,),
        in_specs=[smem(lambda s: (s, 0, 0)), smem(lambda s: (jnp.minimum(s + 1, steps - 1), 0, 0)),
                  pl.BlockSpec(memory_space=pl.ANY)],
        out_specs=pl.BlockSpec((tg, d), lambda s: (s, 0)),
        scratch_shapes=[pltpu.VMEM((2, tg, d), F32), pltpu.SemaphoreType.DMA((2,))],
        compiler_params=_cparams(("arbitrary",)), name="moe_gather",
    )(idx, idx, h)


def _moe_up_kernel(te_ref, nu_ref, x_ref, w1_ref, w3_ref, o_ref):
    del te_ref

    @pl.when(pl.program_id(1) < nu_ref[0])
    def _():
        _ffn_up_kernel(x_ref, w1_ref, w3_ref, o_ref)

    @pl.when(pl.program_id(1) >= nu_ref[0])
    def _():
        o_ref[...] = jnp.zeros_like(o_ref)


def _moe_dn_kernel(te_ref, nu_ref, g_ref, w_ref, o_ref):
    del te_ref

    @pl.when(pl.program_id(1) < nu_ref[0])
    def _():
        o_ref[...] = jnp.dot(g_ref[...], w_ref[...], preferred_element_type=F32).astype(o_ref.dtype)

    @pl.when(pl.program_id(1) >= nu_ref[0])
    def _():
        o_ref[...] = jnp.zeros_like(o_ref)


def _moe_grouped(body, a, weights, tile_expert, n_used, *, tn, out_dtype, name):
    rows, k = a.shape
    n = weights[0].shape[-1]
    tm = MOE_ROW_TILE
    used = lambda i, nu: jnp.minimum(i, nu[0] - 1)
    w_spec = pl.BlockSpec((None, k, tn), lambda j, i, te, nu: (te[i], 0, j))
    return pl.pallas_call(
        body, out_shape=jax.ShapeDtypeStruct((rows, n), out_dtype),
        grid_spec=pltpu.PrefetchScalarGridSpec(
            num_scalar_prefetch=2, grid=(n // tn, rows // tm),
            in_specs=[pl.BlockSpec((tm, k), lambda j, i, te, nu: (used(i, nu), 0))] + [w_spec] * len(weights),
            out_specs=pl.BlockSpec((tm, tn), lambda j, i, te, nu: (i, j))),
        compiler_params=_cparams(("arbitrary", "arbitrary")), name=name,
    )(tile_expert, n_used, a, *weights)


def _moe_combine_kernel(*refs, alpha, emit_h):
    it = iter(refs)
    cur_ref, nxt_ref, x_ref, wts_ref, gate_ref, g_ref, b_ref = (next(it) for _ in range(7))
    if emit_h:
        sc_ref, sh_ref = next(it), next(it)
    y_hbm, xo_ref = next(it), next(it)
    ho_ref = next(it) if emit_h else None
    buf, sem = next(it), next(it)
    s, n_steps = pl.program_id(0), pl.num_programs(0)
    n = buf.shape[1]
    tm = n // TOP_K

    @pl.when(s == 0)
    def _():
        _gather_rows(cur_ref, y_hbm, buf, sem, 0, n)

    @pl.when(s + 1 < n_steps)
    def _():
        _gather_rows(nxt_ref, y_hbm, buf, sem, (s + 1) % 2, n)

    slot = s % 2
    _wait_rows(y_hbm, buf, sem, slot, n)
    f = wts_ref[:, 0:1] * buf[slot, 0:tm, :] + wts_ref[:, 1:2] * buf[slot, tm:2 * tm, :]
    xn = _layer_norm(alpha * x_ref[...] + gate_ref[0] * f, g_ref[...], b_ref[...])
    xo_ref[...] = xn
    if emit_h:
        ho_ref[...] = (xn * (1.0 + sc_ref[0]) + sh_ref[0]).astype(ho_ref.dtype)


def _moe_combine_ln(x, y_sorted, dest, wts, mods, ln_g, ln_b, *, alpha, rows_per_group, g0, gate_chunk,
                    next_mods, scale_chunk, shift_chunk, name):
    m, d = x.shape
    tm = _tile(rows_per_group or m, (MOE_COMBINE_TILE, 64, 32, 16, 8))
    steps = m // tm
    idx = jnp.swapaxes(dest.reshape(steps, tm, TOP_K), 1, 2).reshape(steps, 1, TOP_K * tm)
    emit_h = next_mods is not None
    row = pl.BlockSpec((tm, d), lambda s: (s, 0))
    vec = pl.BlockSpec((1, d), lambda s: (0, 0))
    smem = lambda f: pl.BlockSpec((None, 1, TOP_K * tm), f, memory_space=pltpu.SMEM)
    in_specs = [smem(lambda s: (s, 0, 0)), smem(lambda s: (jnp.minimum(s + 1, steps - 1), 0, 0)), row,
                pl.BlockSpec((tm, LANES), lambda s: (s, 0)),
                pl.BlockSpec((1, 1, d), _group_map(rows_per_group, tm, g0, gate_chunk)), vec, vec]
    args = [idx, idx, x, wts, mods, ln_g.reshape(1, d), ln_b.reshape(1, d)]
    out_shape = [jax.ShapeDtypeStruct((m, d), F32)]
    out_specs = [row]
    if emit_h:
        in_specs += [pl.BlockSpec((1, 1, d), _group_map(rows_per_group, tm, g0, scale_chunk)),
                     pl.BlockSpec((1, 1, d), _group_map(rows_per_group, tm, g0, shift_chunk))]
        args += [next_mods, next_mods]
        out_shape.append(jax.ShapeDtypeStruct((m, d), BF16))
        out_specs.append(row)
    in_specs.append(pl.BlockSpec(memory_space=pl.ANY))
    args.append(y_sorted)
    out = pl.pallas_call(
        functools.partial(_moe_combine_kernel, alpha=alpha, emit_h=emit_h),
        out_shape=out_shape, grid=(steps,), in_specs=in_specs, out_specs=out_specs,
        scratch_shapes=[pltpu.VMEM((2, TOP_K * tm, d), F32), pltpu.SemaphoreType.DMA((2,))],
        compiler_params=_cparams(("arbitrary",)), name=name,
    )(*args)
    return out if emit_h else (out[0], None)


def _moe_ln(x, h, mods, router, w1, w3, w2, ln_g, ln_b, *, alpha, rows_per_group, g0, next_mods, name):
    n_experts = router.shape[-1]
    route = _router(h, router)
    expert_idx = route[:, TOP_K:2 * TOP_K].astype(jnp.int32)
    dest, row_src, tile_expert, n_used = _moe_plan(expert_idx, n_experts, MOE_ROW_TILE)
    xs = _moe_gather(h, row_src)
    up = _moe_grouped(_moe_up_kernel, xs, (w1, w3), tile_expert, n_used, tn=_tile(w1.shape[-1], (1024, 512, 256, 128)),
                      out_dtype=BF16, name=name + "_up")
    ys = _moe_grouped(_moe_dn_kernel, up, (w2,), tile_expert, n_used, tn=_tile(w2.shape[-1], (1024, 512, 256, 128)),
                      out_dtype=F32, name=name + "_dn")
    return _moe_combine_ln(x, ys, dest, route, mods, ln_g, ln_b, alpha=alpha, rows_per_group=rows_per_group, g0=g0,
                           gate_chunk=5, next_mods=next_mods, scale_chunk=1, shift_chunk=0, name=name + "_ln")


def kernel(x, c, ctx, c_ctx, w_ada, b_ada, w_in, b_in, attn_q_norm, attn_k_norm, gla_w2, gla_b2, gla_norm,
           s5_lambda_re, s5_lambda_im, s5_log_step, s5_b_re, s5_b_im, s5_c_re, s5_c_im, s5_d, s5_glu_w, s5_glu_b,
           w_branch, w_out, ln1_g, ln1_b, ln2_g, ln2_b, ffn_w1, ffn_w3, ffn_w2, moe_router, moe_w1, moe_w3, moe_w2):
    batch, t, d = x.shape
    tc = ctx.shape[1]
    depth = w_ada.shape[0]
    assert batch + 1 <= MOD_ROWS and t % CHUNK == 0 and tc % CHUNK == 0 and t % GRID_W == 0
    lay = _Layout(d)
    assert lay.in_width == w_in.shape[-1]
    alpha = (2 * depth) ** 0.25
    rope = _rope_tables(t)
    log_gamma = jnp.log1p(-jnp.exp2(-5.0 - jnp.arange(REC_HEADS, dtype=F32)))
    lg_tab = jnp.broadcast_to(log_gamma[:, None, None], (REC_HEADS, 1, HEAD_DIM))
    n_blk = (s5_lambda_re.shape[2] * S5_STATE) // S5_LANE_BLOCK

    xl = x.reshape(batch * t, d)
    xc = ctx.reshape(batch * tc, d)
    cvec = jnp.concatenate([c, c_ctx[None], jnp.zeros((MOD_ROWS - batch - 1, d), F32)], axis=0)

    def mods_of(l):
        return _matmul(cvec, w_ada[l], b_ada[l], a_silu=True, name=f"ada{l}").reshape(MOD_ROWS, 1, 6 * d)

    mods = mods_of(0)
    hl = _modulate(xl, mods, rows_per_group=t, g0=0, scale_chunk=1, shift_chunk=0, name="mod_lat")
    hc = _modulate(xc, mods, rows_per_group=0, g0=batch, scale_chunk=1, shift_chunk=0, name="mod_ctx")

    for l in range(depth):
        ctx_out = l < depth - 1
        w_in_p = lay.permute(w_in[l]).astype(BF16)
        b_in_p = lay.permute(b_in[l])
        zl = _matmul(hl, w_in_p, b_in_p, name=f"in_lat{l}")
        zc = _matmul(hc, w_in_p, b_in_p, n_out=None if ctx_out else lay.small, name=f"in_ctx{l}")

        y_att = _attention(zl, zc, zl, lay, attn_q_norm[l], attn_k_norm[l], rope, batch=batch, tq_rows=t, tc=tc, t=t,
                           name=f"attn_lat{l}")
        yc_att = (_attention(zc, zc, None, lay, attn_q_norm[l], attn_k_norm[l], None, batch=batch, tq_rows=tc, tc=tc,
                             t=t, name=f"attn_ctx{l}") if ctx_out else None)

        w2h = gla_w2[l].reshape(2, GLA_RANK, REC_HEADS, HEAD_DIM)
        w2cat = jnp.zeros((REC_HEADS, LANES, 2 * HEAD_DIM), F32)
        for r in range(2):
            w2cat = w2cat.at[:, r * GLA_RANK:(r + 1) * GLA_RANK, r * HEAD_DIM:(r + 1) * HEAD_DIM].set(
                jnp.transpose(w2h[r], (1, 0, 2)))
        b2cat = jnp.transpose(gla_b2[l].reshape(2, REC_HEADS, HEAD_DIM), (1, 0, 2)).reshape(REC_HEADS, 1, 2 * HEAD_DIM)
        y_gla, yc_gla = _recurrence(zl, zc, lay, "gla", (w2cat.astype(BF16), b2cat, gla_norm[l].reshape(1, -1)),
                                    batch=batch, t=t, tc=tc, ctx_out=ctx_out, name=f"gla{l}")
        y_ret, yc_ret = _recurrence(zl, zc, lay, "ret", (lg_tab, rope), batch=batch, t=t, tc=tc, ctx_out=ctx_out,
                                    name=f"ret{l}")

        mw = lay.src["s5_u"][1]
        u0 = lay.off["s5_u"]
        u_tm = jnp.concatenate([jnp.swapaxes(zc[:, u0:u0 + mw].reshape(batch, tc, mw), 0, 1),
                                jnp.swapaxes(zl[:, u0:u0 + mw].reshape(batch, t, mw), 0, 1)], axis=0)
        u_tm = u_tm.reshape((tc + t) * batch, mw)
        tabs = _s5_prep(s5_lambda_re[l], s5_lambda_im[l], s5_log_step[l], s5_b_re[l], s5_b_im[l], batch)
        c_re_bd = _block_diag(jnp.swapaxes(s5_c_re[l], -1, -2), n_blk).astype(BF16)
        c_im_bd = _block_diag(jnp.swapaxes(s5_c_im[l], -1, -2), n_blk).astype(BF16)
        yf, yb = _s5_scan(u_tm, tabs, c_re_bd, c_im_bd, batch=batch, t=t, tc=tc)
        y_s5_tm = _s5_glu(u_tm, yf, yb, s5_d[l], s5_glu_w[l].astype(BF16), s5_glu_b[l])
        y_s5_bm = jnp.swapaxes(y_s5_tm.reshape(tc + t, batch, mw), 0, 1)
        y_s5 = y_s5_bm[:, tc:].reshape(batch * t, mw)
        yc_s5 = y_s5_bm[:, :tc].reshape(batch * tc, mw)

        wb = w_branch[l].astype(BF16)
        wo = w_out[l].astype(BF16)
        is_moe = l % 2 == 1
        h_dtype = F32 if is_moe else BF16
        ml = _matmul(_merge((y_gla, y_att, y_ret, y_s5), zl, lay, wb, name=f"merge_lat{l}"), wo, name=f"out_lat{l}")
        xl, hl = _ln_res(xl, ml, mods, ln1_g[l], ln1_b[l], alpha=alpha, rows_per_group=t, g0=0, gate_chunk=2,
                         next_mods=mods, scale_chunk=4, shift_chunk=3, h_dtype=h_dtype, name=f"ln1_lat{l}")
        if ctx_out:
            mc = _matmul(_merge((yc_gla, yc_att, yc_ret, yc_s5), zc, lay, wb, name=f"merge_ctx{l}"), wo,
                         name=f"out_ctx{l}")
            xc, hc = _ln_res(xc, mc, mods, ln1_g[l], ln1_b[l], alpha=alpha, rows_per_group=0, g0=batch, gate_chunk=2,
                             next_mods=mods, scale_chunk=4, shift_chunk=3, h_dtype=h_dtype, name=f"ln1_ctx{l}")

        i = l // 2
        next_mods = mods_of(l + 1) if l + 1 < depth else None
        if is_moe:
            w1, w3, w2 = moe_w1[i].astype(BF16), moe_w3[i].astype(BF16), moe_w2[i].astype(BF16)
            xl, hl = _moe_ln(xl, hl, mods, moe_router[i], w1, w3, w2, ln2_g[l], ln2_b[l], alpha=alpha,
                             rows_per_group=t, g0=0, next_mods=next_mods, name=f"moe_lat{l}")
            if ctx_out:
                xc, hc = _moe_ln(xc, hc, mods, moe_router[i], w1, w3, w2, ln2_g[l], ln2_b[l], alpha=alpha,
                                 rows_per_group=0, g0=batch, next_mods=next_mods, name=f"moe_ctx{l}")
        else:
            w1, w3, w2 = ffn_w1[i].astype(BF16), ffn_w3[i].astype(BF16), ffn_w2[i].astype(BF16)
            fl = _matmul(_ffn_up(hl, w1, w3, name=f"ffn_up_lat{l}"), w2, name=f"ffn_dn_lat{l}")
            xl, hl = _ln_res(xl, fl, mods, ln2_g[l], ln2_b[l], alpha=alpha, rows_per_group=t, g0=0, gate_chunk=5,
                             next_mods=next_mods, scale_chunk=1, shift_chunk=0, name=f"ln2_lat{l}")
            if ctx_out:
                fc = _matmul(_ffn_up(hc, w1, w3, name=f"ffn_up_ctx{l}"), w2, name=f"ffn_dn_ctx{l}")
                xc, hc = _ln_res(xc, fc, mods, ln2_g[l], ln2_b[l], alpha=alpha, rows_per_group=0, g0=batch,
                                 gate_chunk=5, next_mods=next_mods, scale_chunk=1, shift_chunk=0, name=f"ln2_ctx{l}")
        mods = next_mods

    return xl.reshape(batch, t, d)
```

```python
import functools
import math

import jax
import jax.numpy as jnp
from jax import lax
from jax.experimental import pallas as pl
from jax.experimental.pallas import tpu as pltpu

F32 = jnp.float32
BF16 = jnp.bfloat16

V7X_VMEM_BYTES = 64 * 1024 * 1024
VMEM_LIMIT = V7X_VMEM_BYTES - 8 * 1024 * 1024
TILE_BUDGET = 44 * 1024 * 1024
LANES = 128

HEAD_DIM = 128
ATT_KV_HEADS = 2
REC_HEADS = 4
GLA_RANK = 16
GLA_GATE_NORM = 16.0
CHUNK = 64
REC_BLOCK = 256
S5_GROUP = 16
S5_STATE = 64
N_BRANCH = 4
TOP_K = 2
ROPE_THETA = 10000.0
GRID_W = 64
EPS = 1e-6
MOD_ROWS = 16


def _cparams(sem):
    return pltpu.CompilerParams(dimension_semantics=sem, vmem_limit_bytes=VMEM_LIMIT)


def _tile(n, prefs):
    for t in prefs:
        if n % t == 0:
            return t
    return n


def _mm_tiles(m, k, n, a_bytes, w_bytes, o_bytes, n_w=1):
    for tm, tn in ((1024, 1024), (1024, 512), (512, 1024), (512, 512), (256, 512), (256, 256), (128, 256),
                   (128, 128), (64, 128), (32, 128), (16, 128), (8, 128)):
        if m % tm or n % tn:
            continue
        need = 2 * (tm * k * a_bytes + n_w * k * tn * w_bytes + tm * tn * o_bytes) + n_w * tm * tn * 4
        if need <= TILE_BUDGET:
            return tm, tn
    raise ValueError(f"no matmul tiling for {(m, k, n)}")


class _Layout:
    def __init__(self, d):
        mw = d // 4
        hk = REC_HEADS * HEAD_DIM
        kv = ATT_KV_HEADS * HEAD_DIM
        src_sizes = (mw, kv, kv, hk, hk, mw, mw, 2 * GLA_RANK, hk, hk, mw, mw, mw, N_BRANCH * d)
        names = ("att_q", "att_k", "att_v", "gla_q", "gla_k", "gla_v", "gla_g", "gla_lr",
                 "ret_q", "ret_k", "ret_v", "ret_g", "s5_u", "gates")
        self.src = {}
        off = 0
        for nm, sz in zip(names, src_sizes):
            self.src[nm] = (off, sz)
            off += sz
        self.in_width = off
        order = tuple(nm for nm in names if nm not in ("gla_lr", "gates")) + ("gla_lr",)
        self.off = {}
        self.plan = []
        pos = 0
        for nm in order:
            width = max(self.src[nm][1], LANES)
            self.off[nm] = pos
            self.plan.append((nm, self.src[nm][1]))
            if width > self.src[nm][1]:
                self.plan.append((None, width - self.src[nm][1]))
            pos += width
        small = -(-pos // 1024) * 1024
        if small > pos:
            self.plan.append((None, small - pos))
        self.small = small
        self.off["gates"] = small
        self.plan.append(("gates", self.src["gates"][1]))
        self.width = small + self.src["gates"][1]

    def blk(self, name, width):
        off = self.off[name]
        assert off % width == 0, (name, off, width)
        return off // width

    def permute(self, w):
        parts = []
        for nm, sz in self.plan:
            if nm is None:
                parts.append(jnp.zeros(w.shape[:-1] + (sz,), w.dtype))
            else:
                o = self.src[nm][0]
                parts.append(w[..., o:o + sz])
        return jnp.concatenate(parts, axis=-1)


W_IN_COL_TILE = 512


def _permute_w_in_kernel(a_ref, b_ref, lr_ref, o_ref, *, j_plain, j_lr, shift):
    j = pl.program_id(1)
    tn = o_ref.shape[1]

    @pl.when(j < j_plain)
    def _():
        o_ref[...] = a_ref[...].astype(BF16)

    @pl.when(jnp.logical_and(j >= j_plain, j != j_lr))
    def _():
        x = jnp.concatenate([a_ref[...], b_ref[...]], axis=1)
        o_ref[...] = pltpu.roll(x, x.shape[1] - shift, 1)[:, :tn].astype(BF16)

    @pl.when(j == j_lr)
    def _():
        lane = lax.broadcasted_iota(jnp.int32, lr_ref.shape, 1)
        lr = jnp.where(lane < shift, lr_ref[...], 0.0)
        o_ref[...] = jnp.concatenate([lr, jnp.zeros((lr.shape[0], tn - LANES), F32)], axis=1).astype(BF16)


def _permute_w_in(w_in, l, lay):
    _, d, _ = w_in.shape
    tn = W_IN_COL_TILE
    lr_off, lr_w = lay.src["gla_lr"]
    g_off = lay.src["gates"][0]
    j_plain, j_lr = lr_off // tn, lay.off["gla_lr"] // tn
    assert lr_off % tn == 0 and lay.off["gla_lr"] % tn == 0 and lay.small - lay.off["gla_lr"] == tn
    assert lay.off["gla_lr"] - lr_off == g_off - (lr_off + lr_w) and lay.small - g_off == tn - lr_w
    assert lay.width % tn == 0 and lr_w < LANES
    tk = _tile(d, (1024, 512, 256, 128))
    src_tile = lambda j: jnp.where(j < j_lr, j, j - 1)
    sub = tn // LANES
    return pl.pallas_call(
        functools.partial(_permute_w_in_kernel, j_plain=j_plain, j_lr=j_lr, shift=lr_w),
        out_shape=jax.ShapeDtypeStruct((d, lay.width), BF16), grid=(d // tk, lay.width // tn),
        in_specs=[pl.BlockSpec((None, tk, tn), lambda i, j: (l, i, src_tile(j))),
                  pl.BlockSpec((None, tk, LANES), lambda i, j: (l, i, sub * (src_tile(j) + 1))),
                  pl.BlockSpec((None, tk, LANES), lambda i, j: (l, i, lr_off // LANES))],
        out_specs=pl.BlockSpec((tk, tn), lambda i, j: (i, j)),
        compiler_params=_cparams(("parallel", "arbitrary")), name=f"w_in_perm{l}",
    )(w_in, w_in, w_in)


def _mm_kernel(*refs, has_bias, a_silu):
    if has_bias:
        a_ref, w_ref, b_ref, o_ref = refs
    else:
        a_ref, w_ref, o_ref = refs
    a = a_ref[...]
    if a_silu:
        a = a * jax.nn.sigmoid(a)
    acc = jnp.dot(a.astype(BF16), w_ref[...].astype(BF16), preferred_element_type=F32)
    if has_bias:
        acc = acc + b_ref[...]
    o_ref[...] = acc.astype(o_ref.dtype)


def _matmul(a, w, bias=None, *, n_out=None, out_dtype=F32, a_silu=False, w_index=None, name):
    m, k = a.shape
    n = n_out or w.shape[-1]
    tm, tn = _mm_tiles(m, k, n, a.dtype.itemsize, w.dtype.itemsize, jnp.dtype(out_dtype).itemsize)
    if w_index is None:
        w_spec = pl.BlockSpec((k, tn), lambda i, j: (0, j))
    else:
        w_spec = pl.BlockSpec((None, k, tn), lambda i, j: (w_index, 0, j))
    in_specs = [pl.BlockSpec((tm, k), lambda i, j: (i, 0)), w_spec]
    args = [a, w]
    if bias is not None:
        in_specs.append(pl.BlockSpec((1, tn), lambda i, j: (0, j)))
        args.append(bias.reshape(1, -1))
    return pl.pallas_call(
        functools.partial(_mm_kernel, has_bias=bias is not None, a_silu=a_silu),
        out_shape=jax.ShapeDtypeStruct((m, n), out_dtype),
        grid=(m // tm, n // tn),
        in_specs=in_specs,
        out_specs=pl.BlockSpec((tm, tn), lambda i, j: (i, j)),
        compiler_params=_cparams(("parallel", "arbitrary")),
        name=name,
    )(*args)


def _group_map(rows_per_group, tm, g0, chunk):
    tiles = rows_per_group // tm if rows_per_group else None

    def index_map(i):
        return ((i // tiles + g0) if tiles else g0, 0, chunk)
    return index_map


def _modulate_kernel(x_ref, sc_ref, sh_ref, o_ref):
    o_ref[...] = (x_ref[...] * (1.0 + sc_ref[0]) + sh_ref[0]).astype(o_ref.dtype)


def _modulate(x, mods, *, rows_per_group, g0, scale_chunk, shift_chunk, name):
    m, d = x.shape
    tm = _tile(rows_per_group or m, (512, 256, 128, 64, 32, 16, 8))
    return pl.pallas_call(
        _modulate_kernel,
        out_shape=jax.ShapeDtypeStruct((m, d), BF16),
        grid=(m // tm,),
        in_specs=[pl.BlockSpec((tm, d), lambda i: (i, 0)),
                  pl.BlockSpec((1, 1, d), _group_map(rows_per_group, tm, g0, scale_chunk)),
                  pl.BlockSpec((1, 1, d), _group_map(rows_per_group, tm, g0, shift_chunk))],
        out_specs=pl.BlockSpec((tm, d), lambda i: (i, 0)),
        compiler_params=_cparams(("parallel",)),
        name=name,
    )(x, mods, mods)


def _layer_norm(y, gain, bias):
    mu = jnp.mean(y, axis=-1, keepdims=True)
    dlt = y - mu
    var = jnp.mean(dlt * dlt, axis=-1, keepdims=True)
    return dlt * lax.rsqrt(var + EPS) * gain + bias


def _ln_res_kernel(*refs, alpha, emit_h):
    if emit_h:
        x_ref, m_ref, gate_ref, g_ref, b_ref, sc_ref, sh_ref, xo_ref, ho_ref = refs
    else:
        x_ref, m_ref, gate_ref, g_ref, b_ref, xo_ref = refs
    xn = _layer_norm(alpha * x_ref[...] + gate_ref[0] * m_ref[...], g_ref[...], b_ref[...])
    xo_ref[...] = xn
    if emit_h:
        ho_ref[...] = (xn * (1.0 + sc_ref[0]) + sh_ref[0]).astype(ho_ref.dtype)


def _ln_res(x, mres, mods, ln_g, ln_b, *, alpha, rows_per_group, g0, gate_chunk, next_mods=None,
            scale_chunk=None, shift_chunk=None, h_dtype=BF16, name):
    m, d = x.shape
    tm = _tile(rows_per_group or m, (256, 128, 64, 32, 16, 8))
    emit_h = next_mods is not None
    row = pl.BlockSpec((tm, d), lambda i: (i, 0))
    vec = pl.BlockSpec((1, d), lambda i: (0, 0))
    in_specs = [row, row, pl.BlockSpec((1, 1, d), _group_map(rows_per_group, tm, g0, gate_chunk)), vec, vec]
    args = [x, mres, mods, ln_g.reshape(1, d), ln_b.reshape(1, d)]
    out_shape = [jax.ShapeDtypeStruct((m, d), F32)]
    out_specs = [row]
    if emit_h:
        in_specs += [pl.BlockSpec((1, 1, d), _group_map(rows_per_group, tm, g0, scale_chunk)),
                     pl.BlockSpec((1, 1, d), _group_map(rows_per_group, tm, g0, shift_chunk))]
        args += [next_mods, next_mods]
        out_shape.append(jax.ShapeDtypeStruct((m, d), h_dtype))
        out_specs.append(row)
    out = pl.pallas_call(
        functools.partial(_ln_res_kernel, alpha=alpha, emit_h=emit_h),
        out_shape=out_shape, grid=(m // tm,), in_specs=in_specs, out_specs=out_specs,
        compiler_params=_cparams(("parallel",)), name=name,
    )(*args)
    return out if emit_h else (out[0], None)


def _rope_tables(t):
    nf = HEAD_DIM // 4
    rows = jnp.repeat(jnp.arange(t // GRID_W, dtype=F32), GRID_W)
    cols = (jnp.arange(t) % GRID_W).astype(F32)
    inv = ROPE_THETA ** (-jnp.arange(nf, dtype=F32) / nf)
    ar, ac = rows[:, None] * inv, cols[:, None] * inv
    cr, sr, cc, sc = jnp.cos(ar), jnp.sin(ar), jnp.cos(ac), jnp.sin(ac)
    zero = jnp.zeros_like(sr)
    cos = jnp.concatenate([cr, cr, cc, cc], axis=-1)
    sin_a = jnp.concatenate([-sr, zero, -sc, zero], axis=-1)
    sin_b = jnp.concatenate([zero, sr, zero, sc], axis=-1)
    return cos, sin_a, sin_b


def _rope(x, cos, sin_a, sin_b):
    return x * cos + pltpu.roll(x, 96, 1) * sin_a + pltpu.roll(x, 32, 1) * sin_b


def _rms(x, gain):
    return x * lax.rsqrt(jnp.mean(x * x, axis=-1, keepdims=True) + EPS) * gain


def _attn_kernel(*refs, n_rep, tc, t, has_lat):
    it = iter(refs)
    q_ref, kc_ref, vc_ref, qg_ref, kg_ref = (next(it) for _ in range(5))
    if has_lat:
        kl_ref, vl_ref, cosk, sak, sbk, cosq, saq, sbq = (next(it) for _ in range(8))
    o_ref, k_scr, v_scr = next(it), next(it), next(it)

    @pl.when(pl.program_id(2) == 0)
    def _():
        k_scr[0:tc, :] = _rms(kc_ref[...].astype(F32), kg_ref[...]).astype(BF16)
        v_scr[0:tc, :] = vc_ref[...].astype(BF16)
        if has_lat:
            kl = _rope(_rms(kl_ref[...].astype(F32), kg_ref[...]), cosk[...], sak[...], sbk[...])
            k_scr[tc:tc + t, :] = kl.astype(BF16)
            v_scr[tc:tc + t, :] = vl_ref[...].astype(BF16)

    scale = HEAD_DIM ** -0.5
    keys = k_scr[...]
    vals = v_scr[...]
    for r in range(n_rep):
        q = _rms(q_ref[:, r * HEAD_DIM:(r + 1) * HEAD_DIM].astype(F32), qg_ref[...])
        if has_lat:
            q = _rope(q, cosq[...], saq[...], sbq[...])
        s = lax.dot_general((q * scale).astype(BF16), keys, (((1,), (1,)), ((), ())), preferred_element_type=F32)
        e = jnp.exp(s - jnp.max(s, axis=-1, keepdims=True))
        den = jnp.sum(e, axis=-1, keepdims=True)
        o = jnp.dot(e.astype(BF16), vals, preferred_element_type=F32) / den
        o_ref[:, r * HEAD_DIM:(r + 1) * HEAD_DIM] = o.astype(o_ref.dtype)


def _attention(zq, zc, zl, lay, q_norm, k_norm, rope, *, batch, tq_rows, tc, t, name):
    has_lat = zl is not None
    n_rep = (lay.src["att_q"][1] // HEAD_DIM) // ATT_KV_HEADS
    qw = n_rep * HEAD_DIM
    tq = _tile(tq_rows, (256, 128, 64, 32, 16, 8))
    nq = tq_rows // tq
    qb, kb, vb = lay.blk("att_q", qw), lay.blk("att_k", HEAD_DIM), lay.blk("att_v", HEAD_DIM)
    vec = pl.BlockSpec((1, HEAD_DIM), lambda b, g, i: (0, 0))
    in_specs = [pl.BlockSpec((tq, qw), lambda b, g, i: (b * nq + i, qb + g)),
                pl.BlockSpec((tc, HEAD_DIM), lambda b, g, i: (b, kb + g)),
                pl.BlockSpec((tc, HEAD_DIM), lambda b, g, i: (b, vb + g)), vec, vec]
    args = [zq, zc, zc, q_norm.reshape(1, -1), k_norm.reshape(1, -1)]
    n_keys = tc
    if has_lat:
        full = pl.BlockSpec((t, HEAD_DIM), lambda b, g, i: (0, 0))
        part = pl.BlockSpec((tq, HEAD_DIM), lambda b, g, i: (i, 0))
        in_specs += [pl.BlockSpec((t, HEAD_DIM), lambda b, g, i: (b, kb + g)),
                     pl.BlockSpec((t, HEAD_DIM), lambda b, g, i: (b, vb + g)), full, full, full, part, part, part]
        args += [zl, zl, *rope, *rope]
        n_keys = tc + t
    return pl.pallas_call(
        functools.partial(_attn_kernel, n_rep=n_rep, tc=tc, t=t, has_lat=has_lat),
        out_shape=jax.ShapeDtypeStruct((batch * tq_rows, lay.src["att_q"][1]), BF16),
        grid=(batch, ATT_KV_HEADS, nq),
        in_specs=in_specs,
        out_specs=pl.BlockSpec((tq, qw), lambda b, g, i: (b * nq + i, g)),
        scratch_shapes=[pltpu.VMEM((n_keys, HEAD_DIM), BF16), pltpu.VMEM((n_keys, HEAD_DIM), BF16)],
        compiler_params=_cparams(("parallel", "parallel", "arbitrary")),
        name=name,
    )(*args)


def _split3(x):
    hi = x.astype(BF16)
    r1 = x - hi.astype(F32)
    mid = r1.astype(BF16)
    lo = (r1 - mid.astype(F32)).astype(BF16)
    return hi, mid, lo


def _place(parts):
    n = len(parts)
    zero = jnp.zeros_like(parts[0])
    return jnp.concatenate([jnp.concatenate([p if j == c else zero for j in range(n)], axis=1)
                            for c, p in enumerate(parts)], axis=0)


def _rec_block(b, q_s, k_s, v_s, d, *, forward, want_out, blk):
    nch = blk // CHUNK
    rows = slice(b * blk, (b + 1) * blk)
    q, k, g, v = q_s[rows, :], k_s[rows, :], d["g"][rows, :], v_s[rows, :]
    shift = CHUNK.bit_length() - 1
    ri = lax.broadcasted_iota(jnp.int32, (blk, blk), 0)
    ci = lax.broadcasted_iota(jnp.int32, (blk, blk), 1)
    same = jnp.right_shift(ri, shift) == jnp.right_shift(ci, shift)
    tri = (jnp.where(same, ri - ci, -1) >= 0).astype(BF16)
    c3 = jnp.dot(tri, jnp.concatenate(_split3(g), axis=1), preferred_element_type=F32)
    cum = c3[:, :HEAD_DIM] + c3[:, HEAD_DIM:2 * HEAD_DIM] + c3[:, 2 * HEAD_DIM:]
    qe, ke, qd, kd = [], [], [], []
    for c in range(nch):
        cs = slice(c * CHUNK, (c + 1) * CHUNK)
        if forward:
            a = cum[cs]
            ref, tot = a[CHUNK // 2:CHUNK // 2 + 1], a[CHUNK - 1:CHUNK]
        else:
            a = cum[cs][CHUNK - 1:CHUNK] - cum[cs] + g[cs]
            ref, tot = a[CHUNK // 2 - 1:CHUNK // 2], a[0:1]
        qe.append(q[cs] * jnp.exp(a - ref))
        ke.append(k[cs] * jnp.exp(ref - a))
        qd.append((q[cs] * jnp.exp(a)).astype(BF16))
        kd.append((k[cs] * jnp.exp(tot - a)).astype(BF16))
        d["dec"][(b * nch + c) * 8:(b * nch + c + 1) * 8, :] = jnp.broadcast_to(jnp.exp(tot), (8, HEAD_DIM))
    if want_out:
        s = lax.dot_general(jnp.concatenate(qe, axis=0).astype(BF16), jnp.concatenate(ke, axis=0).astype(BF16),
                            (((1,), (1,)), ((), ())), preferred_element_type=F32)
        if forward:
            mask = jnp.where(same, ri - ci, -1) >= 0
        else:
            mask = jnp.where(same, ci - ri, 0) > 0
        d["o"][rows, :] = jnp.dot(jnp.where(mask, s, 0.0).astype(BF16), v, preferred_element_type=F32)
        d["qd"][rows, :] = _place(qd)
    v_t = jnp.transpose(v.astype(F32)).astype(BF16)
    d["dst"][b] = jnp.dot(v_t, _place(kd), preferred_element_type=F32)


_REC_DIR_SCRATCH = ("o", "dst", "stc", "qd", "dec")


def _rec_kernel(*refs, kind, ctx_out, t, tc, blk):
    it = iter(refs)
    ql, kl, vl, gl, qc, kc, vc, gc = (next(it) for _ in range(8))
    if kind == "gla":
        lrl, lrc, w2, b2, norm = (next(it) for _ in range(5))
    else:
        lg, cos, sa, sb = (next(it) for _ in range(4))
    yl = next(it)
    yc = next(it) if ctx_out else None
    q_s, k_s, v_s, gf_s, gb_s = (next(it) for _ in range(5))
    dirs = [dict(zip(("g",) + _REC_DIR_SCRATCH, (g_s,) + tuple(next(it) for _ in _REC_DIR_SCRATCH)))
            for g_s in (gf_s, gb_s)]
    scale = HEAD_DIM ** -0.5
    nch, nb, nbc = blk // CHUNK, (tc + t) // blk, tc // blk
    ctx_rows, lat_rows = slice(0, tc), slice(tc, tc + t)

    for rows, q_in, k_in, v_in, lr_in, roped in ((ctx_rows, qc, kc, vc, lrc if kind == "gla" else None, False),
                                                 (lat_rows, ql, kl, vl, lrl if kind == "gla" else None, True)):
        q, k = q_in[...].astype(F32), k_in[...].astype(F32)
        if kind == "gla":
            logits = jnp.dot(lr_in[...].astype(BF16), w2[...], preferred_element_type=F32) + b2[...]
            lsig = (jnp.minimum(logits, 0.0) - jnp.log1p(jnp.exp(-jnp.abs(logits)))) * (1.0 / GLA_GATE_NORM)
            gf_s[rows, :] = lsig[:, :HEAD_DIM]
            gb_s[rows, :] = lsig[:, HEAD_DIM:]
            q = q * scale
        else:
            gf_s[rows, :] = jnp.broadcast_to(lg[...], q.shape)
            gb_s[rows, :] = jnp.broadcast_to(lg[...], q.shape)
            if roped:
                q = _rope(q, cos[...], sa[...], sb[...])
                k = _rope(k, cos[...], sa[...], sb[...])
            k = k * scale
        q_s[rows, :] = q
        k_s[rows, :] = k
        v_s[rows, :] = v_in[...].astype(BF16)

    for b in range(nb):
        for d, fwd in zip(dirs, (True, False)):
            _rec_block(b, q_s, k_s, v_s, d, forward=fwd, want_out=ctx_out or b >= nbc, blk=blk)

    for d, fwd in zip(dirs, (True, False)):
        order = range(nb) if fwd else list(range(nbc - 1, -1, -1)) + list(range(nb - 1, nbc - 1, -1))
        st = jnp.zeros(d["dst"].shape[1:2] + (HEAD_DIM,), F32)
        for b in order:
            for c in (range(nch) if fwd else range(nch - 1, -1, -1)):
                lanes = slice(c * HEAD_DIM, (c + 1) * HEAD_DIM)
                d["stc"][b, :, lanes] = st.astype(BF16)
                st = st * d["dec"][(b * nch + c) * 8:(b * nch + c) * 8 + 1, :] + d["dst"][b, :, lanes]

    for b in range(0 if ctx_out else nbc, nb):
        rows = slice(b * blk, (b + 1) * blk)
        for d in dirs:
            d["o"][rows, :] += lax.dot_general(d["qd"][rows, :], d["stc"][b], (((1,), (1,)), ((), ())),
                                               preferred_element_type=F32)

    def finish(rows, g_ref, y_ref):
        o = dirs[0]["o"][rows, :] + dirs[1]["o"][rows, :]
        if kind == "gla":
            o = _rms(o, norm[...])
        else:
            mu = jnp.mean(o, axis=-1, keepdims=True)
            dlt = o - mu
            o = dlt * lax.rsqrt(jnp.mean(dlt * dlt, axis=-1, keepdims=True) + EPS)
        g = g_ref[...].astype(F32)
        y_ref[...] = (o * (g * jax.nn.sigmoid(g))).astype(y_ref.dtype)

    finish(lat_rows, gl, yl)
    if ctx_out:
        finish(ctx_rows, gc, yc)


def _recurrence(zl, zc, lay, kind, params, *, batch, t, tc, ctx_out, name):
    dv = lay.src[kind + "_v"][1] // REC_HEADS
    qb, kb = lay.blk(kind + "_q", HEAD_DIM), lay.blk(kind + "_k", HEAD_DIM)
    vb, gb = lay.blk(kind + "_v", dv), lay.blk(kind + "_g", dv)

    def blocks(rows):
        return [pl.BlockSpec((rows, HEAD_DIM), lambda b, h: (b, qb + h)),
                pl.BlockSpec((rows, HEAD_DIM), lambda b, h: (b, kb + h)),
                pl.BlockSpec((rows, dv), lambda b, h: (b, vb + h)),
                pl.BlockSpec((rows, dv), lambda b, h: (b, gb + h))]

    in_specs = blocks(t) + blocks(tc)
    args = [zl] * 4 + [zc] * 4
    if kind == "gla":
        w2cat, b2cat, norm = params
        lrb = lay.blk("gla_lr", LANES)
        in_specs += [pl.BlockSpec((t, LANES), lambda b, h: (b, lrb)), pl.BlockSpec((tc, LANES), lambda b, h: (b, lrb)),
                     pl.BlockSpec((None, LANES, 2 * HEAD_DIM), lambda b, h: (h, 0, 0)),
                     pl.BlockSpec((None, 1, 2 * HEAD_DIM), lambda b, h: (h, 0, 0)),
                     pl.BlockSpec((1, dv), lambda b, h: (0, 0))]
        args += [zl, zc, w2cat, b2cat, norm]
    else:
        lg, rope = params
        full = pl.BlockSpec((t, HEAD_DIM), lambda b, h: (0, 0))
        in_specs += [pl.BlockSpec((None, 1, HEAD_DIM), lambda b, h: (h, 0, 0)), full, full, full]
        args += [lg, *rope]
    out_shape = [jax.ShapeDtypeStruct((batch * t, REC_HEADS * dv), BF16)]
    out_specs = [pl.BlockSpec((t, dv), lambda b, h: (b, h))]
    if ctx_out:
        out_shape.append(jax.ShapeDtypeStruct((batch * tc, REC_HEADS * dv), BF16))
        out_specs.append(pl.BlockSpec((tc, dv), lambda b, h: (b, h)))

    blk = math.gcd(math.gcd(t, tc), REC_BLOCK)
    rows, nb, nch = tc + t, (tc + t) // blk, blk // CHUNK
    per_dir = {"o": pltpu.VMEM((rows, dv), F32), "dst": pltpu.VMEM((nb, dv, nch * HEAD_DIM), F32),
               "stc": pltpu.VMEM((nb, dv, nch * HEAD_DIM), BF16), "qd": pltpu.VMEM((rows, nch * HEAD_DIM), BF16),
               "dec": pltpu.VMEM((nb * nch * 8, HEAD_DIM), F32)}
    scratch = ([pltpu.VMEM((rows, HEAD_DIM), F32)] * 2 + [pltpu.VMEM((rows, dv), BF16)]
               + [pltpu.VMEM((rows, HEAD_DIM), F32)] * 2 + [per_dir[n] for n in _REC_DIR_SCRATCH] * 2)
    out = pl.pallas_call(
        functools.partial(_rec_kernel, kind=kind, ctx_out=ctx_out, t=t, tc=tc, blk=blk),
        out_shape=out_shape, grid=(batch, REC_HEADS), in_specs=in_specs, out_specs=out_specs,
        scratch_shapes=scratch,
        compiler_params=_cparams(("parallel", "parallel")), name=name,
    )(*args)
    return out[0], (out[1] if ctx_out else None)


S5_LANE_BLOCK = 1024
S5_TIME_BLOCK = 32


def _s5_prep_kernel(lr_ref, li_ref, ls_ref, br_ref, bi_ref, ar_ref, ai_ref, bbr_ref, bbi_ref):
    lam_r, lam_i = lr_ref[0], li_ref[0]
    dt = jnp.exp(ls_ref[0])
    mag = jnp.exp(lam_r * dt)
    a_r = mag * jnp.cos(lam_i * dt)
    a_i = mag * jnp.sin(lam_i * dt)
    den = lam_r * lam_r + lam_i * lam_i
    z_r = ((a_r - 1.0) * lam_r + a_i * lam_i) / den
    z_i = (a_i * lam_r - (a_r - 1.0) * lam_i) / den
    ar_ref[0] = jnp.broadcast_to(a_r, ar_ref.shape[1:])
    ai_ref[0] = jnp.broadcast_to(a_i, ai_ref.shape[1:])
    nblk, _, cols = br_ref.shape[1:]
    for j in range(nblk):
        zr = z_r[:, j * cols:(j + 1) * cols]
        zi = z_i[:, j * cols:(j + 1) * cols]
        bbr_ref[0, j] = (zr * br_ref[0, j] - zi * bi_ref[0, j]).astype(BF16)
        bbi_ref[0, j] = (zr * bi_ref[0, j] + zi * br_ref[0, j]).astype(BF16)


def _block_diag(w, n_blk):
    *lead, g, r, c = w.shape
    gl = g // n_blk
    w = w.reshape(*lead, n_blk, gl, r, c)
    eye = jnp.eye(gl, dtype=w.dtype)
    out = w[..., :, :, None, :] * eye[:, None, :, None]
    return out.reshape(*lead, n_blk, gl * r, gl * c)


def _s5_prep(lam_re, lam_im, log_step, b_re, b_im, batch):
    _, g, p = lam_re.shape
    n_blk = (g * p) // S5_LANE_BLOCK
    flat = lambda a: a.reshape(2, 1, g * p)
    ls = jnp.broadcast_to(log_step[:, :, None], (2, g, p))
    brd = _block_diag(jnp.swapaxes(b_re, -1, -2), n_blk)
    bid = _block_diag(jnp.swapaxes(b_im, -1, -2), n_blk)
    vec = pl.BlockSpec((1, 1, g * p), lambda d: (d, 0, 0))
    mat = pl.BlockSpec((1,) + brd.shape[1:], lambda d: (d, 0, 0, 0))
    tab = pl.BlockSpec((1, batch, g * p), lambda d: (d, 0, 0))
    return pl.pallas_call(
        _s5_prep_kernel,
        out_shape=[jax.ShapeDtypeStruct((2, batch, g * p), F32)] * 2 + [jax.ShapeDtypeStruct(brd.shape, BF16)] * 2,
        grid=(2,), in_specs=[vec, vec, vec, mat, mat], out_specs=[tab, tab, mat, mat],
        compiler_params=_cparams(("parallel",)), name="s5_prep",
    )(flat(lam_re), flat(lam_im), flat(ls), brd, bid)


def _s5_kernel(uf_ref, ub_ref, ar_ref, ai_ref, bbr_ref, bbi_ref, cr_ref, ci_ref, yf_ref, yb_ref,
               xr_scr, xi_scr, sr_scr, si_scr, *, batch, tb):
    n_blk, kin, _ = bbr_ref.shape[1:]
    kout = cr_ref.shape[3]

    @pl.when(pl.program_id(0) == 0)
    def _():
        sr_scr[...] = jnp.zeros_like(sr_scr)
        si_scr[...] = jnp.zeros_like(si_scr)

    for d, (u_ref, y_ref) in enumerate(((uf_ref, yf_ref), (ub_ref, yb_ref))):
        u = u_ref[...].astype(BF16)
        for j in range(n_blk):
            lanes = slice(j * S5_LANE_BLOCK, (j + 1) * S5_LANE_BLOCK)
            uj = u[:, j * kin:(j + 1) * kin]
            xr_scr[:, lanes] = jnp.dot(uj, bbr_ref[d, j], preferred_element_type=F32)
            xi_scr[:, lanes] = jnp.dot(uj, bbi_ref[d, j], preferred_element_type=F32)
        for j in range(n_blk):
            lanes = slice(j * S5_LANE_BLOCK, (j + 1) * S5_LANE_BLOCK)
            a_r, a_i = ar_ref[d, :, lanes], ai_ref[d, :, lanes]

            def step(s, carry, lanes=lanes, a_r=a_r, a_i=a_i, d=d):
                s_r, s_i = carry
                tt = s if d == 0 else tb - 1 - s
                rows = pl.ds(pl.multiple_of(tt * batch, batch), batch)
                n_r = a_r * s_r - a_i * s_i + xr_scr[rows, lanes]
                n_i = a_r * s_i + a_i * s_r + xi_scr[rows, lanes]
                xr_scr[rows, lanes] = n_r
                xi_scr[rows, lanes] = n_i
                return n_r, n_i

            s_r, s_i = lax.fori_loop(0, tb, step, (sr_scr[d, :, lanes], si_scr[d, :, lanes]), unroll=4)
            sr_scr[d, :, lanes] = s_r
            si_scr[d, :, lanes] = s_i
        for j in range(n_blk):
            lanes = slice(j * S5_LANE_BLOCK, (j + 1) * S5_LANE_BLOCK)
            y = (jnp.dot(xr_scr[:, lanes].astype(BF16), cr_ref[d, j], preferred_element_type=F32)
                 - jnp.dot(xi_scr[:, lanes].astype(BF16), ci_ref[d, j], preferred_element_type=F32))
            y_ref[:, j * kout:(j + 1) * kout] = y


def _s5_scan(u_tm, tabs, c_re_bd, c_im_bd, *, batch, t, tc):
    a_r, a_i, bbr, bbi = tabs
    rows, width = u_tm.shape
    tb = _tile(math.gcd(t, tc), (S5_TIME_BLOCK, 16, 8, 4, 2, 1))
    nbc, nbl = tc // tb, t // tb
    nb = nbc + nbl
    blk = tb * batch

    def bwd(i):
        return jnp.where(i < nbc, nbc - 1 - i, nb - 1 - (i - nbc))

    full4 = lambda a: pl.BlockSpec(a.shape, lambda i: (0, 0, 0, 0))
    full3 = lambda a: pl.BlockSpec(a.shape, lambda i: (0, 0, 0))
    n_state = a_r.shape[-1]
    return pl.pallas_call(
        functools.partial(_s5_kernel, batch=batch, tb=tb),
        out_shape=[jax.ShapeDtypeStruct((rows, width), F32)] * 2,
        grid=(nb,),
        in_specs=[pl.BlockSpec((blk, width), lambda i: (i, 0)), pl.BlockSpec((blk, width), lambda i: (bwd(i), 0)),
                  full3(a_r), full3(a_i), full4(bbr), full4(bbi), full4(c_re_bd), full4(c_im_bd)],
        out_specs=[pl.BlockSpec((blk, width), lambda i: (i, 0)), pl.BlockSpec((blk, width), lambda i: (bwd(i), 0))],
        scratch_shapes=[pltpu.VMEM((blk, n_state), F32)] * 2 + [pltpu.VMEM((2, batch, n_state), F32)] * 2,
        compiler_params=_cparams(("arbitrary",)), name="s5_scan",
    )(u_tm, u_tm, a_r, a_i, bbr, bbi, c_re_bd, c_im_bd)


def _glu_kernel(u_ref, yf_ref, yb_ref, d_ref, w_ref, b_ref, o_ref):
    y = u_ref[...].astype(F32) * d_ref[...] + yf_ref[...] + yb_ref[...]
    z = jax.nn.gelu(y, approximate=True)
    gate = jnp.dot(z.astype(BF16), w_ref[...], preferred_element_type=F32) + b_ref[...]
    o_ref[...] = (z * jax.nn.sigmoid(gate)).astype(o_ref.dtype)


def _s5_glu(u_tm, yf, yb, d, glu_w, glu_b):
    rows, width = u_tm.shape
    tm = _tile(rows, (512, 256, 128, 64, 32, 16, 8))
    row = pl.BlockSpec((tm, width), lambda i: (i, 0))
    vec = pl.BlockSpec((1, width), lambda i: (0, 0))
    return pl.pallas_call(
        _glu_kernel, out_shape=jax.ShapeDtypeStruct((rows, width), BF16), grid=(rows // tm,),
        in_specs=[row, row, row, vec, pl.BlockSpec((width, width), lambda i: (0, 0)), vec], out_specs=row,
        compiler_params=_cparams(("parallel",)), name="s5_glu",
    )(u_tm, yf, yb, d.reshape(1, width), glu_w, glu_b.reshape(1, width))


def _merge_kernel(y0, y1, y2, y3, g0, g1, g2, g3, wb_ref, o_ref):
    acc = None
    for i, (y_ref, g_ref) in enumerate(((y0, g0), (y1, g1), (y2, g2), (y3, g3))):
        term = jax.nn.sigmoid(g_ref[...].astype(F32)) * jnp.dot(y_ref[...], wb_ref[i], preferred_element_type=F32)
        acc = term if acc is None else acc + term
    o_ref[...] = acc.astype(o_ref.dtype)


def _merge(ys, z, lay, w_branch, *, name):
    m, mw = ys[0].shape
    d = w_branch.shape[-1]
    tm = _tile(m, (1024, 512, 256, 128, 64, 32, 16))
    tn = _tile(d, (512, 256, 128))
    g_off = lay.off["gates"]
    assert g_off % tn == 0

    def gate_spec(i):
        return pl.BlockSpec((tm, tn), lambda r, c: (r, (g_off + i * d) // tn + c))

    yspec = pl.BlockSpec((tm, mw), lambda r, c: (r, 0))
    return pl.pallas_call(
        _merge_kernel, out_shape=jax.ShapeDtypeStruct((m, d), BF16), grid=(m // tm, d // tn),
        in_specs=[yspec] * N_BRANCH + [gate_spec(i) for i in range(N_BRANCH)]
        + [pl.BlockSpec((N_BRANCH, mw, tn), lambda r, c: (0, 0, c))],
        out_specs=pl.BlockSpec((tm, tn), lambda r, c: (r, c)),
        compiler_params=_cparams(("parallel", "arbitrary")), name=name,
    )(*ys, z, z, z, z, w_branch)


def _ffn_up_kernel(h_ref, w1_ref, w3_ref, o_ref):
    h = h_ref[...]
    a = jnp.dot(h, w1_ref[...], preferred_element_type=F32)
    b = jnp.dot(h, w3_ref[...], preferred_element_type=F32)
    o_ref[...] = (a * jax.nn.sigmoid(a) * b).astype(o_ref.dtype)


def _ffn_up(h, w1, w3, *, name):
    m, k = h.shape
    f = w1.shape[-1]
    tm, tn = _mm_tiles(m, k, f, 2, 2, 2, n_w=2)
    w_spec = pl.BlockSpec((k, tn), lambda i, j: (0, j))
    return pl.pallas_call(
        _ffn_up_kernel, out_shape=jax.ShapeDtypeStruct((m, f), BF16), grid=(m // tm, f // tn),
        in_specs=[pl.BlockSpec((tm, k), lambda i, j: (i, 0)), w_spec, w_spec],
        out_specs=pl.BlockSpec((tm, tn), lambda i, j: (i, j)),
        compiler_params=_cparams(("parallel", "arbitrary")), name=name,
    )(h, w1, w3)


MOE_ROW_TILE = 512
MOE_GATHER_TILE = 256
MOE_COMBINE_TILE = 128
DMA_UNROLL = 8


def _router_kernel(h_ref, rh_ref, rl_ref, o_ref, *, n_experts):
    h = h_ref[...]
    hh = h.astype(BF16)
    hl = (h - hh.astype(F32)).astype(BF16)
    logits = (jnp.dot(hh, rh_ref[...], preferred_element_type=F32) + jnp.dot(hh, rl_ref[...], preferred_element_type=F32)
              + jnp.dot(hl, rh_ref[...], preferred_element_type=F32))
    lane = lax.broadcasted_iota(jnp.int32, logits.shape, 1).astype(F32)
    neg = jnp.float32(-jnp.inf)
    x1 = jnp.where(lane < n_experts, logits, neg)
    m1 = jnp.max(x1, axis=-1, keepdims=True)
    i1 = jnp.min(jnp.where(x1 == m1, lane, float(LANES)), axis=-1, keepdims=True)
    x2 = jnp.where(lane == i1, neg, x1)
    m2 = jnp.max(x2, axis=-1, keepdims=True)
    i2 = jnp.min(jnp.where(x2 == m2, lane, float(LANES)), axis=-1, keepdims=True)
    e2 = jnp.exp(m2 - m1)
    w1 = 1.0 / (1.0 + e2)
    o_ref[...] = (jnp.where(lane == 0.0, w1, 0.0) + jnp.where(lane == 1.0, e2 * w1, 0.0)
                  + jnp.where(lane == 2.0, i1, 0.0) + jnp.where(lane == 3.0, i2, 0.0))


def _router(h, router):
    m, d = h.shape
    n_experts = router.shape[-1]
    rp = jnp.zeros((d, LANES), F32).at[:, :n_experts].set(router)
    rh = rp.astype(BF16)
    rl = (rp - rh.astype(F32)).astype(BF16)
    tm = _tile(m, (256, 128, 64, 32, 16, 8))
    mat = pl.BlockSpec((d, LANES), lambda i: (0, 0))
    return pl.pallas_call(
        functools.partial(_router_kernel, n_experts=n_experts),
        out_shape=jax.ShapeDtypeStruct((m, LANES), F32), grid=(m // tm,),
        in_specs=[pl.BlockSpec((tm, d), lambda i: (i, 0)), mat, mat],
        out_specs=pl.BlockSpec((tm, LANES), lambda i: (i, 0)),
        compiler_params=_cparams(("parallel",)), name="moe_router",
    )(h, rh, rl)


def _moe_plan(expert_idx, n_experts, tile):
    m = expert_idx.shape[0]
    flat = expert_idx.reshape(-1)
    onehot = (flat[:, None] == jnp.arange(n_experts, dtype=jnp.int32)[None, :]).astype(jnp.int32)
    rank = jnp.take_along_axis(jnp.cumsum(onehot, axis=0) - onehot, flat[:, None], axis=1)[:, 0]
    count = jnp.sum(onehot, axis=0)
    group = ((count + tile - 1) // tile) * tile
    group_end = jnp.cumsum(group)
    dest = (group_end - group)[flat] + rank
    rows = m * TOP_K + n_experts * tile
    row_src = jnp.zeros((rows,), jnp.int32).at[dest].set(jnp.arange(m * TOP_K, dtype=jnp.int32) // TOP_K)
    n_tiles = rows // tile
    n_used = group_end[-1] // tile
    tile_id = jnp.arange(n_tiles, dtype=jnp.int32)
    owner = jnp.searchsorted(group_end, jnp.minimum(tile_id, n_used - 1) * tile, side="right").astype(jnp.int32)
    return dest.reshape(m, TOP_K), row_src, jnp.minimum(owner, n_experts - 1), n_used.reshape(1).astype(jnp.int32)


def _row_copy(src_hbm, src_row, buf, slot, dst_row, sem):
    return pltpu.make_async_copy(src_hbm.at[pl.ds(src_row, 1), :], buf.at[slot, pl.ds(dst_row, 1), :], sem.at[slot])


def _gather_rows(idx_ref, src_hbm, buf, sem, slot, n):
    def body(r, carry):
        _row_copy(src_hbm, idx_ref[0, r], buf, slot, r, sem).start()
        return carry
    lax.fori_loop(0, n, body, 0, unroll=DMA_UNROLL)


def _wait_rows(src_hbm, buf, sem, slot, n):
    def body(r, carry):
        _row_copy(src_hbm, 0, buf, slot, r, sem).wait()
        return carry
    lax.fori_loop(0, n, body, 0, unroll=DMA_UNROLL)


def _gather_kernel(cur_ref, nxt_ref, src_hbm, o_ref, buf, sem):
    s, n_steps = pl.program_id(0), pl.num_programs(0)
    n = buf.shape[1]

    @pl.when(s == 0)
    def _():
        _gather_rows(cur_ref, src_hbm, buf, sem, 0, n)

    @pl.when(s + 1 < n_steps)
    def _():
        _gather_rows(nxt_ref, src_hbm, buf, sem, (s + 1) % 2, n)

    slot = s % 2
    _wait_rows(src_hbm, buf, sem, slot, n)
    o_ref[...] = buf[slot].astype(o_ref.dtype)


def _moe_gather(h, row_src):
    d = h.shape[1]
    rows = row_src.shape[0]
    tg = _tile(rows, (MOE_GATHER_TILE, 128, 64, 32, 16, 8))
    steps = rows // tg
    idx = row_src.reshape(steps, 1, tg)
    smem = lambda f: pl.BlockSpec((None, 1, tg), f, memory_space=pltpu.SMEM)
    return pl.pallas_call(
        _gather_kernel, out_shape=jax.ShapeDtypeStruct((rows, d), BF16), grid=(steps,),
        in_specs=[smem(lambda s: (s, 0, 0)), smem(lambda s: (jnp.minimum(s + 1, steps - 1), 0, 0)),
                  pl.BlockSpec(memory_space=pl.ANY)],
        out_specs=pl.BlockSpec((tg, d), lambda s: (s, 0)),
        scratch_shapes=[pltpu.VMEM((2, tg, d), F32), pltpu.SemaphoreType.DMA((2,))],
        compiler_params=_cparams(("arbitrary",)), name="moe_gather",
    )(idx, idx, h)


def _moe_up_kernel(te_ref, nu_ref, x_ref, w1_ref, w3_ref, o_ref):
    del te_ref

    @pl.when(pl.program_id(1) < nu_ref[0])
    def _():
        _ffn_up_kernel(x_ref, w1_ref, w3_ref, o_ref)

    @pl.when(pl.program_id(1) >= nu_ref[0])
    def _():
        o_ref[...] = jnp.zeros_like(o_ref)


def _moe_dn_kernel(te_ref, nu_ref, g_ref, w_ref, o_ref):
    del te_ref

    @pl.when(pl.program_id(1) < nu_ref[0])
    def _():
        o_ref[...] = jnp.dot(g_ref[...], w_ref[...], preferred_element_type=F32).astype(o_ref.dtype)

    @pl.when(pl.program_id(1) >= nu_ref[0])
    def _():
        o_ref[...] = jnp.zeros_like(o_ref)


def _moe_grouped(body, a, weights, tile_expert, n_used, *, tn, out_dtype, name):
    rows, k = a.shape
    n = weights[0].shape[-1]
    tm = MOE_ROW_TILE
    used = lambda i, nu: jnp.minimum(i, nu[0] - 1)
    w_spec = pl.BlockSpec((None, k, tn), lambda j, i, te, nu: (te[i], 0, j))
    return pl.pallas_call(
        body, out_shape=jax.ShapeDtypeStruct((rows, n), out_dtype),
        grid_spec=pltpu.PrefetchScalarGridSpec(
            num_scalar_prefetch=2, grid=(n // tn, rows // tm),
            in_specs=[pl.BlockSpec((tm, k), lambda j, i, te, nu: (used(i, nu), 0))] + [w_spec] * len(weights),
            out_specs=pl.BlockSpec((tm, tn), lambda j, i, te, nu: (i, j))),
        compiler_params=_cparams(("arbitrary", "arbitrary")), name=name,
    )(tile_expert, n_used, a, *weights)


def _moe_combine_kernel(*refs, alpha, emit_h):
    it = iter(refs)
    cur_ref, nxt_ref, x_ref, wts_ref, gate_ref, g_ref, b_ref = (next(it) for _ in range(7))
    if emit_h:
        sc_ref, sh_ref = next(it), next(it)
    y_hbm, xo_ref = next(it), next(it)
    ho_ref = next(it) if emit_h else None
    buf, sem = next(it), next(it)
    s, n_steps = pl.program_id(0), pl.num_programs(0)
    n = buf.shape[1]
    tm = n // TOP_K

    @pl.when(s == 0)
    def _():
        _gather_rows(cur_ref, y_hbm, buf, sem, 0, n)

    @pl.when(s + 1 < n_steps)
    def _():
        _gather_rows(nxt_ref, y_hbm, buf, sem, (s + 1) % 2, n)

    slot = s % 2
    _wait_rows(y_hbm, buf, sem, slot, n)
    f = wts_ref[:, 0:1] * buf[slot, 0:tm, :] + wts_ref[:, 1:2] * buf[slot, tm:2 * tm, :]
    xn = _layer_norm(alpha * x_ref[...] + gate_ref[0] * f, g_ref[...], b_ref[...])
    xo_ref[...] = xn
    if emit_h:
        ho_ref[...] = (xn * (1.0 + sc_ref[0]) + sh_ref[0]).astype(ho_ref.dtype)


def _moe_combine_ln(x, y_sorted, dest, wts, mods, ln_g, ln_b, *, alpha, rows_per_group, g0, gate_chunk,
                    next_mods, scale_chunk, shift_chunk, name):
    m, d = x.shape
    tm = _tile(rows_per_group or m, (MOE_COMBINE_TILE, 64, 32, 16, 8))
    steps = m // tm
    idx = jnp.swapaxes(dest.reshape(steps, tm, TOP_K), 1, 2).reshape(steps, 1, TOP_K * tm)
    emit_h = next_mods is not None
    row = pl.BlockSpec((tm, d), lambda s: (s, 0))
    vec = pl.BlockSpec((1, d), lambda s: (0, 0))
    smem = lambda f: pl.BlockSpec((None, 1, TOP_K * tm), f, memory_space=pltpu.SMEM)
    in_specs = [smem(lambda s: (s, 0, 0)), smem(lambda s: (jnp.minimum(s + 1, steps - 1), 0, 0)), row,
                pl.BlockSpec((tm, LANES), lambda s: (s, 0)),
                pl.BlockSpec((1, 1, d), _group_map(rows_per_group, tm, g0, gate_chunk)), vec, vec]
    args = [idx, idx, x, wts, mods, ln_g.reshape(1, d), ln_b.reshape(1, d)]
    out_shape = [jax.ShapeDtypeStruct((m, d), F32)]
    out_specs = [row]
    if emit_h:
        in_specs += [pl.BlockSpec((1, 1, d), _group_map(rows_per_group, tm, g0, scale_chunk)),
                     pl.BlockSpec((1, 1, d), _group_map(rows_per_group, tm, g0, shift_chunk))]
        args += [next_mods, next_mods]
        out_shape.append(jax.ShapeDtypeStruct((m, d), BF16))
        out_specs.append(row)
    in_specs.append(pl.BlockSpec(memory_space=pl.ANY))
    args.append(y_sorted)
    out = pl.pallas_call(
        functools.partial(_moe_combine_kernel, alpha=alpha, emit_h=emit_h),
        out_shape=out_shape, grid=(steps,), in_specs=in_specs, out_specs=out_specs,
        scratch_shapes=[pltpu.VMEM((2, TOP_K * tm, d), F32), pltpu.SemaphoreType.DMA((2,))],
        compiler_params=_cparams(("arbitrary",)), name=name,
    )(*args)
    return out if emit_h else (out[0], None)


def _moe_ln(x, h, mods, router, w1, w3, w2, ln_g, ln_b, *, alpha, rows_per_group, g0, next_mods, name):
    n_experts = router.shape[-1]
    route = _router(h, router)
    expert_idx = route[:, TOP_K:2 * TOP_K].astype(jnp.int32)
    dest, row_src, tile_expert, n_used = _moe_plan(expert_idx, n_experts, MOE_ROW_TILE)
    xs = _moe_gather(h, row_src)
    up = _moe_grouped(_moe_up_kernel, xs, (w1, w3), tile_expert, n_used, tn=_tile(w1.shape[-1], (1024, 512, 256, 128)),
                      out_dtype=BF16, name=name + "_up")
    ys = _moe_grouped(_moe_dn_kernel, up, (w2,), tile_expert, n_used, tn=_tile(w2.shape[-1], (1024, 512, 256, 128)),
                      out_dtype=F32, name=name + "_dn")
    return _moe_combine_ln(x, ys, dest, route, mods, ln_g, ln_b, alpha=alpha, rows_per_group=rows_per_group, g0=g0,
                           gate_chunk=5, next_mods=next_mods, scale_chunk=1, shift_chunk=0, name=name + "_ln")


def kernel(x, c, ctx, c_ctx, w_ada, b_ada, w_in, b_in, attn_q_norm, attn_k_norm, gla_w2, gla_b2, gla_norm,
           s5_lambda_re, s5_lambda_im, s5_log_step, s5_b_re, s5_b_im, s5_c_re, s5_c_im, s5_d, s5_glu_w, s5_glu_b,
           w_branch, w_out, ln1_g, ln1_b, ln2_g, ln2_b, ffn_w1, ffn_w3, ffn_w2, moe_router, moe_w1, moe_w3, moe_w2):
    batch, t, d = x.shape
    tc = ctx.shape[1]
    depth = w_ada.shape[0]
    assert batch + 1 <= MOD_ROWS and t % CHUNK == 0 and tc % CHUNK == 0 and t % GRID_W == 0
    lay = _Layout(d)
    assert lay.in_width == w_in.shape[-1]
    alpha = (2 * depth) ** 0.25
    rope = _rope_tables(t)
    log_gamma = jnp.log1p(-jnp.exp2(-5.0 - jnp.arange(REC_HEADS, dtype=F32)))
    lg_tab = jnp.broadcast_to(log_gamma[:, None, None], (REC_HEADS, 1, HEAD_DIM))
    n_blk = (s5_lambda_re.shape[2] * S5_STATE) // S5_LANE_BLOCK

    xl = x.reshape(batch * t, d)
    xc = ctx.reshape(batch * tc, d)
    cvec = jnp.concatenate([c, c_ctx[None], jnp.zeros((MOD_ROWS - batch - 1, d), F32)], axis=0)

    def mods_of(l):
        return _matmul(cvec, w_ada, b_ada[l], a_silu=True, w_index=l, name=f"ada{l}").reshape(MOD_ROWS, 1, 6 * d)

    mods = mods_of(0)
    hl = _modulate(xl, mods, rows_per_group=t, g0=0, scale_chunk=1, shift_chunk=0, name="mod_lat")
    hc = _modulate(xc, mods, rows_per_group=0, g0=batch, scale_chunk=1, shift_chunk=0, name="mod_ctx")

    for l in range(depth):
        ctx_out = l < depth - 1
        w_in_p = _permute_w_in(w_in, l, lay)
        b_in_p = lay.permute(b_in[l])
        zl = _matmul(hl, w_in_p, b_in_p, out_dtype=BF16, name=f"in_lat{l}")
        zc = _matmul(hc, w_in_p, b_in_p, n_out=None if ctx_out else lay.small, out_dtype=BF16, name=f"in_ctx{l}")

        y_att = _attention(zl, zc, zl, lay, attn_q_norm[l], attn_k_norm[l], rope, batch=batch, tq_rows=t, tc=tc, t=t,
                           name=f"attn_lat{l}")
        yc_att = (_attention(zc, zc, None, lay, attn_q_norm[l], attn_k_norm[l], None, batch=batch, tq_rows=tc, tc=tc,
                             t=t, name=f"attn_ctx{l}") if ctx_out else None)

        w2h = gla_w2[l].reshape(2, GLA_RANK, REC_HEADS, HEAD_DIM)
        w2cat = jnp.zeros((REC_HEADS, LANES, 2 * HEAD_DIM), F32)
        for r in range(2):
            w2cat = w2cat.at[:, r * GLA_RANK:(r + 1) * GLA_RANK, r * HEAD_DIM:(r + 1) * HEAD_DIM].set(
                jnp.transpose(w2h[r], (1, 0, 2)))
        b2cat = jnp.transpose(gla_b2[l].reshape(2, REC_HEADS, HEAD_DIM), (1, 0, 2)).reshape(REC_HEADS, 1, 2 * HEAD_DIM)
        y_gla, yc_gla = _recurrence(zl, zc, lay, "gla", (w2cat.astype(BF16), b2cat, gla_norm[l].reshape(1, -1)),
                                    batch=batch, t=t, tc=tc, ctx_out=ctx_out, name=f"gla{l}")
        y_ret, yc_ret = _recurrence(zl, zc, lay, "ret", (lg_tab, rope), batch=batch, t=t, tc=tc, ctx_out=ctx_out,
                                    name=f"ret{l}")

        mw = lay.src["s5_u"][1]
        u0 = lay.off["s5_u"]
        u_tm = jnp.concatenate([jnp.swapaxes(zc[:, u0:u0 + mw].reshape(batch, tc, mw), 0, 1),
                                jnp.swapaxes(zl[:, u0:u0 + mw].reshape(batch, t, mw), 0, 1)], axis=0)
        u_tm = u_tm.reshape((tc + t) * batch, mw)
        tabs = _s5_prep(s5_lambda_re[l], s5_lambda_im[l], s5_log_step[l], s5_b_re[l], s5_b_im[l], batch)
        c_re_bd = _block_diag(jnp.swapaxes(s5_c_re[l], -1, -2), n_blk).astype(BF16)
        c_im_bd = _block_diag(jnp.swapaxes(s5_c_im[l], -1, -2), n_blk).astype(BF16)
        yf, yb = _s5_scan(u_tm, tabs, c_re_bd, c_im_bd, batch=batch, t=t, tc=tc)
        y_s5_tm = _s5_glu(u_tm, yf, yb, s5_d[l], s5_glu_w[l].astype(BF16), s5_glu_b[l])
        y_s5_bm = jnp.swapaxes(y_s5_tm.reshape(tc + t, batch, mw), 0, 1)
        y_s5 = y_s5_bm[:, tc:].reshape(batch * t, mw)
        yc_s5 = y_s5_bm[:, :tc].reshape(batch * tc, mw)

        wb = w_branch[l].astype(BF16)
        wo = w_out[l].astype(BF16)
        is_moe = l % 2 == 1
        h_dtype = F32 if is_moe else BF16
        ml = _matmul(_merge((y_gla, y_att, y_ret, y_s5), zl, lay, wb, name=f"merge_lat{l}"), wo, name=f"out_lat{l}")
        xl, hl = _ln_res(xl, ml, mods, ln1_g[l], ln1_b[l], alpha=alpha, rows_per_group=t, g0=0, gate_chunk=2,
                         next_mods=mods, scale_chunk=4, shift_chunk=3, h_dtype=h_dtype, name=f"ln1_lat{l}")
        if ctx_out:
            mc = _matmul(_merge((yc_gla, yc_att, yc_ret, yc_s5), zc, lay, wb, name=f"merge_ctx{l}"), wo,
                         name=f"out_ctx{l}")
            xc, hc = _ln_res(xc, mc, mods, ln1_g[l], ln1_b[l], alpha=alpha, rows_per_group=0, g0=batch, gate_chunk=2,
                             next_mods=mods, scale_chunk=4, shift_chunk=3, h_dtype=h_dtype, name=f"ln1_ctx{l}")

        i = l // 2
        next_mods = mods_of(l + 1) if l + 1 < depth else None
        if is_moe:
            w1, w3, w2 = moe_w1[i].astype(BF16), moe_w3[i].astype(BF16), moe_w2[i].astype(BF16)
            xl, hl = _moe_ln(xl, hl, mods, moe_router[i], w1, w3, w2, ln2_g[l], ln2_b[l], alpha=alpha,
                             rows_per_group=t, g0=0, next_mods=next_mods, name=f"moe_lat{l}")
            if ctx_out:
                xc, hc = _moe_ln(xc, hc, mods, moe_router[i], w1, w3, w2, ln2_g[l], ln2_b[l], alpha=alpha,
                                 rows_per_group=0, g0=batch, next_mods=next_mods, name=f"moe_ctx{l}")
        else:
            w1, w3, w2 = ffn_w1[i].astype(BF16), ffn_w3[i].astype(BF16), ffn_w2[i].astype(BF16)
            fl = _matmul(_ffn_up(hl, w1, w3, name=f"ffn_up_lat{l}"), w2, name=f"ffn_dn_lat{l}")
            xl, hl = _ln_res(xl, fl, mods, ln2_g[l], ln2_b[l], alpha=alpha, rows_per_group=t, g0=0, gate_chunk=5,
                             next_mods=next_mods, scale_chunk=1, shift_chunk=0, name=f"ln2_lat{l}")
            if ctx_out:
                fc = _matmul(_ffn_up(hc, w1, w3, name=f"ffn_up_ctx{l}"), w2, name=f"ffn_dn_ctx{l}")
                xc, hc = _ln_res(xc, fc, mods, ln2_g[l], ln2_b[l], alpha=alpha, rows_per_group=0, g0=batch,
                                 gate_chunk=5, next_mods=next_mods, scale_chunk=1, shift_chunk=0, name=f"ln2_ctx{l}")
        mods = next_mods

    return xl.reshape(batch, t, d)
```

```python
import functools
import math

import jax
import jax.numpy as jnp
from jax import lax
from jax.experimental import pallas as pl
from jax.experimental.pallas import tpu as pltpu

F32 = jnp.float32
BF16 = jnp.bfloat16

V7X_VMEM_BYTES = 64 * 1024 * 1024
VMEM_LIMIT = V7X_VMEM_BYTES - 8 * 1024 * 1024
TILE_BUDGET = 44 * 1024 * 1024
LANES = 128

HEAD_DIM = 128
ATT_KV_HEADS = 2
REC_HEADS = 4
GLA_RANK = 16
GLA_GATE_NORM = 16.0
CHUNK = 64
REC_BLOCK = 256
S5_GROUP = 16
S5_STATE = 64
N_BRANCH = 4
TOP_K = 2
ROPE_THETA = 10000.0
GRID_W = 64
EPS = 1e-6
MOD_ROWS = 16


def _cparams(sem):
    return pltpu.CompilerParams(dimension_semantics=sem, vmem_limit_bytes=VMEM_LIMIT)


def _tile(n, prefs):
    for t in prefs:
        if n % t == 0:
            return t
    return n


def _mm_tiles(m, k, n, a_bytes, w_bytes, o_bytes, n_w=1):
    for tm, tn in ((1024, 1024), (1024, 512), (512, 1024), (512, 512), (256, 512), (256, 256), (128, 256),
                   (128, 128), (64, 128), (32, 128), (16, 128), (8, 128)):
        if m % tm or n % tn:
            continue
        need = 2 * (tm * k * a_bytes + n_w * k * tn * w_bytes + tm * tn * o_bytes) + n_w * tm * tn * 4
        if need <= TILE_BUDGET:
            return tm, tn
    raise ValueError(f"no matmul tiling for {(m, k, n)}")


class _Layout:
    def __init__(self, d):
        mw = d // 4
        hk = REC_HEADS * HEAD_DIM
        kv = ATT_KV_HEADS * HEAD_DIM
        src_sizes = (mw, kv, kv, hk, hk, mw, mw, 2 * GLA_RANK, hk, hk, mw, mw, mw, N_BRANCH * d)
        names = ("att_q", "att_k", "att_v", "gla_q", "gla_k", "gla_v", "gla_g", "gla_lr",
                 "ret_q", "ret_k", "ret_v", "ret_g", "s5_u", "gates")
        self.src = {}
        off = 0
        for nm, sz in zip(names, src_sizes):
            self.src[nm] = (off, sz)
            off += sz
        self.in_width = off
        order = tuple(nm for nm in names if nm not in ("gla_lr", "gates")) + ("gla_lr",)
        self.off = {}
        self.plan = []
        pos = 0
        for nm in order:
            width = max(self.src[nm][1], LANES)
            self.off[nm] = pos
            self.plan.append((nm, self.src[nm][1]))
            if width > self.src[nm][1]:
                self.plan.append((None, width - self.src[nm][1]))
            pos += width
        small = -(-pos // 1024) * 1024
        if small > pos:
            self.plan.append((None, small - pos))
        self.small = small
        self.off["gates"] = small
        self.plan.append(("gates", self.src["gates"][1]))
        self.width = small + self.src["gates"][1]

    def blk(self, name, width):
        off = self.off[name]
        assert off % width == 0, (name, off, width)
        return off // width

    def permute(self, w):
        parts = []
        for nm, sz in self.plan:
            if nm is None:
                parts.append(jnp.zeros(w.shape[:-1] + (sz,), w.dtype))
            else:
                o = self.src[nm][0]
                parts.append(w[..., o:o + sz])
        return jnp.concatenate(parts, axis=-1)


W_IN_ROW_TILE = 512


def _permute_w_in_kernel(wt_hbm, o_ref, buf, sem, *, layer, j_plain, j_lr, lr_off, lr_w):
    s, n_steps = pl.program_id(0), pl.num_programs(0)
    tn = o_ref.shape[0]

    def tile_copy(j, slot):
        shifted = jnp.where(j < j_lr, j, j - 1) * tn + lr_w
        r0 = jnp.where(j < j_plain, j * tn, jnp.where(j == j_lr, lr_off, shifted))
        return pltpu.make_async_copy(wt_hbm.at[layer, pl.ds(pl.multiple_of(r0, 8), tn), :], buf.at[slot], sem.at[slot])

    @pl.when(s == 0)
    def _():
        tile_copy(s, 0).start()

    @pl.when(s + 1 < n_steps)
    def _():
        tile_copy(s + 1, (s + 1) % 2).start()

    slot = s % 2
    tile_copy(s, slot).wait()
    row = lax.broadcasted_iota(jnp.int32, o_ref.shape, 0)
    keep = row < jnp.where(s == j_lr, lr_w, tn)
    o_ref[...] = jnp.where(keep, buf[slot], 0.0).astype(BF16)


def _permute_w_in(w_in, l, lay):
    wt = jnp.swapaxes(w_in, 1, 2)
    d = wt.shape[2]
    tn = W_IN_ROW_TILE
    lr_off, lr_w = lay.src["gla_lr"]
    g_off = lay.src["gates"][0]
    j_plain, j_lr = lr_off // tn, lay.off["gla_lr"] // tn
    assert lr_off % tn == 0 and lay.off["gla_lr"] % tn == 0 and lay.small - lay.off["gla_lr"] == tn
    assert lay.off["gla_lr"] - lr_off == g_off - (lr_off + lr_w) and lay.small - g_off == tn - lr_w
    assert lay.width % tn == 0 and lr_w % 8 == 0 and lr_off + tn <= lay.in_width
    return pl.pallas_call(
        functools.partial(_permute_w_in_kernel, layer=l, j_plain=j_plain, j_lr=j_lr, lr_off=lr_off, lr_w=lr_w),
        out_shape=jax.ShapeDtypeStruct((lay.width, d), BF16), grid=(lay.width // tn,),
        in_specs=[pl.BlockSpec(memory_space=pl.ANY)],
        out_specs=pl.BlockSpec((tn, d), lambda s: (s, 0)),
        scratch_shapes=[pltpu.VMEM((2, tn, d), F32), pltpu.SemaphoreType.DMA((2,))],
        compiler_params=_cparams(("arbitrary",)), name=f"w_in_perm{l}",
    )(wt)


def _mm_kernel(*refs, has_bias, a_silu, w_rows):
    if has_bias:
        a_ref, w_ref, b_ref, o_ref = refs
    else:
        a_ref, w_ref, o_ref = refs
    a = a_ref[...]
    if a_silu:
        a = a * jax.nn.sigmoid(a)
    contract = (((1,), (1 if w_rows else 0,)), ((), ()))
    acc = lax.dot_general(a.astype(BF16), w_ref[...].astype(BF16), contract, preferred_element_type=F32)
    if has_bias:
        acc = acc + b_ref[...]
    o_ref[...] = acc.astype(o_ref.dtype)


def _matmul(a, w, bias=None, *, n_out=None, out_dtype=F32, a_silu=False, w_index=None, w_rows=False, name):
    m, k = a.shape
    n = n_out or w.shape[0 if w_rows else -1]
    tm, tn = _mm_tiles(m, k, n, a.dtype.itemsize, w.dtype.itemsize, jnp.dtype(out_dtype).itemsize)
    if w_rows:
        w_spec = pl.BlockSpec((tn, k), lambda i, j: (j, 0))
    elif w_index is None:
        w_spec = pl.BlockSpec((k, tn), lambda i, j: (0, j))
    else:
        w_spec = pl.BlockSpec((None, k, tn), lambda i, j: (w_index, 0, j))
    in_specs = [pl.BlockSpec((tm, k), lambda i, j: (i, 0)), w_spec]
    args = [a, w]
    if bias is not None:
        in_specs.append(pl.BlockSpec((1, tn), lambda i, j: (0, j)))
        args.append(bias.reshape(1, -1))
    return pl.pallas_call(
        functools.partial(_mm_kernel, has_bias=bias is not None, a_silu=a_silu, w_rows=w_rows),
        out_shape=jax.ShapeDtypeStruct((m, n), out_dtype),
        grid=(m // tm, n // tn),
        in_specs=in_specs,
        out_specs=pl.BlockSpec((tm, tn), lambda i, j: (i, j)),
        compiler_params=_cparams(("parallel", "arbitrary")),
        name=name,
    )(*args)


def _group_map(rows_per_group, tm, g0, chunk):
    tiles = rows_per_group // tm if rows_per_group else None

    def index_map(i):
        return ((i // tiles + g0) if tiles else g0, 0, chunk)
    return index_map


def _modulate_kernel(x_ref, sc_ref, sh_ref, o_ref):
    o_ref[...] = (x_ref[...] * (1.0 + sc_ref[0]) + sh_ref[0]).astype(o_ref.dtype)


def _modulate(x, mods, *, rows_per_group, g0, scale_chunk, shift_chunk, name):
    m, d = x.shape
    tm = _tile(rows_per_group or m, (512, 256, 128, 64, 32, 16, 8))
    return pl.pallas_call(
        _modulate_kernel,
        out_shape=jax.ShapeDtypeStruct((m, d), BF16),
        grid=(m // tm,),
        in_specs=[pl.BlockSpec((tm, d), lambda i: (i, 0)),
                  pl.BlockSpec((1, 1, d), _group_map(rows_per_group, tm, g0, scale_chunk)),
                  pl.BlockSpec((1, 1, d), _group_map(rows_per_group, tm, g0, shift_chunk))],
        out_specs=pl.BlockSpec((tm, d), lambda i: (i, 0)),
        compiler_params=_cparams(("parallel",)),
        name=name,
    )(x, mods, mods)


def _layer_norm(y, gain, bias):
    mu = jnp.mean(y, axis=-1, keepdims=True)
    dlt = y - mu
    var = jnp.mean(dlt * dlt, axis=-1, keepdims=True)
    return dlt * lax.rsqrt(var + EPS) * gain + bias


def _ln_res_kernel(*refs, alpha, emit_h):
    if emit_h:
        x_ref, m_ref, gate_ref, g_ref, b_ref, sc_ref, sh_ref, xo_ref, ho_ref = refs
    else:
        x_ref, m_ref, gate_ref, g_ref, b_ref, xo_ref = refs
    xn = _layer_norm(alpha * x_ref[...] + gate_ref[0] * m_ref[...], g_ref[...], b_ref[...])
    xo_ref[...] = xn
    if emit_h:
        ho_ref[...] = (xn * (1.0 + sc_ref[0]) + sh_ref[0]).astype(ho_ref.dtype)


def _ln_res(x, mres, mods, ln_g, ln_b, *, alpha, rows_per_group, g0, gate_chunk, next_mods=None,
            scale_chunk=None, shift_chunk=None, h_dtype=BF16, name):
    m, d = x.shape
    tm = _tile(rows_per_group or m, (256, 128, 64, 32, 16, 8))
    emit_h = next_mods is not None
    row = pl.BlockSpec((tm, d), lambda i: (i, 0))
    vec = pl.BlockSpec((1, d), lambda i: (0, 0))
    in_specs = [row, row, pl.BlockSpec((1, 1, d), _group_map(rows_per_group, tm, g0, gate_chunk)), vec, vec]
    args = [x, mres, mods, ln_g.reshape(1, d), ln_b.reshape(1, d)]
    out_shape = [jax.ShapeDtypeStruct((m, d), F32)]
    out_specs = [row]
    if emit_h:
        in_specs += [pl.BlockSpec((1, 1, d), _group_map(rows_per_group, tm, g0, scale_chunk)),
                     pl.BlockSpec((1, 1, d), _group_map(rows_per_group, tm, g0, shift_chunk))]
        args += [next_mods, next_mods]
        out_shape.append(jax.ShapeDtypeStruct((m, d), h_dtype))
        out_specs.append(row)
    out = pl.pallas_call(
        functools.partial(_ln_res_kernel, alpha=alpha, emit_h=emit_h),
        out_shape=out_shape, grid=(m // tm,), in_specs=in_specs, out_specs=out_specs,
        compiler_params=_cparams(("parallel",)), name=name,
    )(*args)
    return out if emit_h else (out[0], None)


def _rope_tables(t):
    nf = HEAD_DIM // 4
    rows = jnp.repeat(jnp.arange(t // GRID_W, dtype=F32), GRID_W)
    cols = (jnp.arange(t) % GRID_W).astype(F32)
    inv = ROPE_THETA ** (-jnp.arange(nf, dtype=F32) / nf)
    ar, ac = rows[:, None] * inv, cols[:, None] * inv
    cr, sr, cc, sc = jnp.cos(ar), jnp.sin(ar), jnp.cos(ac), jnp.sin(ac)
    zero = jnp.zeros_like(sr)
    cos = jnp.concatenate([cr, cr, cc, cc], axis=-1)
    sin_a = jnp.concatenate([-sr, zero, -sc, zero], axis=-1)
    sin_b = jnp.concatenate([zero, sr, zero, sc], axis=-1)
    return cos, sin_a, sin_b


def _rope(x, cos, sin_a, sin_b):
    return x * cos + pltpu.roll(x, 96, 1) * sin_a + pltpu.roll(x, 32, 1) * sin_b


def _rms(x, gain):
    return x * lax.rsqrt(jnp.mean(x * x, axis=-1, keepdims=True) + EPS) * gain


def _attn_kernel(*refs, n_rep, tc, t, has_lat):
    it = iter(refs)
    q_ref, kc_ref, vc_ref, qg_ref, kg_ref = (next(it) for _ in range(5))
    if has_lat:
        kl_ref, vl_ref, cosk, sak, sbk, cosq, saq, sbq = (next(it) for _ in range(8))
    o_ref, k_scr, v_scr = next(it), next(it), next(it)

    @pl.when(pl.program_id(2) == 0)
    def _():
        k_scr[0:tc, :] = _rms(kc_ref[...].astype(F32), kg_ref[...]).astype(BF16)
        v_scr[0:tc, :] = vc_ref[...].astype(BF16)
        if has_lat:
            kl = _rope(_rms(kl_ref[...].astype(F32), kg_ref[...]), cosk[...], sak[...], sbk[...])
            k_scr[tc:tc + t, :] = kl.astype(BF16)
            v_scr[tc:tc + t, :] = vl_ref[...].astype(BF16)

    scale = HEAD_DIM ** -0.5
    keys = k_scr[...]
    vals = v_scr[...]
    for r in range(n_rep):
        q = _rms(q_ref[:, r * HEAD_DIM:(r + 1) * HEAD_DIM].astype(F32), qg_ref[...])
        if has_lat:
            q = _rope(q, cosq[...], saq[...], sbq[...])
        s = lax.dot_general((q * scale).astype(BF16), keys, (((1,), (1,)), ((), ())), preferred_element_type=F32)
        e = jnp.exp(s - jnp.max(s, axis=-1, keepdims=True))
        den = jnp.sum(e, axis=-1, keepdims=True)
        o = jnp.dot(e.astype(BF16), vals, preferred_element_type=F32) / den
        o_ref[:, r * HEAD_DIM:(r + 1) * HEAD_DIM] = o.astype(o_ref.dtype)


def _attention(zq, zc, zl, lay, q_norm, k_norm, rope, *, batch, tq_rows, tc, t, name):
    has_lat = zl is not None
    n_rep = (lay.src["att_q"][1] // HEAD_DIM) // ATT_KV_HEADS
    qw = n_rep * HEAD_DIM
    tq = _tile(tq_rows, (256, 128, 64, 32, 16, 8))
    nq = tq_rows // tq
    qb, kb, vb = lay.blk("att_q", qw), lay.blk("att_k", HEAD_DIM), lay.blk("att_v", HEAD_DIM)
    vec = pl.BlockSpec((1, HEAD_DIM), lambda b, g, i: (0, 0))
    in_specs = [pl.BlockSpec((tq, qw), lambda b, g, i: (b * nq + i, qb + g)),
                pl.BlockSpec((tc, HEAD_DIM), lambda b, g, i: (b, kb + g)),
                pl.BlockSpec((tc, HEAD_DIM), lambda b, g, i: (b, vb + g)), vec, vec]
    args = [zq, zc, zc, q_norm.reshape(1, -1), k_norm.reshape(1, -1)]
    n_keys = tc
    if has_lat:
        full = pl.BlockSpec((t, HEAD_DIM), lambda b, g, i: (0, 0))
        part = pl.BlockSpec((tq, HEAD_DIM), lambda b, g, i: (i, 0))
        in_specs += [pl.BlockSpec((t, HEAD_DIM), lambda b, g, i: (b, kb + g)),
                     pl.BlockSpec((t, HEAD_DIM), lambda b, g, i: (b, vb + g)), full, full, full, part, part, part]
        args += [zl, zl, *rope, *rope]
        n_keys = tc + t
    return pl.pallas_call(
        functools.partial(_attn_kernel, n_rep=n_rep, tc=tc, t=t, has_lat=has_lat),
        out_shape=jax.ShapeDtypeStruct((batch * tq_rows, lay.src["att_q"][1]), BF16),
        grid=(batch, ATT_KV_HEADS, nq),
        in_specs=in_specs,
        out_specs=pl.BlockSpec((tq, qw), lambda b, g, i: (b * nq + i, g)),
        scratch_shapes=[pltpu.VMEM((n_keys, HEAD_DIM), BF16), pltpu.VMEM((n_keys, HEAD_DIM), BF16)],
        compiler_params=_cparams(("parallel", "parallel", "arbitrary")),
        name=name,
    )(*args)


def _split3(x):
    hi = x.astype(BF16)
    r1 = x - hi.astype(F32)
    mid = r1.astype(BF16)
    lo = (r1 - mid.astype(F32)).astype(BF16)
    return hi, mid, lo


def _place(parts):
    n = len(parts)
    zero = jnp.zeros_like(parts[0])
    return jnp.concatenate([jnp.concatenate([p if j == c else zero for j in range(n)], axis=1)
                            for c, p in enumerate(parts)], axis=0)


def _rec_block(b, q_s, k_s, v_s, d, *, forward, want_out, blk):
    nch = blk // CHUNK
    rows = slice(b * blk, (b + 1) * blk)
    q, k, g, v = q_s[rows, :], k_s[rows, :], d["g"][rows, :], v_s[rows, :]
    shift = CHUNK.bit_length() - 1
    ri = lax.broadcasted_iota(jnp.int32, (blk, blk), 0)
    ci = lax.broadcasted_iota(jnp.int32, (blk, blk), 1)
    same = jnp.right_shift(ri, shift) == jnp.right_shift(ci, shift)
    tri = (jnp.where(same, ri - ci, -1) >= 0).astype(BF16)
    c3 = jnp.dot(tri, jnp.concatenate(_split3(g), axis=1), preferred_element_type=F32)
    cum = c3[:, :HEAD_DIM] + c3[:, HEAD_DIM:2 * HEAD_DIM] + c3[:, 2 * HEAD_DIM:]
    qe, ke, qd, kd = [], [], [], []
    for c in range(nch):
        cs = slice(c * CHUNK, (c + 1) * CHUNK)
        if forward:
            a = cum[cs]
            ref, tot = a[CHUNK // 2:CHUNK // 2 + 1], a[CHUNK - 1:CHUNK]
        else:
            a = cum[cs][CHUNK - 1:CHUNK] - cum[cs] + g[cs]
            ref, tot = a[CHUNK // 2 - 1:CHUNK // 2], a[0:1]
        qe.append(q[cs] * jnp.exp(a - ref))
        ke.append(k[cs] * jnp.exp(ref - a))
        qd.append((q[cs] * jnp.exp(a)).astype(BF16))
        kd.append((k[cs] * jnp.exp(tot - a)).astype(BF16))
        d["dec"][(b * nch + c) * 8:(b * nch + c + 1) * 8, :] = jnp.broadcast_to(jnp.exp(tot), (8, HEAD_DIM))
    if want_out:
        s = lax.dot_general(jnp.concatenate(qe, axis=0).astype(BF16), jnp.concatenate(ke, axis=0).astype(BF16),
                            (((1,), (1,)), ((), ())), preferred_element_type=F32)
        if forward:
            mask = jnp.where(same, ri - ci, -1) >= 0
        else:
            mask = jnp.where(same, ci - ri, 0) > 0
        d["o"][rows, :] = jnp.dot(jnp.where(mask, s, 0.0).astype(BF16), v, preferred_element_type=F32)
        d["qd"][rows, :] = _place(qd)
    v_t = jnp.transpose(v.astype(F32)).astype(BF16)
    d["dst"][b] = jnp.dot(v_t, _place(kd), preferred_element_type=F32)


_REC_DIR_SCRATCH = ("o", "dst", "stc", "qd", "dec")


def _rec_kernel(*refs, kind, ctx_out, t, tc, blk):
    it = iter(refs)
    ql, kl, vl, gl, qc, kc, vc, gc = (next(it) for _ in range(8))
    if kind == "gla":
        lrl, lrc, w2, b2, norm = (next(it) for _ in range(5))
    else:
        lg, cos, sa, sb = (next(it) for _ in range(4))
    yl = next(it)
    yc = next(it) if ctx_out else None
    q_s, k_s, v_s, gf_s, gb_s = (next(it) for _ in range(5))
    dirs = [dict(zip(("g",) + _REC_DIR_SCRATCH, (g_s,) + tuple(next(it) for _ in _REC_DIR_SCRATCH)))
            for g_s in (gf_s, gb_s)]
    scale = HEAD_DIM ** -0.5
    nch, nb, nbc = blk // CHUNK, (tc + t) // blk, tc // blk
    ctx_rows, lat_rows = slice(0, tc), slice(tc, tc + t)

    for rows, q_in, k_in, v_in, lr_in, roped in ((ctx_rows, qc, kc, vc, lrc if kind == "gla" else None, False),
                                                 (lat_rows, ql, kl, vl, lrl if kind == "gla" else None, True)):
        q, k = q_in[...].astype(F32), k_in[...].astype(F32)
        if kind == "gla":
            logits = jnp.dot(lr_in[...].astype(BF16), w2[...], preferred_element_type=F32) + b2[...]
            lsig = (jnp.minimum(logits, 0.0) - jnp.log1p(jnp.exp(-jnp.abs(logits)))) * (1.0 / GLA_GATE_NORM)
            gf_s[rows, :] = lsig[:, :HEAD_DIM]
            gb_s[rows, :] = lsig[:, HEAD_DIM:]
            q = q * scale
        else:
            gf_s[rows, :] = jnp.broadcast_to(lg[...], q.shape)
            gb_s[rows, :] = jnp.broadcast_to(lg[...], q.shape)
            if roped:
                q = _rope(q, cos[...], sa[...], sb[...])
                k = _rope(k, cos[...], sa[...], sb[...])
            k = k * scale
        q_s[rows, :] = q
        k_s[rows, :] = k
        v_s[rows, :] = v_in[...].astype(BF16)

    for b in range(nb):
        for d, fwd in zip(dirs, (True, False)):
            _rec_block(b, q_s, k_s, v_s, d, forward=fwd, want_out=ctx_out or b >= nbc, blk=blk)

    for d, fwd in zip(dirs, (True, False)):
        order = range(nb) if fwd else list(range(nbc - 1, -1, -1)) + list(range(nb - 1, nbc - 1, -1))
        st = jnp.zeros(d["dst"].shape[1:2] + (HEAD_DIM,), F32)
        for b in order:
            for c in (range(nch) if fwd else range(nch - 1, -1, -1)):
                lanes = slice(c * HEAD_DIM, (c + 1) * HEAD_DIM)
                d["stc"][b, :, lanes] = st.astype(BF16)
                st = st * d["dec"][(b * nch + c) * 8:(b * nch + c) * 8 + 1, :] + d["dst"][b, :, lanes]

    for b in range(0 if ctx_out else nbc, nb):
        rows = slice(b * blk, (b + 1) * blk)
        for d in dirs:
            d["o"][rows, :] += lax.dot_general(d["qd"][rows, :], d["stc"][b], (((1,), (1,)), ((), ())),
                                               preferred_element_type=F32)

    def finish(rows, g_ref, y_ref):
        o = dirs[0]["o"][rows, :] + dirs[1]["o"][rows, :]
        if kind == "gla":
            o = _rms(o, norm[...])
        else:
            mu = jnp.mean(o, axis=-1, keepdims=True)
            dlt = o - mu
            o = dlt * lax.rsqrt(jnp.mean(dlt * dlt, axis=-1, keepdims=True) + EPS)
        g = g_ref[...].astype(F32)
        y_ref[...] = (o * (g * jax.nn.sigmoid(g))).astype(y_ref.dtype)

    finish(lat_rows, gl, yl)
    if ctx_out:
        finish(ctx_rows, gc, yc)


def _recurrence(zl, zc, lay, kind, params, *, batch, t, tc, ctx_out, name):
    dv = lay.src[kind + "_v"][1] // REC_HEADS
    qb, kb = lay.blk(kind + "_q", HEAD_DIM), lay.blk(kind + "_k", HEAD_DIM)
    vb, gb = lay.blk(kind + "_v", dv), lay.blk(kind + "_g", dv)

    def blocks(rows):
        return [pl.BlockSpec((rows, HEAD_DIM), lambda b, h: (b, qb + h)),
                pl.BlockSpec((rows, HEAD_DIM), lambda b, h: (b, kb + h)),
                pl.BlockSpec((rows, dv), lambda b, h: (b, vb + h)),
                pl.BlockSpec((rows, dv), lambda b, h: (b, gb + h))]

    in_specs = blocks(t) + blocks(tc)
    args = [zl] * 4 + [zc] * 4
    if kind == "gla":
        w2cat, b2cat, norm = params
        lrb = lay.blk("gla_lr", LANES)
        in_specs += [pl.BlockSpec((t, LANES), lambda b, h: (b, lrb)), pl.BlockSpec((tc, LANES), lambda b, h: (b, lrb)),
                     pl.BlockSpec((None, LANES, 2 * HEAD_DIM), lambda b, h: (h, 0, 0)),
                     pl.BlockSpec((None, 1, 2 * HEAD_DIM), lambda b, h: (h, 0, 0)),
                     pl.BlockSpec((1, dv), lambda b, h: (0, 0))]
        args += [zl, zc, w2cat, b2cat, norm]
    else:
        lg, rope = params
        full = pl.BlockSpec((t, HEAD_DIM), lambda b, h: (0, 0))
        in_specs += [pl.BlockSpec((None, 1, HEAD_DIM), lambda b, h: (h, 0, 0)), full, full, full]
        args += [lg, *rope]
    out_shape = [jax.ShapeDtypeStruct((batch * t, REC_HEADS * dv), BF16)]
    out_specs = [pl.BlockSpec((t, dv), lambda b, h: (b, h))]
    if ctx_out:
        out_shape.append(jax.ShapeDtypeStruct((batch * tc, REC_HEADS * dv), BF16))
        out_specs.append(pl.BlockSpec((tc, dv), lambda b, h: (b, h)))

    blk = math.gcd(math.gcd(t, tc), REC_BLOCK)
    rows, nb, nch = tc + t, (tc + t) // blk, blk // CHUNK
    per_dir = {"o": pltpu.VMEM((rows, dv), F32), "dst": pltpu.VMEM((nb, dv, nch * HEAD_DIM), F32),
               "stc": pltpu.VMEM((nb, dv, nch * HEAD_DIM), BF16), "qd": pltpu.VMEM((rows, nch * HEAD_DIM), BF16),
               "dec": pltpu.VMEM((nb * nch * 8, HEAD_DIM), F32)}
    scratch = ([pltpu.VMEM((rows, HEAD_DIM), F32)] * 2 + [pltpu.VMEM((rows, dv), BF16)]
               + [pltpu.VMEM((rows, HEAD_DIM), F32)] * 2 + [per_dir[n] for n in _REC_DIR_SCRATCH] * 2)
    out = pl.pallas_call(
        functools.partial(_rec_kernel, kind=kind, ctx_out=ctx_out, t=t, tc=tc, blk=blk),
        out_shape=out_shape, grid=(batch, REC_HEADS), in_specs=in_specs, out_specs=out_specs,
        scratch_shapes=scratch,
        compiler_params=_cparams(("parallel", "parallel")), name=name,
    )(*args)
    return out[0], (out[1] if ctx_out else None)


S5_LANE_BLOCK = 1024
S5_TIME_BLOCK = 32


def _s5_prep_kernel(lr_ref, li_ref, ls_ref, br_ref, bi_ref, ar_ref, ai_ref, bbr_ref, bbi_ref):
    lam_r, lam_i = lr_ref[0], li_ref[0]
    dt = jnp.exp(ls_ref[0])
    mag = jnp.exp(lam_r * dt)
    a_r = mag * jnp.cos(lam_i * dt)
    a_i = mag * jnp.sin(lam_i * dt)
    den = lam_r * lam_r + lam_i * lam_i
    z_r = ((a_r - 1.0) * lam_r + a_i * lam_i) / den
    z_i = (a_i * lam_r - (a_r - 1.0) * lam_i) / den
    ar_ref[0] = jnp.broadcast_to(a_r, ar_ref.shape[1:])
    ai_ref[0] = jnp.broadcast_to(a_i, ai_ref.shape[1:])
    nblk, _, cols = br_ref.shape[1:]
    for j in range(nblk):
        zr = z_r[:, j * cols:(j + 1) * cols]
        zi = z_i[:, j * cols:(j + 1) * cols]
        bbr_ref[0, j] = (zr * br_ref[0, j] - zi * bi_ref[0, j]).astype(BF16)
        bbi_ref[0, j] = (zr * bi_ref[0, j] + zi * br_ref[0, j]).astype(BF16)


def _block_diag(w, n_blk):
    *lead, g, r, c = w.shape
    gl = g // n_blk
    w = w.reshape(*lead, n_blk, gl, r, c)
    eye = jnp.eye(gl, dtype=w.dtype)
    out = w[..., :, :, None, :] * eye[:, None, :, None]
    return out.reshape(*lead, n_blk, gl * r, gl * c)


def _s5_prep(lam_re, lam_im, log_step, b_re, b_im, batch):
    _, g, p = lam_re.shape
    n_blk = (g * p) // S5_LANE_BLOCK
    flat = lambda a: a.reshape(2, 1, g * p)
    ls = jnp.broadcast_to(log_step[:, :, None], (2, g, p))
    brd = _block_diag(jnp.swapaxes(b_re, -1, -2), n_blk)
    bid = _block_diag(jnp.swapaxes(b_im, -1, -2), n_blk)
    vec = pl.BlockSpec((1, 1, g * p), lambda d: (d, 0, 0))
    mat = pl.BlockSpec((1,) + brd.shape[1:], lambda d: (d, 0, 0, 0))
    tab = pl.BlockSpec((1, batch, g * p), lambda d: (d, 0, 0))
    return pl.pallas_call(
        _s5_prep_kernel,
        out_shape=[jax.ShapeDtypeStruct((2, batch, g * p), F32)] * 2 + [jax.ShapeDtypeStruct(brd.shape, BF16)] * 2,
        grid=(2,), in_specs=[vec, vec, vec, mat, mat], out_specs=[tab, tab, mat, mat],
        compiler_params=_cparams(("parallel",)), name="s5_prep",
    )(flat(lam_re), flat(lam_im), flat(ls), brd, bid)


def _s5_kernel(uf_ref, ub_ref, ar_ref, ai_ref, bbr_ref, bbi_ref, cr_ref, ci_ref, yf_ref, yb_ref,
               xr_scr, xi_scr, sr_scr, si_scr, *, batch, tb):
    n_blk, kin, _ = bbr_ref.shape[1:]
    kout = cr_ref.shape[3]

    @pl.when(pl.program_id(0) == 0)
    def _():
        sr_scr[...] = jnp.zeros_like(sr_scr)
        si_scr[...] = jnp.zeros_like(si_scr)

    dirs = tuple(enumerate(((uf_ref, yf_ref), (ub_ref, yb_ref))))
    blocks = [slice(j * S5_LANE_BLOCK, (j + 1) * S5_LANE_BLOCK) for j in range(n_blk)]
    for d, (u_ref, _) in dirs:
        u = u_ref[...].astype(BF16)
        for j, lanes in enumerate(blocks):
            uj = u[:, j * kin:(j + 1) * kin]
            xr_scr[d, :, lanes] = jnp.dot(uj, bbr_ref[d, j], preferred_element_type=F32)
            xi_scr[d, :, lanes] = jnp.dot(uj, bbi_ref[d, j], preferred_element_type=F32)
    for d, _ in dirs:
        for lanes in blocks:
            a_r, a_i = ar_ref[d, :, lanes], ai_ref[d, :, lanes]
            s_r, s_i = sr_scr[d, :, lanes], si_scr[d, :, lanes]
            for s in (range(tb) if d == 0 else range(tb - 1, -1, -1)):
                rows = slice(s * batch, (s + 1) * batch)
                s_r, s_i = (a_r * s_r - a_i * s_i + xr_scr[d, rows, lanes],
                            a_r * s_i + a_i * s_r + xi_scr[d, rows, lanes])
                xr_scr[d, rows, lanes] = s_r
                xi_scr[d, rows, lanes] = s_i
            sr_scr[d, :, lanes] = s_r
            si_scr[d, :, lanes] = s_i
    for d, (_, y_ref) in dirs:
        for j, lanes in enumerate(blocks):
            y = (jnp.dot(xr_scr[d, :, lanes].astype(BF16), cr_ref[d, j], preferred_element_type=F32)
                 - jnp.dot(xi_scr[d, :, lanes].astype(BF16), ci_ref[d, j], preferred_element_type=F32))
            y_ref[:, j * kout:(j + 1) * kout] = y


def _s5_scan(u_tm, tabs, c_re_bd, c_im_bd, *, batch, t, tc):
    a_r, a_i, bbr, bbi = tabs
    rows, width = u_tm.shape
    tb = _tile(math.gcd(t, tc), (S5_TIME_BLOCK, 16, 8, 4, 2, 1))
    nbc, nbl = tc // tb, t // tb
    nb = nbc + nbl
    blk = tb * batch

    def bwd(i):
        return jnp.where(i < nbc, nbc - 1 - i, nb - 1 - (i - nbc))

    full4 = lambda a: pl.BlockSpec(a.shape, lambda i: (0, 0, 0, 0), pipeline_mode=pl.Buffered(1))
    full3 = lambda a: pl.BlockSpec(a.shape, lambda i: (0, 0, 0), pipeline_mode=pl.Buffered(1))
    n_state = a_r.shape[-1]
    return pl.pallas_call(
        functools.partial(_s5_kernel, batch=batch, tb=tb),
        out_shape=[jax.ShapeDtypeStruct((rows, width), F32)] * 2,
        grid=(nb,),
        in_specs=[pl.BlockSpec((blk, width), lambda i: (i, 0)), pl.BlockSpec((blk, width), lambda i: (bwd(i), 0)),
                  full3(a_r), full3(a_i), full4(bbr), full4(bbi), full4(c_re_bd), full4(c_im_bd)],
        out_specs=[pl.BlockSpec((blk, width), lambda i: (i, 0)), pl.BlockSpec((blk, width), lambda i: (bwd(i), 0))],
        scratch_shapes=[pltpu.VMEM((2, blk, n_state), F32)] * 2 + [pltpu.VMEM((2, batch, n_state), F32)] * 2,
        compiler_params=_cparams(("arbitrary",)), name="s5_scan",
    )(u_tm, u_tm, a_r, a_i, bbr, bbi, c_re_bd, c_im_bd)


def _glu_kernel(u_ref, yf_ref, yb_ref, d_ref, w_ref, b_ref, o_ref):
    y = u_ref[...].astype(F32) * d_ref[...] + yf_ref[...] + yb_ref[...]
    z = jax.nn.gelu(y, approximate=True)
    gate = jnp.dot(z.astype(BF16), w_ref[...], preferred_element_type=F32) + b_ref[...]
    o_ref[...] = (z * jax.nn.sigmoid(gate)).astype(o_ref.dtype)


def _s5_glu(u_tm, yf, yb, d, glu_w, glu_b):
    rows, width = u_tm.shape
    tm = _tile(rows, (512, 256, 128, 64, 32, 16, 8))
    row = pl.BlockSpec((tm, width), lambda i: (i, 0))
    vec = pl.BlockSpec((1, width), lambda i: (0, 0))
    return pl.pallas_call(
        _glu_kernel, out_shape=jax.ShapeDtypeStruct((rows, width), BF16), grid=(rows // tm,),
        in_specs=[row, row, row, vec, pl.BlockSpec((width, width), lambda i: (0, 0)), vec], out_specs=row,
        compiler_params=_cparams(("parallel",)), name="s5_glu",
    )(u_tm, yf, yb, d.reshape(1, width), glu_w, glu_b.reshape(1, width))


def _merge_kernel(y0, y1, y2, y3, g0, g1, g2, g3, wb_ref, o_ref):
    acc = None
    for i, (y_ref, g_ref) in enumerate(((y0, g0), (y1, g1), (y2, g2), (y3, g3))):
        term = jax.nn.sigmoid(g_ref[...].astype(F32)) * jnp.dot(y_ref[...], wb_ref[i], preferred_element_type=F32)
        acc = term if acc is None else acc + term
    o_ref[...] = acc.astype(o_ref.dtype)


def _merge(ys, z, lay, w_branch, *, name):
    m, mw = ys[0].shape
    d = w_branch.shape[-1]
    tm = _tile(m, (1024, 512, 256, 128, 64, 32, 16))
    tn = _tile(d, (512, 256, 128))
    g_off = lay.off["gates"]
    assert g_off % tn == 0

    def gate_spec(i):
        return pl.BlockSpec((tm, tn), lambda r, c: (r, (g_off + i * d) // tn + c))

    yspec = pl.BlockSpec((tm, mw), lambda r, c: (r, 0))
    return pl.pallas_call(
        _merge_kernel, out_shape=jax.ShapeDtypeStruct((m, d), BF16), grid=(m // tm, d // tn),
        in_specs=[yspec] * N_BRANCH + [gate_spec(i) for i in range(N_BRANCH)]
        + [pl.BlockSpec((N_BRANCH, mw, tn), lambda r, c: (0, 0, c))],
        out_specs=pl.BlockSpec((tm, tn), lambda r, c: (r, c)),
        compiler_params=_cparams(("parallel", "arbitrary")), name=name,
    )(*ys, z, z, z, z, w_branch)


def _ffn_up_kernel(h_ref, w1_ref, w3_ref, o_ref):
    h = h_ref[...]
    a = jnp.dot(h, w1_ref[...], preferred_element_type=F32)
    b = jnp.dot(h, w3_ref[...], preferred_element_type=F32)
    o_ref[...] = (a * jax.nn.sigmoid(a) * b).astype(o_ref.dtype)


def _ffn_up(h, w1, w3, *, name):
    m, k = h.shape
    f = w1.shape[-1]
    tm, tn = _mm_tiles(m, k, f, 2, 2, 2, n_w=2)
    w_spec = pl.BlockSpec((k, tn), lambda i, j: (0, j))
    return pl.pallas_call(
        _ffn_up_kernel, out_shape=jax.ShapeDtypeStruct((m, f), BF16), grid=(m // tm, f // tn),
        in_specs=[pl.BlockSpec((tm, k), lambda i, j: (i, 0)), w_spec, w_spec],
        out_specs=pl.BlockSpec((tm, tn), lambda i, j: (i, j)),
        compiler_params=_cparams(("parallel", "arbitrary")), name=name,
    )(h, w1, w3)


MOE_ROW_TILE = 512
MOE_GATHER_TILE = 256
MOE_COMBINE_TILE = 128
DMA_UNROLL = 8


def _router_kernel(h_ref, rh_ref, rl_ref, o_ref, *, n_experts):
    h = h_ref[...]
    hh = h.astype(BF16)
    hl = (h - hh.astype(F32)).astype(BF16)
    logits = (jnp.dot(hh, rh_ref[...], preferred_element_type=F32) + jnp.dot(hh, rl_ref[...], preferred_element_type=F32)
              + jnp.dot(hl, rh_ref[...], preferred_element_type=F32))
    lane = lax.broadcasted_iota(jnp.int32, logits.shape, 1).astype(F32)
    neg = jnp.float32(-jnp.inf)
    x1 = jnp.where(lane < n_experts, logits, neg)
    m1 = jnp.max(x1, axis=-1, keepdims=True)
    i1 = jnp.min(jnp.where(x1 == m1, lane, float(LANES)), axis=-1, keepdims=True)
    x2 = jnp.where(lane == i1, neg, x1)
    m2 = jnp.max(x2, axis=-1, keepdims=True)
    i2 = jnp.min(jnp.where(x2 == m2, lane, float(LANES)), axis=-1, keepdims=True)
    e2 = jnp.exp(m2 - m1)
    w1 = 1.0 / (1.0 + e2)
    o_ref[...] = (jnp.where(lane == 0.0, w1, 0.0) + jnp.where(lane == 1.0, e2 * w1, 0.0)
                  + jnp.where(lane == 2.0, i1, 0.0) + jnp.where(lane == 3.0, i2, 0.0))


def _router(h, router):
    m, d = h.shape
    n_experts = router.shape[-1]
    rp = jnp.zeros((d, LANES), F32).at[:, :n_experts].set(router)
    rh = rp.astype(BF16)
    rl = (rp - rh.astype(F32)).astype(BF16)
    tm = _tile(m, (256, 128, 64, 32, 16, 8))
    mat = pl.BlockSpec((d, LANES), lambda i: (0, 0))
    return pl.pallas_call(
        functools.partial(_router_kernel, n_experts=n_experts),
        out_shape=jax.ShapeDtypeStruct((m, LANES), F32), grid=(m // tm,),
        in_specs=[pl.BlockSpec((tm, d), lambda i: (i, 0)), mat, mat],
        out_specs=pl.BlockSpec((tm, LANES), lambda i: (i, 0)),
        compiler_params=_cparams(("parallel",)), name="moe_router",
    )(h, rh, rl)


def _moe_plan(expert_idx, n_experts, tile):
    m = expert_idx.shape[0]
    flat = expert_idx.reshape(-1)
    onehot = (flat[:, None] == jnp.arange(n_experts, dtype=jnp.int32)[None, :]).astype(jnp.int32)
    rank = jnp.take_along_axis(jnp.cumsum(onehot, axis=0) - onehot, flat[:, None], axis=1)[:, 0]
    count = jnp.sum(onehot, axis=0)
    group = ((count + tile - 1) // tile) * tile
    group_end = jnp.cumsum(group)
    dest = (group_end - group)[flat] + rank
    rows = m * TOP_K + n_experts * tile
    row_src = jnp.zeros((rows,), jnp.int32).at[dest].set(jnp.arange(m * TOP_K, dtype=jnp.int32) // TOP_K)
    n_tiles = rows // tile
    n_used = group_end[-1] // tile
    tile_id = jnp.arange(n_tiles, dtype=jnp.int32)
    owner = jnp.searchsorted(group_end, jnp.minimum(tile_id, n_used - 1) * tile, side="right").astype(jnp.int32)
    return dest.reshape(m, TOP_K), row_src, jnp.minimum(owner, n_experts - 1), n_used.reshape(1).astype(jnp.int32)


def _row_copy(src_hbm, src_row, buf, slot, dst_row, sem):
    return pltpu.make_async_copy(src_hbm.at[pl.ds(src_row, 1), :], buf.at[slot, pl.ds(dst_row, 1), :], sem.at[slot])


def _gather_rows(idx_ref, src_hbm, buf, sem, slot, n):
    def body(r, carry):
        _row_copy(src_hbm, idx_ref[0, r], buf, slot, r, sem).start()
        return carry
    lax.fori_loop(0, n, body, 0, unroll=DMA_UNROLL)


def _wait_rows(src_hbm, buf, sem, slot, n):
    def body(r, carry):
        _row_copy(src_hbm, 0, buf, slot, r, sem).wait()
        return carry
    lax.fori_loop(0, n, body, 0, unroll=DMA_UNROLL)


def _gather_kernel(cur_ref, nxt_ref, src_hbm, o_ref, buf, sem):
    s, n_steps = pl.program_id(0), pl.num_programs(0)
    n = buf.shape[1]

    @pl.when(s == 0)
    def _():
        _gather_rows(cur_ref, src_hbm, buf, sem, 0, n)

    @pl.when(s + 1 < n_steps)
    def _():
        _gather_rows(nxt_ref, src_hbm, buf, sem, (s + 1) % 2, n)

    slot = s % 2
    _wait_rows(src_hbm, buf, sem, slot, n)
    o_ref[...] = buf[slot].astype(o_ref.dtype)


def _moe_gather(h, row_src):
    d = h.shape[1]
    rows = row_src.shape[0]
    tg = _tile(rows, (MOE_GATHER_TILE, 128, 64, 32, 16, 8))
    steps = rows // tg
    idx = row_src.reshape(steps, 1, tg)
    smem = lambda f: pl.BlockSpec((None, 1, tg), f, memory_space=pltpu.SMEM)
    return pl.pallas_call(
        _gather_kernel, out_shape=jax.ShapeDtypeStruct((rows, d), BF16), grid=(steps,),
        in_specs=[smem(lambda s: (s, 0, 0)), smem(lambda s: (jnp.minimum(s + 1, steps - 1), 0, 0)),
                  pl.BlockSpec(memory_space=pl.ANY)],
        out_specs=pl.BlockSpec((tg, d), lambda s: (s, 0)),
        scratch_shapes=[pltpu.VMEM((2, tg, d), F32), pltpu.SemaphoreType.DMA((2,))],
        compiler_params=_cparams(("arbitrary",)), name="moe_gather",
    )(idx, idx, h)


def _moe_up_kernel(te_ref, nu_ref, x_ref, w1_ref, w3_ref, o_ref):
    del te_ref

    @pl.when(pl.program_id(1) < nu_ref[0])
    def _():
        _ffn_up_kernel(x_ref, w1_ref, w3_ref, o_ref)

    @pl.when(pl.program_id(1) >= nu_ref[0])
    def _():
        o_ref[...] = jnp.zeros_like(o_ref)


def _moe_dn_kernel(te_ref, nu_ref, g_ref, w_ref, o_ref):
    del te_ref

    @pl.when(pl.program_id(1) < nu_ref[0])
    def _():
        o_ref[...] = jnp.dot(g_ref[...], w_ref[...], preferred_element_type=F32).astype(o_ref.dtype)

    @pl.when(pl.program_id(1) >= nu_ref[0])
    def _():
        o_ref[...] = jnp.zeros_like(o_ref)


def _moe_grouped(body, a, weights, tile_expert, n_used, *, tn, out_dtype, name):
    rows, k = a.shape
    n = weights[0].shape[-1]
    tm = MOE_ROW_TILE
    used = lambda i, nu: jnp.minimum(i, nu[0] - 1)
    w_spec = pl.BlockSpec((None, k, tn), lambda j, i, te, nu: (te[i], 0, j))
    return pl.pallas_call(
        body, out_shape=jax.ShapeDtypeStruct((rows, n), out_dtype),
        grid_spec=pltpu.PrefetchScalarGridSpec(
            num_scalar_prefetch=2, grid=(n // tn, rows // tm),
            in_specs=[pl.BlockSpec((tm, k), lambda j, i, te, nu: (used(i, nu), 0))] + [w_spec] * len(weights),
            out_specs=pl.BlockSpec((tm, tn), lambda j, i, te, nu: (i, j))),
        compiler_params=_cparams(("arbitrary", "arbitrary")), name=name,
    )(tile_expert, n_used, a, *weights)


def _moe_combine_kernel(*refs, alpha, emit_h):
    it = iter(refs)
    cur_ref, nxt_ref, x_ref, wts_ref, gate_ref, g_ref, b_ref = (next(it) for _ in range(7))
    if emit_h:
        sc_ref, sh_ref = next(it), next(it)
    y_hbm, xo_ref = next(it), next(it)
    ho_ref = next(it) if emit_h else None
    buf, sem = next(it), next(it)
    s, n_steps = pl.program_id(0), pl.num_programs(0)
    n = buf.shape[1]
    tm = n // TOP_K

    @pl.when(s == 0)
    def _():
        _gather_rows(cur_ref, y_hbm, buf, sem, 0, n)

    @pl.when(s + 1 < n_steps)
    def _():
        _gather_rows(nxt_ref, y_hbm, buf, sem, (s + 1) % 2, n)

    slot = s % 2
    _wait_rows(y_hbm, buf, sem, slot, n)
    f = wts_ref[:, 0:1] * buf[slot, 0:tm, :] + wts_ref[:, 1:2] * buf[slot, tm:2 * tm, :]
    xn = _layer_norm(alpha * x_ref[...] + gate_ref[0] * f, g_ref[...], b_ref[...])
    xo_ref[...] = xn
    if emit_h:
        ho_ref[...] = (xn * (1.0 + sc_ref[0]) + sh_ref[0]).astype(ho_ref.dtype)


def _moe_combine_ln(x, y_sorted, dest, wts, mods, ln_g, ln_b, *, alpha, rows_per_group, g0, gate_chunk,
                    next_mods, scale_chunk, shift_chunk, name):
    m, d = x.shape
    tm = _tile(rows_per_group or m, (MOE_COMBINE_TILE, 64, 32, 16, 8))
    steps = m // tm
    idx = jnp.swapaxes(dest.reshape(steps, tm, TOP_K), 1, 2).reshape(steps, 1, TOP_K * tm)
    emit_h = next_mods is not None
    row = pl.BlockSpec((tm, d), lambda s: (s, 0))
    vec = pl.BlockSpec((1, d), lambda s: (0, 0))
    smem = lambda f: pl.BlockSpec((None, 1, TOP_K * tm), f, memory_space=pltpu.SMEM)
    in_specs = [smem(lambda s: (s, 0, 0)), smem(lambda s: (jnp.minimum(s + 1, steps - 1), 0, 0)), row,
                pl.BlockSpec((tm, LANES), lambda s: (s, 0)),
                pl.BlockSpec((1, 1, d), _group_map(rows_per_group, tm, g0, gate_chunk)), vec, vec]
    args = [idx, idx, x, wts, mods, ln_g.reshape(1, d), ln_b.reshape(1, d)]
    out_shape = [jax.ShapeDtypeStruct((m, d), F32)]
    out_specs = [row]
    if emit_h:
        in_specs += [pl.BlockSpec((1, 1, d), _group_map(rows_per_group, tm, g0, scale_chunk)),
                     pl.BlockSpec((1, 1, d), _group_map(rows_per_group, tm, g0, shift_chunk))]
        args += [next_mods, next_mods]
        out_shape.append(jax.ShapeDtypeStruct((m, d), BF16))
        out_specs.append(row)
    in_specs.append(pl.BlockSpec(memory_space=pl.ANY))
    args.append(y_sorted)
    out = pl.pallas_call(
        functools.partial(_moe_combine_kernel, alpha=alpha, emit_h=emit_h),
        out_shape=out_shape, grid=(steps,), in_specs=in_specs, out_specs=out_specs,
        scratch_shapes=[pltpu.VMEM((2, TOP_K * tm, d), F32), pltpu.SemaphoreType.DMA((2,))],
        compiler_params=_cparams(("arbitrary",)), name=name,
    )(*args)
    return out if emit_h else (out[0], None)


def _moe_ln(x, h, mods, router, w1, w3, w2, ln_g, ln_b, *, alpha, rows_per_group, g0, next_mods, name):
    n_experts = router.shape[-1]
    route = _router(h, router)
    expert_idx = route[:, TOP_K:2 * TOP_K].astype(jnp.int32)
    dest, row_src, tile_expert, n_used = _moe_plan(expert_idx, n_experts, MOE_ROW_TILE)
    xs = _moe_gather(h, row_src)
    up = _moe_grouped(_moe_up_kernel, xs, (w1, w3), tile_expert, n_used, tn=_tile(w1.shape[-1], (1024, 512, 256, 128)),
                      out_dtype=BF16, name=name + "_up")
    ys = _moe_grouped(_moe_dn_kernel, up, (w2,), tile_expert, n_used, tn=_tile(w2.shape[-1], (1024, 512, 256, 128)),
                      out_dtype=F32, name=name + "_dn")
    return _moe_combine_ln(x, ys, dest, route, mods, ln_g, ln_b, alpha=alpha, rows_per_group=rows_per_group, g0=g0,
                           gate_chunk=5, next_mods=next_mods, scale_chunk=1, shift_chunk=0, name=name + "_ln")


def kernel(x, c, ctx, c_ctx, w_ada, b_ada, w_in, b_in, attn_q_norm, attn_k_norm, gla_w2, gla_b2, gla_norm,
           s5_lambda_re, s5_lambda_im, s5_log_step, s5_b_re, s5_b_im, s5_c_re, s5_c_im, s5_d, s5_glu_w, s5_glu_b,
           w_branch, w_out, ln1_g, ln1_b, ln2_g, ln2_b, ffn_w1, ffn_w3, ffn_w2, moe_router, moe_w1, moe_w3, moe_w2):
    batch, t, d = x.shape
    tc = ctx.shape[1]
    depth = w_ada.shape[0]
    assert batch + 1 <= MOD_ROWS and t % CHUNK == 0 and tc % CHUNK == 0 and t % GRID_W == 0
    lay = _Layout(d)
    assert lay.in_width == w_in.shape[-1]
    alpha = (2 * depth) ** 0.25
    rope = _rope_tables(t)
    log_gamma = jnp.log1p(-jnp.exp2(-5.0 - jnp.arange(REC_HEADS, dtype=F32)))
    lg_tab = jnp.broadcast_to(log_gamma[:, None, None], (REC_HEADS, 1, HEAD_DIM))
    n_blk = (s5_lambda_re.shape[2] * S5_STATE) // S5_LANE_BLOCK

    xl = x.reshape(batch * t, d)
    xc = ctx.reshape(batch * tc, d)
    cvec = jnp.concatenate([c, c_ctx[None], jnp.zeros((MOD_ROWS - batch - 1, d), F32)], axis=0)

    def mods_of(l):
        return _matmul(cvec, w_ada, b_ada[l], a_silu=True, w_index=l, name=f"ada{l}").reshape(MOD_ROWS, 1, 6 * d)

    mods = mods_of(0)
    hl = _modulate(xl, mods, rows_per_group=t, g0=0, scale_chunk=1, shift_chunk=0, name="mod_lat")
    hc = _modulate(xc, mods, rows_per_group=0, g0=batch, scale_chunk=1, shift_chunk=0, name="mod_ctx")

    for l in range(depth):
        ctx_out = l < depth - 1
        w_in_p = _permute_w_in(w_in, l, lay)
        b_in_p = lay.permute(b_in[l])
        zl = _matmul(hl, w_in_p, b_in_p, out_dtype=BF16, w_rows=True, name=f"in_lat{l}")
        zc = _matmul(hc, w_in_p, b_in_p, n_out=None if ctx_out else lay.small, out_dtype=BF16, w_rows=True,
                     name=f"in_ctx{l}")

        y_att = _attention(zl, zc, zl, lay, attn_q_norm[l], attn_k_norm[l], rope, batch=batch, tq_rows=t, tc=tc, t=t,
                           name=f"attn_lat{l}")
        yc_att = (_attention(zc, zc, None, lay, attn_q_norm[l], attn_k_norm[l], None, batch=batch, tq_rows=tc, tc=tc,
                             t=t, name=f"attn_ctx{l}") if ctx_out else None)

        w2h = gla_w2[l].reshape(2, GLA_RANK, REC_HEADS, HEAD_DIM)
        w2cat = jnp.zeros((REC_HEADS, LANES, 2 * HEAD_DIM), F32)
        for r in range(2):
            w2cat = w2cat.at[:, r * GLA_RANK:(r + 1) * GLA_RANK, r * HEAD_DIM:(r + 1) * HEAD_DIM].set(
                jnp.transpose(w2h[r], (1, 0, 2)))
        b2cat = jnp.transpose(gla_b2[l].reshape(2, REC_HEADS, HEAD_DIM), (1, 0, 2)).reshape(REC_HEADS, 1, 2 * HEAD_DIM)
        y_gla, yc_gla = _recurrence(zl, zc, lay, "gla", (w2cat.astype(BF16), b2cat, gla_norm[l].reshape(1, -1)),
                                    batch=batch, t=t, tc=tc, ctx_out=ctx_out, name=f"gla{l}")
        y_ret, yc_ret = _recurrence(zl, zc, lay, "ret", (lg_tab, rope), batch=batch, t=t, tc=tc, ctx_out=ctx_out,
                                    name=f"ret{l}")

        mw = lay.src["s5_u"][1]
        u0 = lay.off["s5_u"]
        u_tm = jnp.concatenate([jnp.swapaxes(zc[:, u0:u0 + mw].reshape(batch, tc, mw), 0, 1),
                                jnp.swapaxes(zl[:, u0:u0 + mw].reshape(batch, t, mw), 0, 1)], axis=0)
        u_tm = u_tm.reshape((tc + t) * batch, mw)
        tabs = _s5_prep(s5_lambda_re[l], s5_lambda_im[l], s5_log_step[l], s5_b_re[l], s5_b_im[l], batch)
        c_re_bd = _block_diag(jnp.swapaxes(s5_c_re[l], -1, -2), n_blk).astype(BF16)
        c_im_bd = _block_diag(jnp.swapaxes(s5_c_im[l], -1, -2), n_blk).astype(BF16)
        yf, yb = _s5_scan(u_tm, tabs, c_re_bd, c_im_bd, batch=batch, t=t, tc=tc)
        y_s5_tm = _s5_glu(u_tm, yf, yb, s5_d[l], s5_glu_w[l].astype(BF16), s5_glu_b[l])
        y_s5_bm = jnp.swapaxes(y_s5_tm.reshape(tc + t, batch, mw), 0, 1)
        y_s5 = y_s5_bm[:, tc:].reshape(batch * t, mw)
        yc_s5 = y_s5_bm[:, :tc].reshape(batch * tc, mw)

        wb = w_branch[l].astype(BF16)
        wo = w_out[l].astype(BF16)
        is_moe = l % 2 == 1
        h_dtype = F32 if is_moe else BF16
        ml = _matmul(_merge((y_gla, y_att, y_ret, y_s5), zl, lay, wb, name=f"merge_lat{l}"), wo, name=f"out_lat{l}")
        xl, hl = _ln_res(xl, ml, mods, ln1_g[l], ln1_b[l], alpha=alpha, rows_per_group=t, g0=0, gate_chunk=2,
                         next_mods=mods, scale_chunk=4, shift_chunk=3, h_dtype=h_dtype, name=f"ln1_lat{l}")
        if ctx_out:
            mc = _matmul(_merge((yc_gla, yc_att, yc_ret, yc_s5), zc, lay, wb, name=f"merge_ctx{l}"), wo,
                         name=f"out_ctx{l}")
            xc, hc = _ln_res(xc, mc, mods, ln1_g[l], ln1_b[l], alpha=alpha, rows_per_group=0, g0=batch, gate_chunk=2,
                             next_mods=mods, scale_chunk=4, shift_chunk=3, h_dtype=h_dtype, name=f"ln1_ctx{l}")

        i = l // 2
        next_mods = mods_of(l + 1) if l + 1 < depth else None
        if is_moe:
            w1, w3, w2 = moe_w1[i].astype(BF16), moe_w3[i].astype(BF16), moe_w2[i].astype(BF16)
            xl, hl = _moe_ln(xl, hl, mods, moe_router[i], w1, w3, w2, ln2_g[l], ln2_b[l], alpha=alpha,
                             rows_per_group=t, g0=0, next_mods=next_mods, name=f"moe_lat{l}")
            if ctx_out:
                xc, hc = _moe_ln(xc, hc, mods, moe_router[i], w1, w3, w2, ln2_g[l], ln2_b[l], alpha=alpha,
                                 rows_per_group=0, g0=batch, next_mods=next_mods, name=f"moe_ctx{l}")
        else:
            w1, w3, w2 = ffn_w1[i].astype(BF16), ffn_w3[i].astype(BF16), ffn_w2[i].astype(BF16)
            fl = _matmul(_ffn_up(hl, w1, w3, name=f"ffn_up_lat{l}"), w2, name=f"ffn_dn_lat{l}")
            xl, hl = _ln_res(xl, fl, mods, ln2_g[l], ln2_b[l], alpha=alpha, rows_per_group=t, g0=0, gate_chunk=5,
                             next_mods=next_mods, scale_chunk=1, shift_chunk=0, name=f"ln2_lat{l}")
            if ctx_out:
                fc = _matmul(_ffn_up(hc, w1, w3, name=f"ffn_up_ctx{l}"), w2, name=f"ffn_dn_ctx{l}")
                xc, hc = _ln_res(xc, fc, mods, ln2_g[l], ln2_b[l], alpha=alpha, rows_per_group=0, g0=batch,
                                 gate_chunk=5, next_mods=next_mods, scale_chunk=1, shift_chunk=0, name=f"ln2_ctx{l}")
        mods = next_mods

    return xl.reshape(batch, t, d)
```

```python
import functools
import math

import jax
import jax.numpy as jnp
from jax import lax
from jax.experimental import pallas as pl
from jax.experimental.pallas import tpu as pltpu

F32 = jnp.float32
BF16 = jnp.bfloat16

V7X_VMEM_BYTES = 64 * 1024 * 1024
VMEM_LIMIT = V7X_VMEM_BYTES - 8 * 1024 * 1024
TILE_BUDGET = 44 * 1024 * 1024
LANES = 128

HEAD_DIM = 128
ATT_KV_HEADS = 2
REC_HEADS = 4
GLA_RANK = 16
GLA_GATE_NORM = 16.0
CHUNK = 64
REC_BLOCK = 256
S5_GROUP = 16
S5_STATE = 64
N_BRANCH = 4
TOP_K = 2
ROPE_THETA = 10000.0
GRID_W = 64
EPS = 1e-6
MOD_ROWS = 16


def _cparams(sem):
    return pltpu.CompilerParams(dimension_semantics=sem, vmem_limit_bytes=VMEM_LIMIT)


def _tile(n, prefs):
    for t in prefs:
        if n % t == 0:
            return t
    return n


def _mm_tiles(m, k, n, a_bytes, w_bytes, o_bytes, n_w=1):
    wide = tuple((tm, tn) for tm in (64, 32, 16, 8) for tn in (1024, 512, 256, 128))
    for tm, tn in ((1024, 1024), (1024, 512), (512, 1024), (512, 512), (256, 512), (256, 256), (128, 256),
                   (128, 128)) + wide:
        if m % tm or n % tn:
            continue
        need = 2 * (tm * k * a_bytes + n_w * k * tn * w_bytes + tm * tn * o_bytes) + n_w * tm * tn * 4
        if need <= TILE_BUDGET:
            return tm, tn
    raise ValueError(f"no matmul tiling for {(m, k, n)}")


class _Layout:
    def __init__(self, d):
        mw = d // 4
        hk = REC_HEADS * HEAD_DIM
        kv = ATT_KV_HEADS * HEAD_DIM
        src_sizes = (mw, kv, kv, hk, hk, mw, mw, 2 * GLA_RANK, hk, hk, mw, mw, mw, N_BRANCH * d)
        names = ("att_q", "att_k", "att_v", "gla_q", "gla_k", "gla_v", "gla_g", "gla_lr",
                 "ret_q", "ret_k", "ret_v", "ret_g", "s5_u", "gates")
        self.src = {}
        off = 0
        for nm, sz in zip(names, src_sizes):
            self.src[nm] = (off, sz)
            off += sz
        self.in_width = off
        order = tuple(nm for nm in names if nm not in ("gla_lr", "gates")) + ("gla_lr",)
        self.off = {}
        self.plan = []
        pos = 0
        for nm in order:
            width = max(self.src[nm][1], LANES)
            self.off[nm] = pos
            self.plan.append((nm, self.src[nm][1]))
            if width > self.src[nm][1]:
                self.plan.append((None, width - self.src[nm][1]))
            pos += width
        small = -(-pos // 1024) * 1024
        if small > pos:
            self.plan.append((None, small - pos))
        self.small = small
        self.off["gates"] = small
        self.plan.append(("gates", self.src["gates"][1]))
        self.width = small + self.src["gates"][1]

    def blk(self, name, width):
        off = self.off[name]
        assert off % width == 0, (name, off, width)
        return off // width

    def permute(self, w):
        parts = []
        for nm, sz in self.plan:
            if nm is None:
                parts.append(jnp.zeros(w.shape[:-1] + (sz,), w.dtype))
            else:
                o = self.src[nm][0]
                parts.append(w[..., o:o + sz])
        return jnp.concatenate(parts, axis=-1)


W_IN_ROW_TILE = 512


def _permute_w_in_kernel(wt_hbm, o_ref, buf, sem, *, layer, j_plain, j_lr, lr_off, lr_w):
    s, n_steps = pl.program_id(0), pl.num_programs(0)
    tn = o_ref.shape[0]

    def tile_copy(j, slot):
        shifted = jnp.where(j < j_lr, j, j - 1) * tn + lr_w
        r0 = jnp.where(j < j_plain, j * tn, jnp.where(j == j_lr, lr_off, shifted))
        return pltpu.make_async_copy(wt_hbm.at[layer, pl.ds(pl.multiple_of(r0, 8), tn), :], buf.at[slot], sem.at[slot])

    @pl.when(s == 0)
    def _():
        tile_copy(s, 0).start()

    @pl.when(s + 1 < n_steps)
    def _():
        tile_copy(s + 1, (s + 1) % 2).start()

    slot = s % 2
    tile_copy(s, slot).wait()
    row = lax.broadcasted_iota(jnp.int32, o_ref.shape, 0)
    keep = row < jnp.where(s == j_lr, lr_w, tn)
    o_ref[...] = jnp.where(keep, buf[slot], 0.0).astype(BF16)


def _permute_w_in(w_in, l, lay):
    wt = jnp.swapaxes(w_in, 1, 2)
    d = wt.shape[2]
    tn = W_IN_ROW_TILE
    lr_off, lr_w = lay.src["gla_lr"]
    g_off = lay.src["gates"][0]
    j_plain, j_lr = lr_off // tn, lay.off["gla_lr"] // tn
    assert lr_off % tn == 0 and lay.off["gla_lr"] % tn == 0 and lay.small - lay.off["gla_lr"] == tn
    assert lay.off["gla_lr"] - lr_off == g_off - (lr_off + lr_w) and lay.small - g_off == tn - lr_w
    assert lay.width % tn == 0 and lr_w % 8 == 0 and lr_off + tn <= lay.in_width
    return pl.pallas_call(
        functools.partial(_permute_w_in_kernel, layer=l, j_plain=j_plain, j_lr=j_lr, lr_off=lr_off, lr_w=lr_w),
        out_shape=jax.ShapeDtypeStruct((lay.width, d), BF16), grid=(lay.width // tn,),
        in_specs=[pl.BlockSpec(memory_space=pl.ANY)],
        out_specs=pl.BlockSpec((tn, d), lambda s: (s, 0)),
        scratch_shapes=[pltpu.VMEM((2, tn, d), F32), pltpu.SemaphoreType.DMA((2,))],
        compiler_params=_cparams(("arbitrary",)), name=f"w_in_perm{l}",
    )(wt)


def _mm_kernel(*refs, has_bias, a_silu, w_rows):
    if has_bias:
        a_ref, w_ref, b_ref, o_ref = refs
    else:
        a_ref, w_ref, o_ref = refs
    a = a_ref[...]
    if a_silu:
        a = a * jax.nn.sigmoid(a)
    contract = (((1,), (1 if w_rows else 0,)), ((), ()))
    acc = lax.dot_general(a.astype(BF16), w_ref[...].astype(BF16), contract, preferred_element_type=F32)
    if has_bias:
        acc = acc + b_ref[...]
    o_ref[...] = acc.astype(o_ref.dtype)


def _matmul(a, w, bias=None, *, n_out=None, out_dtype=F32, a_silu=False, w_index=None, w_rows=False, name):
    m, k = a.shape
    n = n_out or w.shape[0 if w_rows else -1]
    tm, tn = _mm_tiles(m, k, n, a.dtype.itemsize, w.dtype.itemsize, jnp.dtype(out_dtype).itemsize)
    if w_rows:
        w_spec = pl.BlockSpec((tn, k), lambda i, j: (j, 0))
    elif w_index is None:
        w_spec = pl.BlockSpec((k, tn), lambda i, j: (0, j))
    else:
        w_spec = pl.BlockSpec((None, k, tn), lambda i, j: (w_index, 0, j))
    in_specs = [pl.BlockSpec((tm, k), lambda i, j: (i, 0)), w_spec]
    args = [a, w]
    if bias is not None:
        in_specs.append(pl.BlockSpec((1, tn), lambda i, j: (0, j)))
        args.append(bias.reshape(1, -1))
    return pl.pallas_call(
        functools.partial(_mm_kernel, has_bias=bias is not None, a_silu=a_silu, w_rows=w_rows),
        out_shape=jax.ShapeDtypeStruct((m, n), out_dtype),
        grid=(m // tm, n // tn),
        in_specs=in_specs,
        out_specs=pl.BlockSpec((tm, tn), lambda i, j: (i, j)),
        compiler_params=_cparams(("parallel", "arbitrary")),
        name=name,
    )(*args)


def _group_map(rows_per_group, tm, g0, chunk):
    tiles = rows_per_group // tm if rows_per_group else None

    def index_map(i):
        return ((i // tiles + g0) if tiles else g0, 0, chunk)
    return index_map


def _modulate_kernel(x_ref, sc_ref, sh_ref, o_ref):
    o_ref[...] = (x_ref[...] * (1.0 + sc_ref[0]) + sh_ref[0]).astype(o_ref.dtype)


def _modulate(x, mods, *, rows_per_group, g0, scale_chunk, shift_chunk, name):
    m, d = x.shape
    tm = _tile(rows_per_group or m, (512, 256, 128, 64, 32, 16, 8))
    return pl.pallas_call(
        _modulate_kernel,
        out_shape=jax.ShapeDtypeStruct((m, d), BF16),
        grid=(m // tm,),
        in_specs=[pl.BlockSpec((tm, d), lambda i: (i, 0)),
                  pl.BlockSpec((1, 1, d), _group_map(rows_per_group, tm, g0, scale_chunk)),
                  pl.BlockSpec((1, 1, d), _group_map(rows_per_group, tm, g0, shift_chunk))],
        out_specs=pl.BlockSpec((tm, d), lambda i: (i, 0)),
        compiler_params=_cparams(("parallel",)),
        name=name,
    )(x, mods, mods)


def _layer_norm(y, gain, bias):
    mu = jnp.mean(y, axis=-1, keepdims=True)
    dlt = y - mu
    var = jnp.mean(dlt * dlt, axis=-1, keepdims=True)
    return dlt * lax.rsqrt(var + EPS) * gain + bias


def _ln_res_kernel(*refs, alpha, emit_h):
    if emit_h:
        x_ref, m_ref, gate_ref, g_ref, b_ref, sc_ref, sh_ref, xo_ref, ho_ref = refs
    else:
        x_ref, m_ref, gate_ref, g_ref, b_ref, xo_ref = refs
    xn = _layer_norm(alpha * x_ref[...] + gate_ref[0] * m_ref[...], g_ref[...], b_ref[...])
    xo_ref[...] = xn
    if emit_h:
        ho_ref[...] = (xn * (1.0 + sc_ref[0]) + sh_ref[0]).astype(ho_ref.dtype)


def _ln_res(x, mres, mods, ln_g, ln_b, *, alpha, rows_per_group, g0, gate_chunk, next_mods=None,
            scale_chunk=None, shift_chunk=None, h_dtype=BF16, name):
    m, d = x.shape
    tm = _tile(rows_per_group or m, (256, 128, 64, 32, 16, 8))
    emit_h = next_mods is not None
    row = pl.BlockSpec((tm, d), lambda i: (i, 0))
    vec = pl.BlockSpec((1, d), lambda i: (0, 0))
    in_specs = [row, row, pl.BlockSpec((1, 1, d), _group_map(rows_per_group, tm, g0, gate_chunk)), vec, vec]
    args = [x, mres, mods, ln_g.reshape(1, d), ln_b.reshape(1, d)]
    out_shape = [jax.ShapeDtypeStruct((m, d), F32)]
    out_specs = [row]
    if emit_h:
        in_specs += [pl.BlockSpec((1, 1, d), _group_map(rows_per_group, tm, g0, scale_chunk)),
                     pl.BlockSpec((1, 1, d), _group_map(rows_per_group, tm, g0, shift_chunk))]
        args += [next_mods, next_mods]
        out_shape.append(jax.ShapeDtypeStruct((m, d), h_dtype))
        out_specs.append(row)
    out = pl.pallas_call(
        functools.partial(_ln_res_kernel, alpha=alpha, emit_h=emit_h),
        out_shape=out_shape, grid=(m // tm,), in_specs=in_specs, out_specs=out_specs,
        compiler_params=_cparams(("parallel",)), name=name,
    )(*args)
    return out if emit_h else (out[0], None)


def _rope_tables(t):
    nf = HEAD_DIM // 4
    rows = jnp.repeat(jnp.arange(t // GRID_W, dtype=F32), GRID_W)
    cols = (jnp.arange(t) % GRID_W).astype(F32)
    inv = ROPE_THETA ** (-jnp.arange(nf, dtype=F32) / nf)
    ar, ac = rows[:, None] * inv, cols[:, None] * inv
    cr, sr, cc, sc = jnp.cos(ar), jnp.sin(ar), jnp.cos(ac), jnp.sin(ac)
    zero = jnp.zeros_like(sr)
    cos = jnp.concatenate([cr, cr, cc, cc], axis=-1)
    sin_a = jnp.concatenate([-sr, zero, -sc, zero], axis=-1)
    sin_b = jnp.concatenate([zero, sr, zero, sc], axis=-1)
    return cos, sin_a, sin_b


def _rope(x, cos, sin_a, sin_b):
    return x * cos + pltpu.roll(x, 96, 1) * sin_a + pltpu.roll(x, 32, 1) * sin_b


def _rms(x, gain):
    return x * lax.rsqrt(jnp.mean(x * x, axis=-1, keepdims=True) + EPS) * gain


def _attn_kernel(*refs, n_rep, tc, t, has_lat):
    it = iter(refs)
    q_ref, kc_ref, vc_ref, qg_ref, kg_ref = (next(it) for _ in range(5))
    if has_lat:
        kl_ref, vl_ref, cosk, sak, sbk, cosq, saq, sbq = (next(it) for _ in range(8))
    o_ref, k_scr, v_scr = next(it), next(it), next(it)

    @pl.when(pl.program_id(2) == 0)
    def _():
        v_scr[:, HEAD_DIM:] = jnp.ones((v_scr.shape[0], HEAD_DIM), BF16)
        k_scr[0:tc, :] = _rms(kc_ref[...].astype(F32), kg_ref[...]).astype(BF16)
        v_scr[0:tc, 0:HEAD_DIM] = vc_ref[...].astype(BF16)
        if has_lat:
            kl = _rope(_rms(kl_ref[...].astype(F32), kg_ref[...]), cosk[...], sak[...], sbk[...])
            k_scr[tc:tc + t, :] = kl.astype(BF16)
            v_scr[tc:tc + t, 0:HEAD_DIM] = vl_ref[...].astype(BF16)

    scale = HEAD_DIM ** -0.5
    keys = k_scr[...]
    vals = v_scr[...]
    for r in range(n_rep):
        q = _rms(q_ref[:, r * HEAD_DIM:(r + 1) * HEAD_DIM].astype(F32), qg_ref[...])
        if has_lat:
            q = _rope(q, cosq[...], saq[...], sbq[...])
        s = lax.dot_general((q * scale).astype(BF16), keys, (((1,), (1,)), ((), ())), preferred_element_type=F32)
        e = jnp.exp(s - jnp.max(s, axis=-1, keepdims=True)).astype(BF16)
        ov = jnp.dot(e, vals, preferred_element_type=F32)
        o_ref[:, r * HEAD_DIM:(r + 1) * HEAD_DIM] = (ov[:, :HEAD_DIM] / ov[:, HEAD_DIM:]).astype(o_ref.dtype)


def _attention(zq, zc, zl, lay, q_norm, k_norm, rope, *, batch, tq_rows, tc, t, name):
    has_lat = zl is not None
    n_rep = (lay.src["att_q"][1] // HEAD_DIM) // ATT_KV_HEADS
    qw = n_rep * HEAD_DIM
    tq = _tile(tq_rows, (256, 128, 64, 32, 16, 8))
    nq = tq_rows // tq
    qb, kb, vb = lay.blk("att_q", qw), lay.blk("att_k", HEAD_DIM), lay.blk("att_v", HEAD_DIM)
    vec = pl.BlockSpec((1, HEAD_DIM), lambda b, g, i: (0, 0))
    in_specs = [pl.BlockSpec((tq, qw), lambda b, g, i: (b * nq + i, qb + g)),
                pl.BlockSpec((tc, HEAD_DIM), lambda b, g, i: (b, kb + g)),
                pl.BlockSpec((tc, HEAD_DIM), lambda b, g, i: (b, vb + g)), vec, vec]
    args = [zq, zc, zc, q_norm.reshape(1, -1), k_norm.reshape(1, -1)]
    n_keys = tc
    if has_lat:
        full = pl.BlockSpec((t, HEAD_DIM), lambda b, g, i: (0, 0))
        part = pl.BlockSpec((tq, HEAD_DIM), lambda b, g, i: (i, 0))
        in_specs += [pl.BlockSpec((t, HEAD_DIM), lambda b, g, i: (b, kb + g)),
                     pl.BlockSpec((t, HEAD_DIM), lambda b, g, i: (b, vb + g)), full, full, full, part, part, part]
        args += [zl, zl, *rope, *rope]
        n_keys = tc + t
    return pl.pallas_call(
        functools.partial(_attn_kernel, n_rep=n_rep, tc=tc, t=t, has_lat=has_lat),
        out_shape=jax.ShapeDtypeStruct((batch * tq_rows, lay.src["att_q"][1]), BF16),
        grid=(batch, ATT_KV_HEADS, nq),
        in_specs=in_specs,
        out_specs=pl.BlockSpec((tq, qw), lambda b, g, i: (b * nq + i, g)),
        scratch_shapes=[pltpu.VMEM((n_keys, HEAD_DIM), BF16), pltpu.VMEM((n_keys, 2 * HEAD_DIM), BF16)],
        compiler_params=_cparams(("parallel", "parallel", "arbitrary")),
        name=name,
    )(*args)


def _split3(x):
    hi = x.astype(BF16)
    r1 = x - hi.astype(F32)
    mid = r1.astype(BF16)
    lo = (r1 - mid.astype(F32)).astype(BF16)
    return hi, mid, lo


def _place(parts):
    n = len(parts)
    zero = jnp.zeros_like(parts[0])
    return jnp.concatenate([jnp.concatenate([p if j == c else zero for j in range(n)], axis=1)
                            for c, p in enumerate(parts)], axis=0)


def _rec_block(b, q_s, k_s, v_s, d, *, forward, want_out, blk):
    nch = blk // CHUNK
    rows = slice(b * blk, (b + 1) * blk)
    q, k, g, v = q_s[rows, :], k_s[rows, :], d["g"][rows, :], v_s[rows, :]
    shift = CHUNK.bit_length() - 1
    ri = lax.broadcasted_iota(jnp.int32, (blk, blk), 0)
    ci = lax.broadcasted_iota(jnp.int32, (blk, blk), 1)
    same = jnp.right_shift(ri, shift) == jnp.right_shift(ci, shift)
    tri = (jnp.where(same, ri - ci, -1) >= 0).astype(BF16)
    c3 = jnp.dot(tri, jnp.concatenate(_split3(g), axis=1), preferred_element_type=F32)
    cum = c3[:, :HEAD_DIM] + c3[:, HEAD_DIM:2 * HEAD_DIM] + c3[:, 2 * HEAD_DIM:]
    qe, ke, qd, kd = [], [], [], []
    for c in range(nch):
        cs = slice(c * CHUNK, (c + 1) * CHUNK)
        if forward:
            a = cum[cs]
            ref, tot = a[CHUNK // 2:CHUNK // 2 + 1], a[CHUNK - 1:CHUNK]
        else:
            a = cum[cs][CHUNK - 1:CHUNK] - cum[cs] + g[cs]
            ref, tot = a[CHUNK // 2 - 1:CHUNK // 2], a[0:1]
        qe.append(q[cs] * jnp.exp(a - ref))
        ke.append(k[cs] * jnp.exp(ref - a))
        qd.append((q[cs] * jnp.exp(a)).astype(BF16))
        kd.append((k[cs] * jnp.exp(tot - a)).astype(BF16))
        d["dec"][(b * nch + c) * 8:(b * nch + c + 1) * 8, :] = jnp.broadcast_to(jnp.exp(tot), (8, HEAD_DIM))
    if want_out:
        s = lax.dot_general(jnp.concatenate(qe, axis=0).astype(BF16), jnp.concatenate(ke, axis=0).astype(BF16),
                            (((1,), (1,)), ((), ())), preferred_element_type=F32)
        if forward:
            mask = jnp.where(same, ri - ci, -1) >= 0
        else:
            mask = jnp.where(same, ci - ri, 0) > 0
        d["o"][rows, :] = jnp.dot(jnp.where(mask, s, 0.0).astype(BF16), v, preferred_element_type=F32)
        d["qd"][rows, :] = _place(qd)
    v_t = jnp.transpose(v.astype(F32)).astype(BF16)
    d["dst"][b] = jnp.dot(v_t, _place(kd), preferred_element_type=F32)


_REC_DIR_SCRATCH = ("o", "dst", "stc", "qd", "dec")


def _rec_kernel(*refs, kind, ctx_out, t, tc, blk):
    it = iter(refs)
    ql, kl, vl, gl, qc, kc, vc, gc = (next(it) for _ in range(8))
    if kind == "gla":
        lrl, lrc, w2, b2, norm = (next(it) for _ in range(5))
    else:
        lg, cos, sa, sb = (next(it) for _ in range(4))
    yl = next(it)
    yc = next(it) if ctx_out else None
    q_s, k_s, v_s, gf_s, gb_s = (next(it) for _ in range(5))
    dirs = [dict(zip(("g",) + _REC_DIR_SCRATCH, (g_s,) + tuple(next(it) for _ in _REC_DIR_SCRATCH)))
            for g_s in (gf_s, gb_s)]
    scale = HEAD_DIM ** -0.5
    nch, nb, nbc = blk // CHUNK, (tc + t) // blk, tc // blk
    ctx_rows, lat_rows = slice(0, tc), slice(tc, tc + t)

    for rows, q_in, k_in, v_in, lr_in, roped in ((ctx_rows, qc, kc, vc, lrc if kind == "gla" else None, False),
                                                 (lat_rows, ql, kl, vl, lrl if kind == "gla" else None, True)):
        q, k = q_in[...].astype(F32), k_in[...].astype(F32)
        if kind == "gla":
            logits = jnp.dot(lr_in[...].astype(BF16), w2[...], preferred_element_type=F32) + b2[...]
            lsig = (jnp.minimum(logits, 0.0) - jnp.log1p(jnp.exp(-jnp.abs(logits)))) * (1.0 / GLA_GATE_NORM)
            gf_s[rows, :] = lsig[:, :HEAD_DIM]
            gb_s[rows, :] = lsig[:, HEAD_DIM:]
            q = q * scale
        else:
            gf_s[rows, :] = jnp.broadcast_to(lg[...], q.shape)
            gb_s[rows, :] = jnp.broadcast_to(lg[...], q.shape)
            if roped:
                q = _rope(q, cos[...], sa[...], sb[...])
                k = _rope(k, cos[...], sa[...], sb[...])
            k = k * scale
        q_s[rows, :] = q
        k_s[rows, :] = k
        v_s[rows, :] = v_in[...].astype(BF16)

    for b in range(nb):
        for d, fwd in zip(dirs, (True, False)):
            _rec_block(b, q_s, k_s, v_s, d, forward=fwd, want_out=ctx_out or b >= nbc, blk=blk)

    for d, fwd in zip(dirs, (True, False)):
        order = range(nb) if fwd else list(range(nbc - 1, -1, -1)) + list(range(nb - 1, nbc - 1, -1))
        st = jnp.zeros(d["dst"].shape[1:2] + (HEAD_DIM,), F32)
        for b in order:
            for c in (range(nch) if fwd else range(nch - 1, -1, -1)):
                lanes = slice(c * HEAD_DIM, (c + 1) * HEAD_DIM)
                d["stc"][b, :, lanes] = st.astype(BF16)
                st = st * d["dec"][(b * nch + c) * 8:(b * nch + c) * 8 + 1, :] + d["dst"][b, :, lanes]

    for b in range(0 if ctx_out else nbc, nb):
        rows = slice(b * blk, (b + 1) * blk)
        for d in dirs:
            d["o"][rows, :] += lax.dot_general(d["qd"][rows, :], d["stc"][b], (((1,), (1,)), ((), ())),
                                               preferred_element_type=F32)

    def finish(rows, g_ref, y_ref):
        o = dirs[0]["o"][rows, :] + dirs[1]["o"][rows, :]
        if kind == "gla":
            o = _rms(o, norm[...])
        else:
            mu = jnp.mean(o, axis=-1, keepdims=True)
            dlt = o - mu
            o = dlt * lax.rsqrt(jnp.mean(dlt * dlt, axis=-1, keepdims=True) + EPS)
        g = g_ref[...].astype(F32)
        y_ref[...] = (o * (g * jax.nn.sigmoid(g))).astype(y_ref.dtype)

    finish(lat_rows, gl, yl)
    if ctx_out:
        finish(ctx_rows, gc, yc)


def _recurrence(zl, zc, lay, kind, params, *, batch, t, tc, ctx_out, name):
    dv = lay.src[kind + "_v"][1] // REC_HEADS
    qb, kb = lay.blk(kind + "_q", HEAD_DIM), lay.blk(kind + "_k", HEAD_DIM)
    vb, gb = lay.blk(kind + "_v", dv), lay.blk(kind + "_g", dv)

    def blocks(rows):
        return [pl.BlockSpec((rows, HEAD_DIM), lambda b, h: (b, qb + h)),
                pl.BlockSpec((rows, HEAD_DIM), lambda b, h: (b, kb + h)),
                pl.BlockSpec((rows, dv), lambda b, h: (b, vb + h)),
                pl.BlockSpec((rows, dv), lambda b, h: (b, gb + h))]

    in_specs = blocks(t) + blocks(tc)
    args = [zl] * 4 + [zc] * 4
    if kind == "gla":
        w2cat, b2cat, norm = params
        lrb = lay.blk("gla_lr", LANES)
        in_specs += [pl.BlockSpec((t, LANES), lambda b, h: (b, lrb)), pl.BlockSpec((tc, LANES), lambda b, h: (b, lrb)),
                     pl.BlockSpec((None, LANES, 2 * HEAD_DIM), lambda b, h: (h, 0, 0)),
                     pl.BlockSpec((None, 1, 2 * HEAD_DIM), lambda b, h: (h, 0, 0)),
                     pl.BlockSpec((1, dv), lambda b, h: (0, 0))]
        args += [zl, zc, w2cat, b2cat, norm]
    else:
        lg, rope = params
        full = pl.BlockSpec((t, HEAD_DIM), lambda b, h: (0, 0))
        in_specs += [pl.BlockSpec((None, 1, HEAD_DIM), lambda b, h: (h, 0, 0)), full, full, full]
        args += [lg, *rope]
    out_shape = [jax.ShapeDtypeStruct((batch * t, REC_HEADS * dv), BF16)]
    out_specs = [pl.BlockSpec((t, dv), lambda b, h: (b, h))]
    if ctx_out:
        out_shape.append(jax.ShapeDtypeStruct((batch * tc, REC_HEADS * dv), BF16))
        out_specs.append(pl.BlockSpec((tc, dv), lambda b, h: (b, h)))

    blk = math.gcd(math.gcd(t, tc), REC_BLOCK)
    rows, nb, nch = tc + t, (tc + t) // blk, blk // CHUNK
    per_dir = {"o": pltpu.VMEM((rows, dv), F32), "dst": pltpu.VMEM((nb, dv, nch * HEAD_DIM), F32),
               "stc": pltpu.VMEM((nb, dv, nch * HEAD_DIM), BF16), "qd": pltpu.VMEM((rows, nch * HEAD_DIM), BF16),
               "dec": pltpu.VMEM((nb * nch * 8, HEAD_DIM), F32)}
    scratch = ([pltpu.VMEM((rows, HEAD_DIM), F32)] * 2 + [pltpu.VMEM((rows, dv), BF16)]
               + [pltpu.VMEM((rows, HEAD_DIM), F32)] * 2 + [per_dir[n] for n in _REC_DIR_SCRATCH] * 2)
    out = pl.pallas_call(
        functools.partial(_rec_kernel, kind=kind, ctx_out=ctx_out, t=t, tc=tc, blk=blk),
        out_shape=out_shape, grid=(batch, REC_HEADS), in_specs=in_specs, out_specs=out_specs,
        scratch_shapes=scratch,
        compiler_params=_cparams(("parallel", "parallel")), name=name,
    )(*args)
    return out[0], (out[1] if ctx_out else None)


S5_LANE_BLOCK = 1024
S5_TIME_BLOCK = 32


def _s5_prep_kernel(lr_ref, li_ref, ls_ref, br_ref, bi_ref, ar_ref, ai_ref, bbr_ref, bbi_ref):
    lam_r, lam_i = lr_ref[0], li_ref[0]
    dt = jnp.exp(ls_ref[0])
    mag = jnp.exp(lam_r * dt)
    a_r = mag * jnp.cos(lam_i * dt)
    a_i = mag * jnp.sin(lam_i * dt)
    den = lam_r * lam_r + lam_i * lam_i
    z_r = ((a_r - 1.0) * lam_r + a_i * lam_i) / den
    z_i = (a_i * lam_r - (a_r - 1.0) * lam_i) / den
    ar_ref[0] = jnp.broadcast_to(a_r, ar_ref.shape[1:])
    ai_ref[0] = jnp.broadcast_to(a_i, ai_ref.shape[1:])
    nblk, _, cols = br_ref.shape[1:]
    for j in range(nblk):
        zr = z_r[:, j * cols:(j + 1) * cols]
        zi = z_i[:, j * cols:(j + 1) * cols]
        bbr_ref[0, j] = (zr * br_ref[0, j] - zi * bi_ref[0, j]).astype(BF16)
        bbi_ref[0, j] = (zr * bi_ref[0, j] + zi * br_ref[0, j]).astype(BF16)


def _block_diag(w, n_blk):
    *lead, g, r, c = w.shape
    gl = g // n_blk
    w = w.reshape(*lead, n_blk, gl, r, c)
    eye = jnp.eye(gl, dtype=w.dtype)
    out = w[..., :, :, None, :] * eye[:, None, :, None]
    return out.reshape(*lead, n_blk, gl * r, gl * c)


def _s5_prep(lam_re, lam_im, log_step, b_re, b_im, batch):
    _, g, p = lam_re.shape
    n_blk = (g * p) // S5_LANE_BLOCK
    flat = lambda a: a.reshape(2, 1, g * p)
    ls = jnp.broadcast_to(log_step[:, :, None], (2, g, p))
    brd = _block_diag(jnp.swapaxes(b_re, -1, -2), n_blk)
    bid = _block_diag(jnp.swapaxes(b_im, -1, -2), n_blk)
    vec = pl.BlockSpec((1, 1, g * p), lambda d: (d, 0, 0))
    mat = pl.BlockSpec((1,) + brd.shape[1:], lambda d: (d, 0, 0, 0))
    tab = pl.BlockSpec((1, batch, g * p), lambda d: (d, 0, 0))
    return pl.pallas_call(
        _s5_prep_kernel,
        out_shape=[jax.ShapeDtypeStruct((2, batch, g * p), F32)] * 2 + [jax.ShapeDtypeStruct(brd.shape, BF16)] * 2,
        grid=(2,), in_specs=[vec, vec, vec, mat, mat], out_specs=[tab, tab, mat, mat],
        compiler_params=_cparams(("parallel",)), name="s5_prep",
    )(flat(lam_re), flat(lam_im), flat(ls), brd, bid)


def _s5_kernel(uf_ref, ub_ref, ar_ref, ai_ref, bbr_ref, bbi_ref, cr_ref, ci_ref, yf_ref, yb_ref,
               xr_scr, xi_scr, sr_scr, si_scr, *, batch, tb):
    n_blk, kin, _ = bbr_ref.shape[1:]
    kout = cr_ref.shape[3]

    @pl.when(pl.program_id(0) == 0)
    def _():
        sr_scr[...] = jnp.zeros_like(sr_scr)
        si_scr[...] = jnp.zeros_like(si_scr)

    dirs = tuple(enumerate(((uf_ref, yf_ref), (ub_ref, yb_ref))))
    blocks = [slice(j * S5_LANE_BLOCK, (j + 1) * S5_LANE_BLOCK) for j in range(n_blk)]
    for d, (u_ref, _) in dirs:
        u = u_ref[...].astype(BF16)
        for j, lanes in enumerate(blocks):
            uj = u[:, j * kin:(j + 1) * kin]
            xr_scr[d, :, lanes] = jnp.dot(uj, bbr_ref[d, j], preferred_element_type=F32)
            xi_scr[d, :, lanes] = jnp.dot(uj, bbi_ref[d, j], preferred_element_type=F32)
    for d, _ in dirs:
        for lanes in blocks:
            a_r, a_i = ar_ref[d, :, lanes], ai_ref[d, :, lanes]
            s_r, s_i = sr_scr[d, :, lanes], si_scr[d, :, lanes]
            for s in (range(tb) if d == 0 else range(tb - 1, -1, -1)):
                rows = slice(s * batch, (s + 1) * batch)
                s_r, s_i = (a_r * s_r - a_i * s_i + xr_scr[d, rows, lanes],
                            a_r * s_i + a_i * s_r + xi_scr[d, rows, lanes])
                xr_scr[d, rows, lanes] = s_r
                xi_scr[d, rows, lanes] = s_i
            sr_scr[d, :, lanes] = s_r
            si_scr[d, :, lanes] = s_i
    for d, (_, y_ref) in dirs:
        for j, lanes in enumerate(blocks):
            y = (jnp.dot(xr_scr[d, :, lanes].astype(BF16), cr_ref[d, j], preferred_element_type=F32)
                 - jnp.dot(xi_scr[d, :, lanes].astype(BF16), ci_ref[d, j], preferred_element_type=F32))
            y_ref[:, j * kout:(j + 1) * kout] = y


def _s5_scan(u_tm, tabs, c_re_bd, c_im_bd, *, batch, t, tc):
    a_r, a_i, bbr, bbi = tabs
    rows, width = u_tm.shape
    tb = _tile(math.gcd(t, tc), (S5_TIME_BLOCK, 16, 8, 4, 2, 1))
    nbc, nbl = tc // tb, t // tb
    nb = nbc + nbl
    blk = tb * batch

    def bwd(i):
        return jnp.where(i < nbc, nbc - 1 - i, nb - 1 - (i - nbc))

    full4 = lambda a: pl.BlockSpec(a.shape, lambda i: (0, 0, 0, 0), pipeline_mode=pl.Buffered(1))
    full3 = lambda a: pl.BlockSpec(a.shape, lambda i: (0, 0, 0), pipeline_mode=pl.Buffered(1))
    n_state = a_r.shape[-1]
    return pl.pallas_call(
        functools.partial(_s5_kernel, batch=batch, tb=tb),
        out_shape=[jax.ShapeDtypeStruct((rows, width), F32)] * 2,
        grid=(nb,),
        in_specs=[pl.BlockSpec((blk, width), lambda i: (i, 0)), pl.BlockSpec((blk, width), lambda i: (bwd(i), 0)),
                  full3(a_r), full3(a_i), full4(bbr), full4(bbi), full4(c_re_bd), full4(c_im_bd)],
        out_specs=[pl.BlockSpec((blk, width), lambda i: (i, 0)), pl.BlockSpec((blk, width), lambda i: (bwd(i), 0))],
        scratch_shapes=[pltpu.VMEM((2, blk, n_state), F32)] * 2 + [pltpu.VMEM((2, batch, n_state), F32)] * 2,
        compiler_params=_cparams(("arbitrary",)), name="s5_scan",
    )(u_tm, u_tm, a_r, a_i, bbr, bbi, c_re_bd, c_im_bd)


def _glu_kernel(u_ref, yf_ref, yb_ref, d_ref, w_ref, b_ref, o_ref):
    y = u_ref[...].astype(F32) * d_ref[...] + yf_ref[...] + yb_ref[...]
    z = jax.nn.gelu(y, approximate=True)
    gate = jnp.dot(z.astype(BF16), w_ref[...], preferred_element_type=F32) + b_ref[...]
    o_ref[...] = (z * jax.nn.sigmoid(gate)).astype(o_ref.dtype)


def _s5_glu(u_tm, yf, yb, d, glu_w, glu_b):
    rows, width = u_tm.shape
    tm = _tile(rows, (512, 256, 128, 64, 32, 16, 8))
    row = pl.BlockSpec((tm, width), lambda i: (i, 0))
    vec = pl.BlockSpec((1, width), lambda i: (0, 0))
    return pl.pallas_call(
        _glu_kernel, out_shape=jax.ShapeDtypeStruct((rows, width), BF16), grid=(rows // tm,),
        in_specs=[row, row, row, vec, pl.BlockSpec((width, width), lambda i: (0, 0)), vec], out_specs=row,
        compiler_params=_cparams(("parallel",)), name="s5_glu",
    )(u_tm, yf, yb, d.reshape(1, width), glu_w, glu_b.reshape(1, width))


def _merge_kernel(y0, y1, y2, y3, g0, g1, g2, g3, wb_ref, o_ref):
    acc = None
    for i, (y_ref, g_ref) in enumerate(((y0, g0), (y1, g1), (y2, g2), (y3, g3))):
        term = jax.nn.sigmoid(g_ref[...].astype(F32)) * jnp.dot(y_ref[...], wb_ref[i], preferred_element_type=F32)
        acc = term if acc is None else acc + term
    o_ref[...] = acc.astype(o_ref.dtype)


def _merge(ys, z, lay, w_branch, *, name):
    m, mw = ys[0].shape
    d = w_branch.shape[-1]
    tm = _tile(m, (1024, 512, 256, 128, 64, 32, 16))
    tn = _tile(d, (512, 256, 128))
    g_off = lay.off["gates"]
    assert g_off % tn == 0

    def gate_spec(i):
        return pl.BlockSpec((tm, tn), lambda r, c: (r, (g_off + i * d) // tn + c))

    yspec = pl.BlockSpec((tm, mw), lambda r, c: (r, 0))
    return pl.pallas_call(
        _merge_kernel, out_shape=jax.ShapeDtypeStruct((m, d), BF16), grid=(m // tm, d // tn),
        in_specs=[yspec] * N_BRANCH + [gate_spec(i) for i in range(N_BRANCH)]
        + [pl.BlockSpec((N_BRANCH, mw, tn), lambda r, c: (0, 0, c))],
        out_specs=pl.BlockSpec((tm, tn), lambda r, c: (r, c)),
        compiler_params=_cparams(("parallel", "arbitrary")), name=name,
    )(*ys, z, z, z, z, w_branch)


def _ffn_up_kernel(h_ref, w1_ref, w3_ref, o_ref):
    h = h_ref[...]
    a = jnp.dot(h, w1_ref[...], preferred_element_type=F32)
    b = jnp.dot(h, w3_ref[...], preferred_element_type=F32)
    o_ref[...] = (a * jax.nn.sigmoid(a) * b).astype(o_ref.dtype)


def _ffn_up(h, w1, w3, *, name):
    m, k = h.shape
    f = w1.shape[-1]
    tm, tn = _mm_tiles(m, k, f, 2, 2, 2, n_w=2)
    w_spec = pl.BlockSpec((k, tn), lambda i, j: (0, j))
    return pl.pallas_call(
        _ffn_up_kernel, out_shape=jax.ShapeDtypeStruct((m, f), BF16), grid=(m // tm, f // tn),
        in_specs=[pl.BlockSpec((tm, k), lambda i, j: (i, 0)), w_spec, w_spec],
        out_specs=pl.BlockSpec((tm, tn), lambda i, j: (i, j)),
        compiler_params=_cparams(("parallel", "arbitrary")), name=name,
    )(h, w1, w3)


MOE_ROW_TILE = 512
MOE_GATHER_TILE = 256
MOE_COMBINE_TILE = 128
DMA_UNROLL = 8
DMA_PRIORITIES = 2


def _router_kernel(h_ref, rh_ref, rl_ref, o_ref, *, n_experts):
    h = h_ref[...]
    hh = h.astype(BF16)
    hl = (h - hh.astype(F32)).astype(BF16)
    logits = (jnp.dot(hh, rh_ref[...], preferred_element_type=F32) + jnp.dot(hh, rl_ref[...], preferred_element_type=F32)
              + jnp.dot(hl, rh_ref[...], preferred_element_type=F32))
    lane = lax.broadcasted_iota(jnp.int32, logits.shape, 1).astype(F32)
    neg = jnp.float32(-jnp.inf)
    x1 = jnp.where(lane < n_experts, logits, neg)
    m1 = jnp.max(x1, axis=-1, keepdims=True)
    i1 = jnp.min(jnp.where(x1 == m1, lane, float(LANES)), axis=-1, keepdims=True)
    x2 = jnp.where(lane == i1, neg, x1)
    m2 = jnp.max(x2, axis=-1, keepdims=True)
    i2 = jnp.min(jnp.where(x2 == m2, lane, float(LANES)), axis=-1, keepdims=True)
    e2 = jnp.exp(m2 - m1)
    w1 = 1.0 / (1.0 + e2)
    o_ref[...] = (jnp.where(lane == 0.0, w1, 0.0) + jnp.where(lane == 1.0, e2 * w1, 0.0)
                  + jnp.where(lane == 2.0, i1, 0.0) + jnp.where(lane == 3.0, i2, 0.0))


def _router(h, router):
    m, d = h.shape
    n_experts = router.shape[-1]
    rp = jnp.zeros((d, LANES), F32).at[:, :n_experts].set(router)
    rh = rp.astype(BF16)
    rl = (rp - rh.astype(F32)).astype(BF16)
    tm = _tile(m, (256, 128, 64, 32, 16, 8))
    mat = pl.BlockSpec((d, LANES), lambda i: (0, 0))
    return pl.pallas_call(
        functools.partial(_router_kernel, n_experts=n_experts),
        out_shape=jax.ShapeDtypeStruct((m, LANES), F32), grid=(m // tm,),
        in_specs=[pl.BlockSpec((tm, d), lambda i: (i, 0)), mat, mat],
        out_specs=pl.BlockSpec((tm, LANES), lambda i: (i, 0)),
        compiler_params=_cparams(("parallel",)), name="moe_router",
    )(h, rh, rl)


def _moe_plan(expert_idx, n_experts, tile):
    m = expert_idx.shape[0]
    flat = expert_idx.reshape(-1)
    onehot = (flat[:, None] == jnp.arange(n_experts, dtype=jnp.int32)[None, :]).astype(jnp.int32)
    rank = jnp.take_along_axis(jnp.cumsum(onehot, axis=0) - onehot, flat[:, None], axis=1)[:, 0]
    count = jnp.sum(onehot, axis=0)
    group = ((count + tile - 1) // tile) * tile
    group_end = jnp.cumsum(group)
    dest = (group_end - group)[flat] + rank
    rows = m * TOP_K + n_experts * tile
    row_src = jnp.zeros((rows,), jnp.int32).at[dest].set(jnp.arange(m * TOP_K, dtype=jnp.int32) // TOP_K)
    n_tiles = rows // tile
    n_used = group_end[-1] // tile
    tile_id = jnp.arange(n_tiles, dtype=jnp.int32)
    owner = jnp.searchsorted(group_end, jnp.minimum(tile_id, n_used - 1) * tile, side="right").astype(jnp.int32)
    return dest.reshape(m, TOP_K), row_src, jnp.minimum(owner, n_experts - 1), n_used.reshape(1).astype(jnp.int32)


def _row_copy(src_hbm, src_row, buf, slot, dst_row, sem):
    return pltpu.make_async_copy(src_hbm.at[pl.ds(src_row, 1), :], buf.at[slot, pl.ds(dst_row, 1), :], sem.at[slot])


def _gather_rows(idx_ref, src_hbm, buf, sem, slot, n):
    def body(p, carry):
        for prio in range(DMA_PRIORITIES):
            r = DMA_PRIORITIES * p + prio
            _row_copy(src_hbm, idx_ref[0, r], buf, slot, r, sem).start(priority=prio)
        return carry
    lax.fori_loop(0, n // DMA_PRIORITIES, body, 0, unroll=DMA_UNROLL // DMA_PRIORITIES)


def _wait_rows(src_hbm, buf, sem, slot, n):
    def body(r, carry):
        _row_copy(src_hbm, 0, buf, slot, r, sem).wait()
        return carry
    lax.fori_loop(0, n, body, 0, unroll=DMA_UNROLL)


def _gather_kernel(cur_ref, nxt_ref, src_hbm, o_ref, buf, sem):
    s, n_steps = pl.program_id(0), pl.num_programs(0)
    n = buf.shape[1]

    @pl.when(s == 0)
    def _():
        _gather_rows(cur_ref, src_hbm, buf, sem, 0, n)

    @pl.when(s + 1 < n_steps)
    def _():
        _gather_rows(nxt_ref, src_hbm, buf, sem, (s + 1) % 2, n)

    slot = s % 2
    _wait_rows(src_hbm, buf, sem, slot, n)
    o_ref[...] = buf[slot].astype(o_ref.dtype)


def _moe_gather(h, row_src):
    d = h.shape[1]
    rows = row_src.shape[0]
    tg = _tile(rows, (MOE_GATHER_TILE, 128, 64, 32, 16, 8))
    steps = rows // tg
    idx = row_src.reshape(steps, 1, tg)
    smem = lambda f: pl.BlockSpec((None, 1, tg), f, memory_space=pltpu.SMEM)
    return pl.pallas_call(
        _gather_kernel, out_shape=jax.ShapeDtypeStruct((rows, d), BF16), grid=(steps,),
        in_specs=[smem(lambda s: (s, 0, 0)), smem(lambda s: (jnp.minimum(s + 1, steps - 1), 0, 0)),
                  pl.BlockSpec(memory_space=pl.ANY)],
        out_specs=pl.BlockSpec((tg, d), lambda s: (s, 0)),
        scratch_shapes=[pltpu.VMEM((2, tg, d), F32), pltpu.SemaphoreType.DMA((2,))],
        compiler_params=_cparams(("arbitrary",)), name="moe_gather",
    )(idx, idx, h)


def _moe_grouped_kernel(te_ref, nu_ref, x_ref, *refs, n_w):
    w_refs, o_ref, w_scrs = refs[:n_w], refs[n_w], refs[n_w + 1:]
    i = pl.program_id(1)

    @pl.when(jnp.logical_or(i == 0, te_ref[i] != te_ref[jnp.maximum(i - 1, 0)]))
    def _():
        for w_ref, w_scr in zip(w_refs, w_scrs):
            w_scr[...] = w_ref[...].astype(BF16)

    @pl.when(i < nu_ref[0])
    def _():
        if n_w == 2:
            _ffn_up_kernel(x_ref, *w_scrs, o_ref)
        else:
            o_ref[...] = jnp.dot(x_ref[...], w_scrs[0][...], preferred_element_type=F32).astype(o_ref.dtype)

    @pl.when(i >= nu_ref[0])
    def _():
        o_ref[...] = jnp.zeros_like(o_ref)


def _moe_grouped(a, weights, tile_expert, n_used, *, tn, out_dtype, name):
    rows, k = a.shape
    n = weights[0].shape[-1]
    tm = MOE_ROW_TILE
    used = lambda i, nu: jnp.minimum(i, nu[0] - 1)
    w_spec = pl.BlockSpec((None, k, tn), lambda j, i, te, nu: (te[i], 0, j))
    return pl.pallas_call(
        functools.partial(_moe_grouped_kernel, n_w=len(weights)),
        out_shape=jax.ShapeDtypeStruct((rows, n), out_dtype),
        grid_spec=pltpu.PrefetchScalarGridSpec(
            num_scalar_prefetch=2, grid=(n // tn, rows // tm),
            in_specs=[pl.BlockSpec((tm, k), lambda j, i, te, nu: (used(i, nu), 0))] + [w_spec] * len(weights),
            out_specs=pl.BlockSpec((tm, tn), lambda j, i, te, nu: (i, j)),
            scratch_shapes=[pltpu.VMEM((k, tn), BF16)] * len(weights)),
        compiler_params=_cparams(("arbitrary", "arbitrary")), name=name,
    )(tile_expert, n_used, a, *weights)


def _moe_combine_kernel(*refs, alpha, emit_h):
    it = iter(refs)
    cur_ref, nxt_ref, x_ref, wts_ref, gate_ref, g_ref, b_ref = (next(it) for _ in range(7))
    if emit_h:
        sc_ref, sh_ref = next(it), next(it)
    y_hbm, xo_ref = next(it), next(it)
    ho_ref = next(it) if emit_h else None
    buf, sem = next(it), next(it)
    s, n_steps = pl.program_id(0), pl.num_programs(0)
    n = buf.shape[1]
    tm = n // TOP_K

    @pl.when(s == 0)
    def _():
        _gather_rows(cur_ref, y_hbm, buf, sem, 0, n)

    @pl.when(s + 1 < n_steps)
    def _():
        _gather_rows(nxt_ref, y_hbm, buf, sem, (s + 1) % 2, n)

    slot = s % 2
    _wait_rows(y_hbm, buf, sem, slot, n)
    f = wts_ref[:, 0:1] * buf[slot, 0:tm, :] + wts_ref[:, 1:2] * buf[slot, tm:2 * tm, :]
    xn = _layer_norm(alpha * x_ref[...] + gate_ref[0] * f, g_ref[...], b_ref[...])
    xo_ref[...] = xn
    if emit_h:
        ho_ref[...] = (xn * (1.0 + sc_ref[0]) + sh_ref[0]).astype(ho_ref.dtype)


def _moe_combine_ln(x, y_sorted, dest, wts, mods, ln_g, ln_b, *, alpha, rows_per_group, g0, gate_chunk,
                    next_mods, scale_chunk, shift_chunk, name):
    m, d = x.shape
    tm = _tile(rows_per_group or m, (MOE_COMBINE_TILE, 64, 32, 16, 8))
    steps = m // tm
    idx = jnp.swapaxes(dest.reshape(steps, tm, TOP_K), 1, 2).reshape(steps, 1, TOP_K * tm)
    emit_h = next_mods is not None
    row = pl.BlockSpec((tm, d), lambda s: (s, 0))
    vec = pl.BlockSpec((1, d), lambda s: (0, 0))
    smem = lambda f: pl.BlockSpec((None, 1, TOP_K * tm), f, memory_space=pltpu.SMEM)
    in_specs = [smem(lambda s: (s, 0, 0)), smem(lambda s: (jnp.minimum(s + 1, steps - 1), 0, 0)), row,
                pl.BlockSpec((tm, LANES), lambda s: (s, 0)),
                pl.BlockSpec((1, 1, d), _group_map(rows_per_group, tm, g0, gate_chunk)), vec, vec]
    args = [idx, idx, x, wts, mods, ln_g.reshape(1, d), ln_b.reshape(1, d)]
    out_shape = [jax.ShapeDtypeStruct((m, d), F32)]
    out_specs = [row]
    if emit_h:
        in_specs += [pl.BlockSpec((1, 1, d), _group_map(rows_per_group, tm, g0, scale_chunk)),
                     pl.BlockSpec((1, 1, d), _group_map(rows_per_group, tm, g0, shift_chunk))]
        args += [next_mods, next_mods]
        out_shape.append(jax.ShapeDtypeStruct((m, d), BF16))
        out_specs.append(row)
    in_specs.append(pl.BlockSpec(memory_space=pl.ANY))
    args.append(y_sorted)
    out = pl.pallas_call(
        functools.partial(_moe_combine_kernel, alpha=alpha, emit_h=emit_h),
        out_shape=out_shape, grid=(steps,), in_specs=in_specs, out_specs=out_specs,
        scratch_shapes=[pltpu.VMEM((2, TOP_K * tm, d), F32), pltpu.SemaphoreType.DMA((2,))],
        compiler_params=_cparams(("arbitrary",)), name=name,
    )(*args)
    return out if emit_h else (out[0], None)


def _moe_ln(x, h, mods, router, w1, w3, w2, ln_g, ln_b, *, alpha, rows_per_group, g0, next_mods, name):
    n_experts = router.shape[-1]
    route = _router(h, router)
    expert_idx = route[:, TOP_K:2 * TOP_K].astype(jnp.int32)
    dest, row_src, tile_expert, n_used = _moe_plan(expert_idx, n_experts, MOE_ROW_TILE)
    xs = _moe_gather(h, row_src)
    up = _moe_grouped(xs, (w1, w3), tile_expert, n_used, tn=_tile(w1.shape[-1], (512, 256, 128)), out_dtype=BF16,
                      name=name + "_up")
    ys = _moe_grouped(up, (w2,), tile_expert, n_used, tn=_tile(w2.shape[-1], (1024, 512, 256, 128)), out_dtype=F32,
                      name=name + "_dn")
    return _moe_combine_ln(x, ys, dest, route, mods, ln_g, ln_b, alpha=alpha, rows_per_group=rows_per_group, g0=g0,
                           gate_chunk=5, next_mods=next_mods, scale_chunk=1, shift_chunk=0, name=name + "_ln")


def kernel(x, c, ctx, c_ctx, w_ada, b_ada, w_in, b_in, attn_q_norm, attn_k_norm, gla_w2, gla_b2, gla_norm,
           s5_lambda_re, s5_lambda_im, s5_log_step, s5_b_re, s5_b_im, s5_c_re, s5_c_im, s5_d, s5_glu_w, s5_glu_b,
           w_branch, w_out, ln1_g, ln1_b, ln2_g, ln2_b, ffn_w1, ffn_w3, ffn_w2, moe_router, moe_w1, moe_w3, moe_w2):
    batch, t, d = x.shape
    tc = ctx.shape[1]
    depth = w_ada.shape[0]
    assert batch + 1 <= MOD_ROWS and t % CHUNK == 0 and tc % CHUNK == 0 and t % GRID_W == 0
    lay = _Layout(d)
    assert lay.in_width == w_in.shape[-1]
    alpha = (2 * depth) ** 0.25
    rope = _rope_tables(t)
    log_gamma = jnp.log1p(-jnp.exp2(-5.0 - jnp.arange(REC_HEADS, dtype=F32)))
    lg_tab = jnp.broadcast_to(log_gamma[:, None, None], (REC_HEADS, 1, HEAD_DIM))
    n_blk = (s5_lambda_re.shape[2] * S5_STATE) // S5_LANE_BLOCK

    xl = x.reshape(batch * t, d)
    xc = ctx.reshape(batch * tc, d)
    cvec = jnp.concatenate([c, c_ctx[None], jnp.zeros((MOD_ROWS - batch - 1, d), F32)], axis=0)

    def mods_of(l):
        return _matmul(cvec, w_ada, b_ada[l], a_silu=True, w_index=l, name=f"ada{l}").reshape(MOD_ROWS, 1, 6 * d)

    mods = mods_of(0)
    hl = _modulate(xl, mods, rows_per_group=t, g0=0, scale_chunk=1, shift_chunk=0, name="mod_lat")
    hc = _modulate(xc, mods, rows_per_group=0, g0=batch, scale_chunk=1, shift_chunk=0, name="mod_ctx")

    for l in range(depth):
        ctx_out = l < depth - 1
        w_in_p = _permute_w_in(w_in, l, lay)
        b_in_p = lay.permute(b_in[l])
        zl = _matmul(hl, w_in_p, b_in_p, out_dtype=BF16, w_rows=True, name=f"in_lat{l}")
        zc = _matmul(hc, w_in_p, b_in_p, n_out=None if ctx_out else lay.small, out_dtype=BF16, w_rows=True,
                     name=f"in_ctx{l}")

        y_att = _attention(zl, zc, zl, lay, attn_q_norm[l], attn_k_norm[l], rope, batch=batch, tq_rows=t, tc=tc, t=t,
                           name=f"attn_lat{l}")
        yc_att = (_attention(zc, zc, None, lay, attn_q_norm[l], attn_k_norm[l], None, batch=batch, tq_rows=tc, tc=tc,
                             t=t, name=f"attn_ctx{l}") if ctx_out else None)

        w2h = gla_w2[l].reshape(2, GLA_RANK, REC_HEADS, HEAD_DIM)
        w2cat = jnp.zeros((REC_HEADS, LANES, 2 * HEAD_DIM), F32)
        for r in range(2):
            w2cat = w2cat.at[:, r * GLA_RANK:(r + 1) * GLA_RANK, r * HEAD_DIM:(r + 1) * HEAD_DIM].set(
                jnp.transpose(w2h[r], (1, 0, 2)))
        b2cat = jnp.transpose(gla_b2[l].reshape(2, REC_HEADS, HEAD_DIM), (1, 0, 2)).reshape(REC_HEADS, 1, 2 * HEAD_DIM)
        y_gla, yc_gla = _recurrence(zl, zc, lay, "gla", (w2cat.astype(BF16), b2cat, gla_norm[l].reshape(1, -1)),
                                    batch=batch, t=t, tc=tc, ctx_out=ctx_out, name=f"gla{l}")
        y_ret, yc_ret = _recurrence(zl, zc, lay, "ret", (lg_tab, rope), batch=batch, t=t, tc=tc, ctx_out=ctx_out,
                                    name=f"ret{l}")

        mw = lay.src["s5_u"][1]
        u0 = lay.off["s5_u"]
        u_tm = jnp.concatenate([jnp.swapaxes(zc[:, u0:u0 + mw].reshape(batch, tc, mw), 0, 1),
                                jnp.swapaxes(zl[:, u0:u0 + mw].reshape(batch, t, mw), 0, 1)], axis=0)
        u_tm = u_tm.reshape((tc + t) * batch, mw)
        tabs = _s5_prep(s5_lambda_re[l], s5_lambda_im[l], s5_log_step[l], s5_b_re[l], s5_b_im[l], batch)
        c_re_bd = _block_diag(jnp.swapaxes(s5_c_re[l], -1, -2), n_blk).astype(BF16)
        c_im_bd = _block_diag(jnp.swapaxes(s5_c_im[l], -1, -2), n_blk).astype(BF16)
        yf, yb = _s5_scan(u_tm, tabs, c_re_bd, c_im_bd, batch=batch, t=t, tc=tc)
        y_s5_tm = _s5_glu(u_tm, yf, yb, s5_d[l], s5_glu_w[l].astype(BF16), s5_glu_b[l])
        y_s5_bm = jnp.swapaxes(y_s5_tm.reshape(tc + t, batch, mw), 0, 1)
        y_s5 = y_s5_bm[:, tc:].reshape(batch * t, mw)
        yc_s5 = y_s5_bm[:, :tc].reshape(batch * tc, mw)

        wb = w_branch[l].astype(BF16)
        wo = w_out[l].astype(BF16)
        is_moe = l % 2 == 1
        h_dtype = F32 if is_moe else BF16
        ml = _matmul(_merge((y_gla, y_att, y_ret, y_s5), zl, lay, wb, name=f"merge_lat{l}"), wo, name=f"out_lat{l}")
        xl, hl = _ln_res(xl, ml, mods, ln1_g[l], ln1_b[l], alpha=alpha, rows_per_group=t, g0=0, gate_chunk=2,
                         next_mods=mods, scale_chunk=4, shift_chunk=3, h_dtype=h_dtype, name=f"ln1_lat{l}")
        if ctx_out:
            mc = _matmul(_merge((yc_gla, yc_att, yc_ret, yc_s5), zc, lay, wb, name=f"merge_ctx{l}"), wo,
                         name=f"out_ctx{l}")
            xc, hc = _ln_res(xc, mc, mods, ln1_g[l], ln1_b[l], alpha=alpha, rows_per_group=0, g0=batch, gate_chunk=2,
                             next_mods=mods, scale_chunk=4, shift_chunk=3, h_dtype=h_dtype, name=f"ln1_ctx{l}")

        i = l // 2
        next_mods = mods_of(l + 1) if l + 1 < depth else None
        if is_moe:
            w1, w3, w2 = moe_w1[i], moe_w3[i], moe_w2[i]
            xl, hl = _moe_ln(xl, hl, mods, moe_router[i], w1, w3, w2, ln2_g[l], ln2_b[l], alpha=alpha,
                             rows_per_group=t, g0=0, next_mods=next_mods, name=f"moe_lat{l}")
            if ctx_out:
                xc, hc = _moe_ln(xc, hc, mods, moe_router[i], w1, w3, w2, ln2_g[l], ln2_b[l], alpha=alpha,
                                 rows_per_group=0, g0=batch, next_mods=next_mods, name=f"moe_ctx{l}")
        else:
            w1, w3, w2 = ffn_w1[i].astype(BF16), ffn_w3[i].astype(BF16), ffn_w2[i].astype(BF16)
            fl = _matmul(_ffn_up(hl, w1, w3, name=f"ffn_up_lat{l}"), w2, name=f"ffn_dn_lat{l}")
            xl, hl = _ln_res(xl, fl, mods, ln2_g[l], ln2_b[l], alpha=alpha, rows_per_group=t, g0=0, gate_chunk=5,
                             next_mods=next_mods, scale_chunk=1, shift_chunk=0, name=f"ln2_lat{l}")
            if ctx_out:
                fc = _matmul(_ffn_up(hc, w1, w3, name=f"ffn_up_ctx{l}"), w2, name=f"ffn_dn_ctx{l}")
                xc, hc = _ln_res(xc, fc, mods, ln2_g[l], ln2_b[l], alpha=alpha, rows_per_group=0, g0=batch,
                                 gate_chunk=5, next_mods=next_mods, scale_chunk=1, shift_chunk=0, name=f"ln2_ctx{l}")
        mods = next_mods

    return xl.reshape(batch, t, d)
```

```python
import functools
import math

import jax
import jax.numpy as jnp
from jax import lax
from jax.experimental import pallas as pl
from jax.experimental.pallas import tpu as pltpu

F32 = jnp.float32
BF16 = jnp.bfloat16

V7X_VMEM_BYTES = 64 * 1024 * 1024
VMEM_LIMIT = V7X_VMEM_BYTES - 8 * 1024 * 1024
TILE_BUDGET = 44 * 1024 * 1024
LANES = 128

HEAD_DIM = 128
ATT_KV_HEADS = 2
REC_HEADS = 4
GLA_RANK = 16
GLA_GATE_NORM = 16.0
CHUNK = 64
REC_BLOCK = 256
S5_GROUP = 16
S5_STATE = 64
N_BRANCH = 4
TOP_K = 2
ROPE_THETA = 10000.0
GRID_W = 64
EPS = 1e-6
MOD_ROWS = 16


def _cparams(sem):
    return pltpu.CompilerParams(dimension_semantics=sem, vmem_limit_bytes=VMEM_LIMIT)


def _tile(n, prefs):
    for t in prefs:
        if n % t == 0:
            return t
    return n


def _mm_tiles(m, k, n, a_bytes, w_bytes, o_bytes, n_w=1):
    wide = tuple((tm, tn) for tm in (64, 32, 16, 8) for tn in (1024, 512, 256, 128))
    for tm, tn in ((1024, 1024), (1024, 512), (512, 1024), (512, 512), (256, 512), (256, 256), (128, 256),
                   (128, 128)) + wide:
        if m % tm or n % tn:
            continue
        need = 2 * (tm * k * a_bytes + n_w * k * tn * w_bytes + tm * tn * o_bytes) + n_w * tm * tn * 4
        if need <= TILE_BUDGET:
            return tm, tn
    raise ValueError(f"no matmul tiling for {(m, k, n)}")


class _Layout:
    def __init__(self, d):
        mw = d // 4
        hk = REC_HEADS * HEAD_DIM
        kv = ATT_KV_HEADS * HEAD_DIM
        src_sizes = (mw, kv, kv, hk, hk, mw, mw, 2 * GLA_RANK, hk, hk, mw, mw, mw, N_BRANCH * d)
        names = ("att_q", "att_k", "att_v", "gla_q", "gla_k", "gla_v", "gla_g", "gla_lr",
                 "ret_q", "ret_k", "ret_v", "ret_g", "s5_u", "gates")
        self.src = {}
        off = 0
        for nm, sz in zip(names, src_sizes):
            self.src[nm] = (off, sz)
            off += sz
        self.in_width = off
        order = tuple(nm for nm in names if nm not in ("gla_lr", "gates")) + ("gla_lr",)
        self.off = {}
        self.plan = []
        pos = 0
        for nm in order:
            width = max(self.src[nm][1], LANES)
            self.off[nm] = pos
            self.plan.append((nm, self.src[nm][1]))
            if width > self.src[nm][1]:
                self.plan.append((None, width - self.src[nm][1]))
            pos += width
        small = -(-pos // 1024) * 1024
        if small > pos:
            self.plan.append((None, small - pos))
        self.small = small
        self.off["gates"] = small
        self.plan.append(("gates", self.src["gates"][1]))
        self.width = small + self.src["gates"][1]

    def blk(self, name, width):
        off = self.off[name]
        assert off % width == 0, (name, off, width)
        return off // width

    def permute(self, w):
        parts = []
        for nm, sz in self.plan:
            if nm is None:
                parts.append(jnp.zeros(w.shape[:-1] + (sz,), w.dtype))
            else:
                o = self.src[nm][0]
                parts.append(w[..., o:o + sz])
        return jnp.concatenate(parts, axis=-1)


W_IN_ROW_TILE = 512


def _permute_w_in_kernel(wt_hbm, o_ref, buf, sem, *, layer, j_plain, j_lr, lr_off, lr_w):
    s, n_steps = pl.program_id(0), pl.num_programs(0)
    tn = o_ref.shape[0]

    def tile_copy(j, slot):
        shifted = jnp.where(j < j_lr, j, j - 1) * tn + lr_w
        r0 = jnp.where(j < j_plain, j * tn, jnp.where(j == j_lr, lr_off, shifted))
        return pltpu.make_async_copy(wt_hbm.at[layer, pl.ds(pl.multiple_of(r0, 8), tn), :], buf.at[slot], sem.at[slot])

    @pl.when(s == 0)
    def _():
        tile_copy(s, 0).start()

    @pl.when(s + 1 < n_steps)
    def _():
        tile_copy(s + 1, (s + 1) % 2).start()

    slot = s % 2
    tile_copy(s, slot).wait()
    row = lax.broadcasted_iota(jnp.int32, o_ref.shape, 0)
    keep = row < jnp.where(s == j_lr, lr_w, tn)
    o_ref[...] = jnp.where(keep, buf[slot], 0.0).astype(BF16)


def _permute_w_in(w_in, l, lay):
    wt = jnp.swapaxes(w_in, 1, 2)
    d = wt.shape[2]
    tn = W_IN_ROW_TILE
    lr_off, lr_w = lay.src["gla_lr"]
    g_off = lay.src["gates"][0]
    j_plain, j_lr = lr_off // tn, lay.off["gla_lr"] // tn
    assert lr_off % tn == 0 and lay.off["gla_lr"] % tn == 0 and lay.small - lay.off["gla_lr"] == tn
    assert lay.off["gla_lr"] - lr_off == g_off - (lr_off + lr_w) and lay.small - g_off == tn - lr_w
    assert lay.width % tn == 0 and lr_w % 8 == 0 and lr_off + tn <= lay.in_width
    return pl.pallas_call(
        functools.partial(_permute_w_in_kernel, layer=l, j_plain=j_plain, j_lr=j_lr, lr_off=lr_off, lr_w=lr_w),
        out_shape=jax.ShapeDtypeStruct((lay.width, d), BF16), grid=(lay.width // tn,),
        in_specs=[pl.BlockSpec(memory_space=pl.ANY)],
        out_specs=pl.BlockSpec((tn, d), lambda s: (s, 0)),
        scratch_shapes=[pltpu.VMEM((2, tn, d), F32), pltpu.SemaphoreType.DMA((2,))],
        compiler_params=_cparams(("arbitrary",)), name=f"w_in_perm{l}",
    )(wt)


def _mm_kernel(*refs, has_bias, a_silu, w_rows):
    if has_bias:
        a_ref, w_ref, b_ref, o_ref = refs
    else:
        a_ref, w_ref, o_ref = refs
    a = a_ref[...]
    if a_silu:
        a = a * jax.nn.sigmoid(a)
    contract = (((1,), (1 if w_rows else 0,)), ((), ()))
    acc = lax.dot_general(a.astype(BF16), w_ref[...].astype(BF16), contract, preferred_element_type=F32)
    if has_bias:
        acc = acc + b_ref[...]
    o_ref[...] = acc.astype(o_ref.dtype)


def _matmul(a, w, bias=None, *, n_out=None, out_dtype=F32, a_silu=False, w_index=None, w_rows=False, name):
    m, k = a.shape
    n = n_out or w.shape[0 if w_rows else -1]
    tm, tn = _mm_tiles(m, k, n, a.dtype.itemsize, w.dtype.itemsize, jnp.dtype(out_dtype).itemsize)
    if w_rows:
        w_spec = pl.BlockSpec((tn, k), lambda i, j: (j, 0))
    elif w_index is None:
        w_spec = pl.BlockSpec((k, tn), lambda i, j: (0, j))
    else:
        w_spec = pl.BlockSpec((None, k, tn), lambda i, j: (w_index, 0, j))
    in_specs = [pl.BlockSpec((tm, k), lambda i, j: (i, 0)), w_spec]
    args = [a, w]
    if bias is not None:
        in_specs.append(pl.BlockSpec((1, tn), lambda i, j: (0, j)))
        args.append(bias.reshape(1, -1))
    return pl.pallas_call(
        functools.partial(_mm_kernel, has_bias=bias is not None, a_silu=a_silu, w_rows=w_rows),
        out_shape=jax.ShapeDtypeStruct((m, n), out_dtype),
        grid=(m // tm, n // tn),
        in_specs=in_specs,
        out_specs=pl.BlockSpec((tm, tn), lambda i, j: (i, j)),
        compiler_params=_cparams(("parallel", "arbitrary")),
        name=name,
    )(*args)


def _group_map(rows_per_group, tm, g0, chunk):
    tiles = rows_per_group // tm if rows_per_group else None

    def index_map(i):
        return ((i // tiles + g0) if tiles else g0, 0, chunk)
    return index_map


def _modulate_kernel(x_ref, sc_ref, sh_ref, o_ref):
    o_ref[...] = (x_ref[...] * (1.0 + sc_ref[0]) + sh_ref[0]).astype(o_ref.dtype)


def _modulate(x, mods, *, rows_per_group, g0, scale_chunk, shift_chunk, name):
    m, d = x.shape
    tm = _tile(rows_per_group or m, (512, 256, 128, 64, 32, 16, 8))
    return pl.pallas_call(
        _modulate_kernel,
        out_shape=jax.ShapeDtypeStruct((m, d), BF16),
        grid=(m // tm,),
        in_specs=[pl.BlockSpec((tm, d), lambda i: (i, 0)),
                  pl.BlockSpec((1, 1, d), _group_map(rows_per_group, tm, g0, scale_chunk)),
                  pl.BlockSpec((1, 1, d), _group_map(rows_per_group, tm, g0, shift_chunk))],
        out_specs=pl.BlockSpec((tm, d), lambda i: (i, 0)),
        compiler_params=_cparams(("parallel",)),
        name=name,
    )(x, mods, mods)


def _layer_norm(y, gain, bias):
    mu = jnp.mean(y, axis=-1, keepdims=True)
    dlt = y - mu
    var = jnp.mean(dlt * dlt, axis=-1, keepdims=True)
    return dlt * lax.rsqrt(var + EPS) * gain + bias


def _ln_res_kernel(*refs, alpha, emit_h):
    if emit_h:
        x_ref, m_ref, gate_ref, g_ref, b_ref, sc_ref, sh_ref, xo_ref, ho_ref = refs
    else:
        x_ref, m_ref, gate_ref, g_ref, b_ref, xo_ref = refs
    xn = _layer_norm(alpha * x_ref[...] + gate_ref[0] * m_ref[...].astype(F32), g_ref[...], b_ref[...])
    xo_ref[...] = xn
    if emit_h:
        ho_ref[...] = (xn * (1.0 + sc_ref[0]) + sh_ref[0]).astype(ho_ref.dtype)


def _ln_res(x, mres, mods, ln_g, ln_b, *, alpha, rows_per_group, g0, gate_chunk, next_mods=None,
            scale_chunk=None, shift_chunk=None, h_dtype=BF16, name):
    m, d = x.shape
    tm = _tile(rows_per_group or m, (256, 128, 64, 32, 16, 8))
    emit_h = next_mods is not None
    row = pl.BlockSpec((tm, d), lambda i: (i, 0))
    vec = pl.BlockSpec((1, d), lambda i: (0, 0))
    in_specs = [row, row, pl.BlockSpec((1, 1, d), _group_map(rows_per_group, tm, g0, gate_chunk)), vec, vec]
    args = [x, mres, mods, ln_g.reshape(1, d), ln_b.reshape(1, d)]
    out_shape = [jax.ShapeDtypeStruct((m, d), F32)]
    out_specs = [row]
    if emit_h:
        in_specs += [pl.BlockSpec((1, 1, d), _group_map(rows_per_group, tm, g0, scale_chunk)),
                     pl.BlockSpec((1, 1, d), _group_map(rows_per_group, tm, g0, shift_chunk))]
        args += [next_mods, next_mods]
        out_shape.append(jax.ShapeDtypeStruct((m, d), h_dtype))
        out_specs.append(row)
    out = pl.pallas_call(
        functools.partial(_ln_res_kernel, alpha=alpha, emit_h=emit_h),
        out_shape=out_shape, grid=(m // tm,), in_specs=in_specs, out_specs=out_specs,
        compiler_params=_cparams(("parallel",)), name=name,
    )(*args)
    return out if emit_h else (out[0], None)


def _rope_tables(t):
    nf = HEAD_DIM // 4
    rows = jnp.repeat(jnp.arange(t // GRID_W, dtype=F32), GRID_W)
    cols = (jnp.arange(t) % GRID_W).astype(F32)
    inv = ROPE_THETA ** (-jnp.arange(nf, dtype=F32) / nf)
    ar, ac = rows[:, None] * inv, cols[:, None] * inv
    cr, sr, cc, sc = jnp.cos(ar), jnp.sin(ar), jnp.cos(ac), jnp.sin(ac)
    zero = jnp.zeros_like(sr)
    cos = jnp.concatenate([cr, cr, cc, cc], axis=-1)
    sin_a = jnp.concatenate([-sr, zero, -sc, zero], axis=-1)
    sin_b = jnp.concatenate([zero, sr, zero, sc], axis=-1)
    return cos, sin_a, sin_b


def _rope(x, cos, sin_a, sin_b):
    return x * cos + pltpu.roll(x, 96, 1) * sin_a + pltpu.roll(x, 32, 1) * sin_b


def _rms(x, gain):
    return x * lax.rsqrt(jnp.mean(x * x, axis=-1, keepdims=True) + EPS) * gain


def _attn_kernel(*refs, n_rep, tc, t, has_lat):
    it = iter(refs)
    q_ref, kc_ref, vc_ref, qg_ref, kg_ref = (next(it) for _ in range(5))
    if has_lat:
        kl_ref, vl_ref, cosk, sak, sbk, cosq, saq, sbq = (next(it) for _ in range(8))
    o_ref, k_scr, v_scr = next(it), next(it), next(it)

    @pl.when(pl.program_id(2) == 0)
    def _():
        v_scr[:, HEAD_DIM:] = jnp.ones((v_scr.shape[0], HEAD_DIM), BF16)
        k_scr[0:tc, :] = _rms(kc_ref[...].astype(F32), kg_ref[...]).astype(BF16)
        v_scr[0:tc, 0:HEAD_DIM] = vc_ref[...].astype(BF16)
        if has_lat:
            kl = _rope(_rms(kl_ref[...].astype(F32), kg_ref[...]), cosk[...], sak[...], sbk[...])
            k_scr[tc:tc + t, :] = kl.astype(BF16)
            v_scr[tc:tc + t, 0:HEAD_DIM] = vl_ref[...].astype(BF16)

    scale = HEAD_DIM ** -0.5
    keys = k_scr[...]
    vals = v_scr[...]
    for r in range(n_rep):
        q = _rms(q_ref[:, r * HEAD_DIM:(r + 1) * HEAD_DIM].astype(F32), qg_ref[...])
        if has_lat:
            q = _rope(q, cosq[...], saq[...], sbq[...])
        s = lax.dot_general((q * scale).astype(BF16), keys, (((1,), (1,)), ((), ())), preferred_element_type=F32)
        e = jnp.exp(s - jnp.max(s, axis=-1, keepdims=True)).astype(BF16)
        ov = jnp.dot(e, vals, preferred_element_type=F32)
        o_ref[:, r * HEAD_DIM:(r + 1) * HEAD_DIM] = (ov[:, :HEAD_DIM] / ov[:, HEAD_DIM:]).astype(o_ref.dtype)


def _attention(zq, zc, zl, lay, q_norm, k_norm, rope, *, batch, tq_rows, tc, t, name):
    has_lat = zl is not None
    n_rep = (lay.src["att_q"][1] // HEAD_DIM) // ATT_KV_HEADS
    qw = n_rep * HEAD_DIM
    tq = _tile(tq_rows, (256, 128, 64, 32, 16, 8))
    nq = tq_rows // tq
    qb, kb, vb = lay.blk("att_q", qw), lay.blk("att_k", HEAD_DIM), lay.blk("att_v", HEAD_DIM)
    vec = pl.BlockSpec((1, HEAD_DIM), lambda b, g, i: (0, 0))
    in_specs = [pl.BlockSpec((tq, qw), lambda b, g, i: (b * nq + i, qb + g)),
                pl.BlockSpec((tc, HEAD_DIM), lambda b, g, i: (b, kb + g)),
                pl.BlockSpec((tc, HEAD_DIM), lambda b, g, i: (b, vb + g)), vec, vec]
    args = [zq, zc, zc, q_norm.reshape(1, -1), k_norm.reshape(1, -1)]
    n_keys = tc
    if has_lat:
        full = pl.BlockSpec((t, HEAD_DIM), lambda b, g, i: (0, 0))
        part = pl.BlockSpec((tq, HEAD_DIM), lambda b, g, i: (i, 0))
        in_specs += [pl.BlockSpec((t, HEAD_DIM), lambda b, g, i: (b, kb + g)),
                     pl.BlockSpec((t, HEAD_DIM), lambda b, g, i: (b, vb + g)), full, full, full, part, part, part]
        args += [zl, zl, *rope, *rope]
        n_keys = tc + t
    return pl.pallas_call(
        functools.partial(_attn_kernel, n_rep=n_rep, tc=tc, t=t, has_lat=has_lat),
        out_shape=jax.ShapeDtypeStruct((batch * tq_rows, lay.src["att_q"][1]), BF16),
        grid=(batch, ATT_KV_HEADS, nq),
        in_specs=in_specs,
        out_specs=pl.BlockSpec((tq, qw), lambda b, g, i: (b * nq + i, g)),
        scratch_shapes=[pltpu.VMEM((n_keys, HEAD_DIM), BF16), pltpu.VMEM((n_keys, 2 * HEAD_DIM), BF16)],
        compiler_params=_cparams(("parallel", "parallel", "arbitrary")),
        name=name,
    )(*args)


def _split3(x):
    hi = x.astype(BF16)
    r1 = x - hi.astype(F32)
    mid = r1.astype(BF16)
    lo = (r1 - mid.astype(F32)).astype(BF16)
    return hi, mid, lo


def _place(parts):
    n = len(parts)
    zero = jnp.zeros_like(parts[0])
    return jnp.concatenate([jnp.concatenate([p if j == c else zero for j in range(n)], axis=1)
                            for c, p in enumerate(parts)], axis=0)


def _rec_block(b, q_s, k_s, v_s, d, *, forward, want_out, blk):
    nch = blk // CHUNK
    rows = slice(b * blk, (b + 1) * blk)
    q, k, g, v = q_s[rows, :], k_s[rows, :], d["g"][rows, :], v_s[rows, :]
    shift = CHUNK.bit_length() - 1
    ri = lax.broadcasted_iota(jnp.int32, (blk, blk), 0)
    ci = lax.broadcasted_iota(jnp.int32, (blk, blk), 1)
    same = jnp.right_shift(ri, shift) == jnp.right_shift(ci, shift)
    tri = (jnp.where(same, ri - ci, -1) >= 0).astype(BF16)
    c3 = jnp.dot(tri, jnp.concatenate(_split3(g), axis=1), preferred_element_type=F32)
    cum = c3[:, :HEAD_DIM] + c3[:, HEAD_DIM:2 * HEAD_DIM] + c3[:, 2 * HEAD_DIM:]
    qe, ke, qd, kd = [], [], [], []
    for c in range(nch):
        cs = slice(c * CHUNK, (c + 1) * CHUNK)
        if forward:
            a = cum[cs]
            ref, tot = a[CHUNK // 2:CHUNK // 2 + 1], a[CHUNK - 1:CHUNK]
        else:
            a = cum[cs][CHUNK - 1:CHUNK] - cum[cs] + g[cs]
            ref, tot = a[CHUNK // 2 - 1:CHUNK // 2], a[0:1]
        qe.append(q[cs] * jnp.exp(a - ref))
        ke.append(k[cs] * jnp.exp(ref - a))
        qd.append((q[cs] * jnp.exp(a)).astype(BF16))
        kd.append((k[cs] * jnp.exp(tot - a)).astype(BF16))
        d["dec"][(b * nch + c) * 8:(b * nch + c + 1) * 8, :] = jnp.broadcast_to(jnp.exp(tot), (8, HEAD_DIM))
    if want_out:
        s = lax.dot_general(jnp.concatenate(qe, axis=0).astype(BF16), jnp.concatenate(ke, axis=0).astype(BF16),
                            (((1,), (1,)), ((), ())), preferred_element_type=F32)
        if forward:
            mask = jnp.where(same, ri - ci, -1) >= 0
        else:
            mask = jnp.where(same, ci - ri, 0) > 0
        d["o"][rows, :] = jnp.dot(jnp.where(mask, s, 0.0).astype(BF16), v, preferred_element_type=F32)
        d["qd"][rows, :] = _place(qd)
    v_t = jnp.transpose(v.astype(F32)).astype(BF16)
    d["dst"][b] = jnp.dot(v_t, _place(kd), preferred_element_type=F32)


_REC_DIR_SCRATCH = ("o", "dst", "stc", "qd", "dec")


def _rec_kernel(*refs, kind, ctx_out, t, tc, blk):
    it = iter(refs)
    ql, kl, vl, gl, qc, kc, vc, gc = (next(it) for _ in range(8))
    if kind == "gla":
        lrl, lrc, w2, b2, norm = (next(it) for _ in range(5))
    else:
        lg, cos, sa, sb = (next(it) for _ in range(4))
    yl = next(it)
    yc = next(it) if ctx_out else None
    q_s, k_s, v_s, gf_s, gb_s = (next(it) for _ in range(5))
    dirs = [dict(zip(("g",) + _REC_DIR_SCRATCH, (g_s,) + tuple(next(it) for _ in _REC_DIR_SCRATCH)))
            for g_s in (gf_s, gb_s)]
    scale = HEAD_DIM ** -0.5
    nch, nb, nbc = blk // CHUNK, (tc + t) // blk, tc // blk
    ctx_rows, lat_rows = slice(0, tc), slice(tc, tc + t)

    for rows, q_in, k_in, v_in, lr_in, roped in ((ctx_rows, qc, kc, vc, lrc if kind == "gla" else None, False),
                                                 (lat_rows, ql, kl, vl, lrl if kind == "gla" else None, True)):
        q, k = q_in[...].astype(F32), k_in[...].astype(F32)
        if kind == "gla":
            logits = jnp.dot(lr_in[...].astype(BF16), w2[...], preferred_element_type=F32) + b2[...]
            lsig = (jnp.minimum(logits, 0.0) - jnp.log1p(jnp.exp(-jnp.abs(logits)))) * (1.0 / GLA_GATE_NORM)
            gf_s[rows, :] = lsig[:, :HEAD_DIM]
            gb_s[rows, :] = lsig[:, HEAD_DIM:]
            q = q * scale
        else:
            gf_s[rows, :] = jnp.broadcast_to(lg[...], q.shape)
            gb_s[rows, :] = jnp.broadcast_to(lg[...], q.shape)
            if roped:
                q = _rope(q, cos[...], sa[...], sb[...])
                k = _rope(k, cos[...], sa[...], sb[...])
            k = k * scale
        q_s[rows, :] = q
        k_s[rows, :] = k
        v_s[rows, :] = v_in[...].astype(BF16)

    for b in range(nb):
        for d, fwd in zip(dirs, (True, False)):
            _rec_block(b, q_s, k_s, v_s, d, forward=fwd, want_out=ctx_out or b >= nbc, blk=blk)

    for d, fwd in zip(dirs, (True, False)):
        order = range(nb) if fwd else list(range(nbc - 1, -1, -1)) + list(range(nb - 1, nbc - 1, -1))
        st = jnp.zeros(d["dst"].shape[1:2] + (HEAD_DIM,), F32)
        for b in order:
            for c in (range(nch) if fwd else range(nch - 1, -1, -1)):
                lanes = slice(c * HEAD_DIM, (c + 1) * HEAD_DIM)
                d["stc"][b, :, lanes] = st.astype(BF16)
                st = st * d["dec"][(b * nch + c) * 8:(b * nch + c) * 8 + 1, :] + d["dst"][b, :, lanes]

    for b in range(0 if ctx_out else nbc, nb):
        rows = slice(b * blk, (b + 1) * blk)
        for d in dirs:
            d["o"][rows, :] += lax.dot_general(d["qd"][rows, :], d["stc"][b], (((1,), (1,)), ((), ())),
                                               preferred_element_type=F32)

    def finish(rows, g_ref, y_ref):
        o = dirs[0]["o"][rows, :] + dirs[1]["o"][rows, :]
        if kind == "gla":
            o = _rms(o, norm[...])
        else:
            mu = jnp.mean(o, axis=-1, keepdims=True)
            dlt = o - mu
            o = dlt * lax.rsqrt(jnp.mean(dlt * dlt, axis=-1, keepdims=True) + EPS)
        g = g_ref[...].astype(F32)
        y_ref[...] = (o * (g * jax.nn.sigmoid(g))).astype(y_ref.dtype)

    finish(lat_rows, gl, yl)
    if ctx_out:
        finish(ctx_rows, gc, yc)


def _recurrence(zl, zc, lay, kind, params, *, batch, t, tc, ctx_out, name):
    dv = lay.src[kind + "_v"][1] // REC_HEADS
    qb, kb = lay.blk(kind + "_q", HEAD_DIM), lay.blk(kind + "_k", HEAD_DIM)
    vb, gb = lay.blk(kind + "_v", dv), lay.blk(kind + "_g", dv)

    def blocks(rows):
        return [pl.BlockSpec((rows, HEAD_DIM), lambda b, h: (b, qb + h)),
                pl.BlockSpec((rows, HEAD_DIM), lambda b, h: (b, kb + h)),
                pl.BlockSpec((rows, dv), lambda b, h: (b, vb + h)),
                pl.BlockSpec((rows, dv), lambda b, h: (b, gb + h))]

    in_specs = blocks(t) + blocks(tc)
    args = [zl] * 4 + [zc] * 4
    if kind == "gla":
        w2cat, b2cat, norm = params
        lrb = lay.blk("gla_lr", LANES)
        in_specs += [pl.BlockSpec((t, LANES), lambda b, h: (b, lrb)), pl.BlockSpec((tc, LANES), lambda b, h: (b, lrb)),
                     pl.BlockSpec((None, LANES, 2 * HEAD_DIM), lambda b, h: (h, 0, 0)),
                     pl.BlockSpec((None, 1, 2 * HEAD_DIM), lambda b, h: (h, 0, 0)),
                     pl.BlockSpec((1, dv), lambda b, h: (0, 0))]
        args += [zl, zc, w2cat, b2cat, norm]
    else:
        lg, rope = params
        full = pl.BlockSpec((t, HEAD_DIM), lambda b, h: (0, 0))
        in_specs += [pl.BlockSpec((None, 1, HEAD_DIM), lambda b, h: (h, 0, 0)), full, full, full]
        args += [lg, *rope]
    out_shape = [jax.ShapeDtypeStruct((batch * t, REC_HEADS * dv), BF16)]
    out_specs = [pl.BlockSpec((t, dv), lambda b, h: (b, h))]
    if ctx_out:
        out_shape.append(jax.ShapeDtypeStruct((batch * tc, REC_HEADS * dv), BF16))
        out_specs.append(pl.BlockSpec((tc, dv), lambda b, h: (b, h)))

    blk = math.gcd(math.gcd(t, tc), REC_BLOCK)
    rows, nb, nch = tc + t, (tc + t) // blk, blk // CHUNK
    per_dir = {"o": pltpu.VMEM((rows, dv), F32), "dst": pltpu.VMEM((nb, dv, nch * HEAD_DIM), F32),
               "stc": pltpu.VMEM((nb, dv, nch * HEAD_DIM), BF16), "qd": pltpu.VMEM((rows, nch * HEAD_DIM), BF16),
               "dec": pltpu.VMEM((nb * nch * 8, HEAD_DIM), F32)}
    scratch = ([pltpu.VMEM((rows, HEAD_DIM), F32)] * 2 + [pltpu.VMEM((rows, dv), BF16)]
               + [pltpu.VMEM((rows, HEAD_DIM), F32)] * 2 + [per_dir[n] for n in _REC_DIR_SCRATCH] * 2)
    out = pl.pallas_call(
        functools.partial(_rec_kernel, kind=kind, ctx_out=ctx_out, t=t, tc=tc, blk=blk),
        out_shape=out_shape, grid=(batch, REC_HEADS), in_specs=in_specs, out_specs=out_specs,
        scratch_shapes=scratch,
        compiler_params=_cparams(("parallel", "parallel")), name=name,
    )(*args)
    return out[0], (out[1] if ctx_out else None)


S5_LANE_BLOCK = 1024
S5_TIME_BLOCK = 32


def _s5_prep_kernel(lr_ref, li_ref, ls_ref, br_ref, bi_ref, ar_ref, ai_ref, bbr_ref, bbi_ref):
    lam_r, lam_i = lr_ref[0], li_ref[0]
    dt = jnp.exp(ls_ref[0])
    mag = jnp.exp(lam_r * dt)
    a_r = mag * jnp.cos(lam_i * dt)
    a_i = mag * jnp.sin(lam_i * dt)
    den = lam_r * lam_r + lam_i * lam_i
    z_r = ((a_r - 1.0) * lam_r + a_i * lam_i) / den
    z_i = (a_i * lam_r - (a_r - 1.0) * lam_i) / den
    ar_ref[0] = jnp.broadcast_to(a_r, ar_ref.shape[1:])
    ai_ref[0] = jnp.broadcast_to(a_i, ai_ref.shape[1:])
    nblk, _, cols = br_ref.shape[1:]
    for j in range(nblk):
        zr = z_r[:, j * cols:(j + 1) * cols]
        zi = z_i[:, j * cols:(j + 1) * cols]
        bbr_ref[0, j] = (zr * br_ref[0, j] - zi * bi_ref[0, j]).astype(BF16)
        bbi_ref[0, j] = (zr * bi_ref[0, j] + zi * br_ref[0, j]).astype(BF16)


def _block_diag(w, n_blk):
    *lead, g, r, c = w.shape
    gl = g // n_blk
    w = w.reshape(*lead, n_blk, gl, r, c)
    eye = jnp.eye(gl, dtype=w.dtype)
    out = w[..., :, :, None, :] * eye[:, None, :, None]
    return out.reshape(*lead, n_blk, gl * r, gl * c)


def _s5_prep(lam_re, lam_im, log_step, b_re, b_im, batch):
    _, g, p = lam_re.shape
    n_blk = (g * p) // S5_LANE_BLOCK
    flat = lambda a: a.reshape(2, 1, g * p)
    ls = jnp.broadcast_to(log_step[:, :, None], (2, g, p))
    brd = _block_diag(jnp.swapaxes(b_re, -1, -2), n_blk)
    bid = _block_diag(jnp.swapaxes(b_im, -1, -2), n_blk)
    vec = pl.BlockSpec((1, 1, g * p), lambda d: (d, 0, 0))
    mat = pl.BlockSpec((1,) + brd.shape[1:], lambda d: (d, 0, 0, 0))
    tab = pl.BlockSpec((1, batch, g * p), lambda d: (d, 0, 0))
    return pl.pallas_call(
        _s5_prep_kernel,
        out_shape=[jax.ShapeDtypeStruct((2, batch, g * p), F32)] * 2 + [jax.ShapeDtypeStruct(brd.shape, BF16)] * 2,
        grid=(2,), in_specs=[vec, vec, vec, mat, mat], out_specs=[tab, tab, mat, mat],
        compiler_params=_cparams(("parallel",)), name="s5_prep",
    )(flat(lam_re), flat(lam_im), flat(ls), brd, bid)


def _s5_kernel(uf_ref, ub_ref, ar_ref, ai_ref, bbr_ref, bbi_ref, cr_ref, ci_ref, yf_ref, yb_ref,
               xr_scr, xi_scr, sr_scr, si_scr, *, batch, tb):
    n_blk, kin, _ = bbr_ref.shape[1:]
    kout = cr_ref.shape[3]

    @pl.when(pl.program_id(0) == 0)
    def _():
        sr_scr[...] = jnp.zeros_like(sr_scr)
        si_scr[...] = jnp.zeros_like(si_scr)

    dirs = tuple(enumerate(((uf_ref, yf_ref), (ub_ref, yb_ref))))
    blocks = [slice(j * S5_LANE_BLOCK, (j + 1) * S5_LANE_BLOCK) for j in range(n_blk)]
    for d, (u_ref, _) in dirs:
        u = u_ref[...].astype(BF16)
        for j, lanes in enumerate(blocks):
            uj = u[:, j * kin:(j + 1) * kin]
            xr_scr[d, :, lanes] = jnp.dot(uj, bbr_ref[d, j], preferred_element_type=F32)
            xi_scr[d, :, lanes] = jnp.dot(uj, bbi_ref[d, j], preferred_element_type=F32)
    for d, _ in dirs:
        for lanes in blocks:
            a_r, a_i = ar_ref[d, :, lanes], ai_ref[d, :, lanes]
            s_r, s_i = sr_scr[d, :, lanes], si_scr[d, :, lanes]
            for s in (range(tb) if d == 0 else range(tb - 1, -1, -1)):
                rows = slice(s * batch, (s + 1) * batch)
                s_r, s_i = (a_r * s_r - a_i * s_i + xr_scr[d, rows, lanes],
                            a_r * s_i + a_i * s_r + xi_scr[d, rows, lanes])
                xr_scr[d, rows, lanes] = s_r
                xi_scr[d, rows, lanes] = s_i
            sr_scr[d, :, lanes] = s_r
            si_scr[d, :, lanes] = s_i
    for d, (_, y_ref) in dirs:
        for j, lanes in enumerate(blocks):
            y = (jnp.dot(xr_scr[d, :, lanes].astype(BF16), cr_ref[d, j], preferred_element_type=F32)
                 - jnp.dot(xi_scr[d, :, lanes].astype(BF16), ci_ref[d, j], preferred_element_type=F32))
            y_ref[:, j * kout:(j + 1) * kout] = y.astype(y_ref.dtype)


def _s5_scan(u_tm, tabs, c_re_bd, c_im_bd, *, batch, t, tc):
    a_r, a_i, bbr, bbi = tabs
    rows, width = u_tm.shape
    tb = _tile(math.gcd(t, tc), (S5_TIME_BLOCK, 16, 8, 4, 2, 1))
    nbc, nbl = tc // tb, t // tb
    nb = nbc + nbl
    blk = tb * batch

    def bwd(i):
        return jnp.where(i < nbc, nbc - 1 - i, nb - 1 - (i - nbc))

    full4 = lambda a: pl.BlockSpec(a.shape, lambda i: (0, 0, 0, 0), pipeline_mode=pl.Buffered(1))
    full3 = lambda a: pl.BlockSpec(a.shape, lambda i: (0, 0, 0), pipeline_mode=pl.Buffered(1))
    n_state = a_r.shape[-1]
    return pl.pallas_call(
        functools.partial(_s5_kernel, batch=batch, tb=tb),
        out_shape=[jax.ShapeDtypeStruct((rows, width), BF16)] * 2,
        grid=(nb,),
        in_specs=[pl.BlockSpec((blk, width), lambda i: (i, 0)), pl.BlockSpec((blk, width), lambda i: (bwd(i), 0)),
                  full3(a_r), full3(a_i), full4(bbr), full4(bbi), full4(c_re_bd), full4(c_im_bd)],
        out_specs=[pl.BlockSpec((blk, width), lambda i: (i, 0)), pl.BlockSpec((blk, width), lambda i: (bwd(i), 0))],
        scratch_shapes=[pltpu.VMEM((2, blk, n_state), F32)] * 2 + [pltpu.VMEM((2, batch, n_state), F32)] * 2,
        compiler_params=_cparams(("arbitrary",)), name="s5_scan",
    )(u_tm, u_tm, a_r, a_i, bbr, bbi, c_re_bd, c_im_bd)


def _glu_kernel(u_ref, yf_ref, yb_ref, d_ref, w_ref, b_ref, o_ref):
    y = u_ref[...].astype(F32) * d_ref[...] + yf_ref[...].astype(F32) + yb_ref[...].astype(F32)
    z = jax.nn.gelu(y, approximate=True)
    gate = jnp.dot(z.astype(BF16), w_ref[...], preferred_element_type=F32) + b_ref[...]
    o_ref[...] = (z * jax.nn.sigmoid(gate)).astype(o_ref.dtype)


def _s5_glu(u_tm, yf, yb, d, glu_w, glu_b):
    rows, width = u_tm.shape
    tm = _tile(rows, (512, 256, 128, 64, 32, 16, 8))
    row = pl.BlockSpec((tm, width), lambda i: (i, 0))
    vec = pl.BlockSpec((1, width), lambda i: (0, 0))
    return pl.pallas_call(
        _glu_kernel, out_shape=jax.ShapeDtypeStruct((rows, width), BF16), grid=(rows // tm,),
        in_specs=[row, row, row, vec, pl.BlockSpec((width, width), lambda i: (0, 0)), vec], out_specs=row,
        compiler_params=_cparams(("parallel",)), name="s5_glu",
    )(u_tm, yf, yb, d.reshape(1, width), glu_w, glu_b.reshape(1, width))


def _merge_kernel(y0, y1, y2, y3, g0, g1, g2, g3, wb_ref, o_ref):
    acc = None
    for i, (y_ref, g_ref) in enumerate(((y0, g0), (y1, g1), (y2, g2), (y3, g3))):
        term = jax.nn.sigmoid(g_ref[...].astype(F32)) * jnp.dot(y_ref[...], wb_ref[i], preferred_element_type=F32)
        acc = term if acc is None else acc + term
    o_ref[...] = acc.astype(o_ref.dtype)


def _merge(ys, z, lay, w_branch, *, name):
    m, mw = ys[0].shape
    d = w_branch.shape[-1]
    tm = _tile(m, (1024, 512, 256, 128, 64, 32, 16))
    tn = _tile(d, (512, 256, 128))
    g_off = lay.off["gates"]
    assert g_off % tn == 0

    def gate_spec(i):
        return pl.BlockSpec((tm, tn), lambda r, c: (r, (g_off + i * d) // tn + c))

    yspec = pl.BlockSpec((tm, mw), lambda r, c: (r, 0))
    return pl.pallas_call(
        _merge_kernel, out_shape=jax.ShapeDtypeStruct((m, d), BF16), grid=(m // tm, d // tn),
        in_specs=[yspec] * N_BRANCH + [gate_spec(i) for i in range(N_BRANCH)]
        + [pl.BlockSpec((N_BRANCH, mw, tn), lambda r, c: (0, 0, c))],
        out_specs=pl.BlockSpec((tm, tn), lambda r, c: (r, c)),
        compiler_params=_cparams(("parallel", "arbitrary")), name=name,
    )(*ys, z, z, z, z, w_branch)


def _ffn_up_kernel(h_ref, w1_ref, w3_ref, o_ref):
    h = h_ref[...]
    a = jnp.dot(h, w1_ref[...], preferred_element_type=F32)
    b = jnp.dot(h, w3_ref[...], preferred_element_type=F32)
    o_ref[...] = (a * jax.nn.sigmoid(a) * b).astype(o_ref.dtype)


def _ffn_up(h, w1, w3, *, name):
    m, k = h.shape
    f = w1.shape[-1]
    tm, tn = _mm_tiles(m, k, f, 2, 2, 2, n_w=2)
    w_spec = pl.BlockSpec((k, tn), lambda i, j: (0, j))
    return pl.pallas_call(
        _ffn_up_kernel, out_shape=jax.ShapeDtypeStruct((m, f), BF16), grid=(m // tm, f // tn),
        in_specs=[pl.BlockSpec((tm, k), lambda i, j: (i, 0)), w_spec, w_spec],
        out_specs=pl.BlockSpec((tm, tn), lambda i, j: (i, j)),
        compiler_params=_cparams(("parallel", "arbitrary")), name=name,
    )(h, w1, w3)


MOE_ROW_TILE = 512
MOE_GATHER_TILE = 256
MOE_COMBINE_TILE = 128
DMA_UNROLL = 8
DMA_PRIORITIES = 2


def _router_kernel(h_ref, rh_ref, rl_ref, o_ref, *, n_experts):
    h = h_ref[...]
    hh = h.astype(BF16)
    hl = (h - hh.astype(F32)).astype(BF16)
    logits = (jnp.dot(hh, rh_ref[...], preferred_element_type=F32) + jnp.dot(hh, rl_ref[...], preferred_element_type=F32)
              + jnp.dot(hl, rh_ref[...], preferred_element_type=F32))
    lane = lax.broadcasted_iota(jnp.int32, logits.shape, 1).astype(F32)
    neg = jnp.float32(-jnp.inf)
    x1 = jnp.where(lane < n_experts, logits, neg)
    m1 = jnp.max(x1, axis=-1, keepdims=True)
    i1 = jnp.min(jnp.where(x1 == m1, lane, float(LANES)), axis=-1, keepdims=True)
    x2 = jnp.where(lane == i1, neg, x1)
    m2 = jnp.max(x2, axis=-1, keepdims=True)
    i2 = jnp.min(jnp.where(x2 == m2, lane, float(LANES)), axis=-1, keepdims=True)
    e2 = jnp.exp(m2 - m1)
    w1 = 1.0 / (1.0 + e2)
    o_ref[...] = (jnp.where(lane == 0.0, w1, 0.0) + jnp.where(lane == 1.0, e2 * w1, 0.0)
                  + jnp.where(lane == 2.0, i1, 0.0) + jnp.where(lane == 3.0, i2, 0.0))


def _router(h, router):
    m, d = h.shape
    n_experts = router.shape[-1]
    rp = jnp.zeros((d, LANES), F32).at[:, :n_experts].set(router)
    rh = rp.astype(BF16)
    rl = (rp - rh.astype(F32)).astype(BF16)
    tm = _tile(m, (256, 128, 64, 32, 16, 8))
    mat = pl.BlockSpec((d, LANES), lambda i: (0, 0))
    return pl.pallas_call(
        functools.partial(_router_kernel, n_experts=n_experts),
        out_shape=jax.ShapeDtypeStruct((m, LANES), F32), grid=(m // tm,),
        in_specs=[pl.BlockSpec((tm, d), lambda i: (i, 0)), mat, mat],
        out_specs=pl.BlockSpec((tm, LANES), lambda i: (i, 0)),
        compiler_params=_cparams(("parallel",)), name="moe_router",
    )(h, rh, rl)


def _moe_plan(expert_idx, n_experts, tile):
    m = expert_idx.shape[0]
    flat = expert_idx.reshape(-1)
    onehot = (flat[:, None] == jnp.arange(n_experts, dtype=jnp.int32)[None, :]).astype(jnp.int32)
    rank = jnp.take_along_axis(jnp.cumsum(onehot, axis=0) - onehot, flat[:, None], axis=1)[:, 0]
    count = jnp.sum(onehot, axis=0)
    group = ((count + tile - 1) // tile) * tile
    group_end = jnp.cumsum(group)
    dest = (group_end - group)[flat] + rank
    rows = m * TOP_K + n_experts * tile
    row_src = jnp.zeros((rows,), jnp.int32).at[dest].set(jnp.arange(m * TOP_K, dtype=jnp.int32) // TOP_K)
    n_tiles = rows // tile
    n_used = group_end[-1] // tile
    tile_id = jnp.arange(n_tiles, dtype=jnp.int32)
    owner = jnp.searchsorted(group_end, jnp.minimum(tile_id, n_used - 1) * tile, side="right").astype(jnp.int32)
    return dest.reshape(m, TOP_K), row_src, jnp.minimum(owner, n_experts - 1), n_used.reshape(1).astype(jnp.int32)


def _row_copy(src_hbm, src_row, buf, slot, dst_row, sem):
    return pltpu.make_async_copy(src_hbm.at[pl.ds(src_row, 1), :], buf.at[slot, pl.ds(dst_row, 1), :], sem.at[slot])


def _gather_rows(idx_ref, src_hbm, buf, sem, slot, n):
    def body(p, carry):
        for prio in range(DMA_PRIORITIES):
            r = DMA_PRIORITIES * p + prio
            _row_copy(src_hbm, idx_ref[0, r], buf, slot, r, sem).start(priority=prio)
        return carry
    lax.fori_loop(0, n // DMA_PRIORITIES, body, 0, unroll=DMA_UNROLL // DMA_PRIORITIES)


def _wait_rows(src_hbm, buf, sem, slot, n):
    def body(r, carry):
        _row_copy(src_hbm, 0, buf, slot, r, sem).wait()
        return carry
    lax.fori_loop(0, n, body, 0, unroll=DMA_UNROLL)


def _gather_kernel(cur_ref, nxt_ref, src_hbm, o_ref, buf, sem):
    s, n_steps = pl.program_id(0), pl.num_programs(0)
    n = buf.shape[1]

    @pl.when(s == 0)
    def _():
        _gather_rows(cur_ref, src_hbm, buf, sem, 0, n)

    @pl.when(s + 1 < n_steps)
    def _():
        _gather_rows(nxt_ref, src_hbm, buf, sem, (s + 1) % 2, n)

    slot = s % 2
    _wait_rows(src_hbm, buf, sem, slot, n)
    o_ref[...] = buf[slot].astype(o_ref.dtype)


def _moe_gather(h, row_src):
    d = h.shape[1]
    rows = row_src.shape[0]
    tg = _tile(rows, (MOE_GATHER_TILE, 128, 64, 32, 16, 8))
    steps = rows // tg
    idx = row_src.reshape(steps, 1, tg)
    smem = lambda f: pl.BlockSpec((None, 1, tg), f, memory_space=pltpu.SMEM)
    return pl.pallas_call(
        _gather_kernel, out_shape=jax.ShapeDtypeStruct((rows, d), BF16), grid=(steps,),
        in_specs=[smem(lambda s: (s, 0, 0)), smem(lambda s: (jnp.minimum(s + 1, steps - 1), 0, 0)),
                  pl.BlockSpec(memory_space=pl.ANY)],
        out_specs=pl.BlockSpec((tg, d), lambda s: (s, 0)),
        scratch_shapes=[pltpu.VMEM((2, tg, d), F32), pltpu.SemaphoreType.DMA((2,))],
        compiler_params=_cparams(("arbitrary",)), name="moe_gather",
    )(idx, idx, h)


def _moe_grouped_kernel(te_ref, nu_ref, x_ref, *refs, n_w):
    w_refs, o_ref, w_scrs = refs[:n_w], refs[n_w], refs[n_w + 1:]
    i = pl.program_id(1)

    @pl.when(jnp.logical_or(i == 0, te_ref[i] != te_ref[jnp.maximum(i - 1, 0)]))
    def _():
        for w_ref, w_scr in zip(w_refs, w_scrs):
            w_scr[...] = w_ref[...].astype(BF16)

    @pl.when(i < nu_ref[0])
    def _():
        if n_w == 2:
            _ffn_up_kernel(x_ref, *w_scrs, o_ref)
        else:
            o_ref[...] = jnp.dot(x_ref[...], w_scrs[0][...], preferred_element_type=F32).astype(o_ref.dtype)

    @pl.when(i >= nu_ref[0])
    def _():
        o_ref[...] = jnp.zeros_like(o_ref)


def _moe_grouped(a, weights, tile_expert, n_used, *, tn, out_dtype, name):
    rows, k = a.shape
    n = weights[0].shape[-1]
    tm = MOE_ROW_TILE
    used = lambda i, nu: jnp.minimum(i, nu[0] - 1)
    w_spec = pl.BlockSpec((None, k, tn), lambda j, i, te, nu: (te[i], 0, j))
    return pl.pallas_call(
        functools.partial(_moe_grouped_kernel, n_w=len(weights)),
        out_shape=jax.ShapeDtypeStruct((rows, n), out_dtype),
        grid_spec=pltpu.PrefetchScalarGridSpec(
            num_scalar_prefetch=2, grid=(n // tn, rows // tm),
            in_specs=[pl.BlockSpec((tm, k), lambda j, i, te, nu: (used(i, nu), 0))] + [w_spec] * len(weights),
            out_specs=pl.BlockSpec((tm, tn), lambda j, i, te, nu: (i, j)),
            scratch_shapes=[pltpu.VMEM((k, tn), BF16)] * len(weights)),
        compiler_params=_cparams(("arbitrary", "arbitrary")), name=name,
    )(tile_expert, n_used, a, *weights)


def _moe_combine_kernel(*refs, alpha, emit_h):
    it = iter(refs)
    cur_ref, nxt_ref, x_ref, wts_ref, gate_ref, g_ref, b_ref = (next(it) for _ in range(7))
    if emit_h:
        sc_ref, sh_ref = next(it), next(it)
    y_hbm, xo_ref = next(it), next(it)
    ho_ref = next(it) if emit_h else None
    buf, sem = next(it), next(it)
    s, n_steps = pl.program_id(0), pl.num_programs(0)
    n = buf.shape[1]
    tm = n // TOP_K

    @pl.when(s == 0)
    def _():
        _gather_rows(cur_ref, y_hbm, buf, sem, 0, n)

    @pl.when(s + 1 < n_steps)
    def _():
        _gather_rows(nxt_ref, y_hbm, buf, sem, (s + 1) % 2, n)

    slot = s % 2
    _wait_rows(y_hbm, buf, sem, slot, n)
    f = wts_ref[:, 0:1] * buf[slot, 0:tm, :] + wts_ref[:, 1:2] * buf[slot, tm:2 * tm, :]
    xn = _layer_norm(alpha * x_ref[...] + gate_ref[0] * f, g_ref[...], b_ref[...])
    xo_ref[...] = xn
    if emit_h:
        ho_ref[...] = (xn * (1.0 + sc_ref[0]) + sh_ref[0]).astype(ho_ref.dtype)


def _moe_combine_ln(x, y_sorted, dest, wts, mods, ln_g, ln_b, *, alpha, rows_per_group, g0, gate_chunk,
                    next_mods, scale_chunk, shift_chunk, name):
    m, d = x.shape
    tm = _tile(rows_per_group or m, (MOE_COMBINE_TILE, 64, 32, 16, 8))
    steps = m // tm
    idx = jnp.swapaxes(dest.reshape(steps, tm, TOP_K), 1, 2).reshape(steps, 1, TOP_K * tm)
    emit_h = next_mods is not None
    row = pl.BlockSpec((tm, d), lambda s: (s, 0))
    vec = pl.BlockSpec((1, d), lambda s: (0, 0))
    smem = lambda f: pl.BlockSpec((None, 1, TOP_K * tm), f, memory_space=pltpu.SMEM)
    in_specs = [smem(lambda s: (s, 0, 0)), smem(lambda s: (jnp.minimum(s + 1, steps - 1), 0, 0)), row,
                pl.BlockSpec((tm, LANES), lambda s: (s, 0)),
                pl.BlockSpec((1, 1, d), _group_map(rows_per_group, tm, g0, gate_chunk)), vec, vec]
    args = [idx, idx, x, wts, mods, ln_g.reshape(1, d), ln_b.reshape(1, d)]
    out_shape = [jax.ShapeDtypeStruct((m, d), F32)]
    out_specs = [row]
    if emit_h:
        in_specs += [pl.BlockSpec((1, 1, d), _group_map(rows_per_group, tm, g0, scale_chunk)),
                     pl.BlockSpec((1, 1, d), _group_map(rows_per_group, tm, g0, shift_chunk))]
        args += [next_mods, next_mods]
        out_shape.append(jax.ShapeDtypeStruct((m, d), BF16))
        out_specs.append(row)
    in_specs.append(pl.BlockSpec(memory_space=pl.ANY))
    args.append(y_sorted)
    out = pl.pallas_call(
        functools.partial(_moe_combine_kernel, alpha=alpha, emit_h=emit_h),
        out_shape=out_shape, grid=(steps,), in_specs=in_specs, out_specs=out_specs,
        scratch_shapes=[pltpu.VMEM((2, TOP_K * tm, d), F32), pltpu.SemaphoreType.DMA((2,))],
        compiler_params=_cparams(("arbitrary",)), name=name,
    )(*args)
    return out if emit_h else (out[0], None)


def _moe_ln(x, h, mods, router, w1, w3, w2, ln_g, ln_b, *, alpha, rows_per_group, g0, next_mods, name):
    n_experts = router.shape[-1]
    route = _router(h, router)
    expert_idx = route[:, TOP_K:2 * TOP_K].astype(jnp.int32)
    dest, row_src, tile_expert, n_used = _moe_plan(expert_idx, n_experts, MOE_ROW_TILE)
    xs = _moe_gather(h, row_src)
    up = _moe_grouped(xs, (w1, w3), tile_expert, n_used, tn=_tile(w1.shape[-1], (512, 256, 128)), out_dtype=BF16,
                      name=name + "_up")
    ys = _moe_grouped(up, (w2,), tile_expert, n_used, tn=_tile(w2.shape[-1], (1024, 512, 256, 128)), out_dtype=F32,
                      name=name + "_dn")
    return _moe_combine_ln(x, ys, dest, route, mods, ln_g, ln_b, alpha=alpha, rows_per_group=rows_per_group, g0=g0,
                           gate_chunk=5, next_mods=next_mods, scale_chunk=1, shift_chunk=0, name=name + "_ln")


def kernel(x, c, ctx, c_ctx, w_ada, b_ada, w_in, b_in, attn_q_norm, attn_k_norm, gla_w2, gla_b2, gla_norm,
           s5_lambda_re, s5_lambda_im, s5_log_step, s5_b_re, s5_b_im, s5_c_re, s5_c_im, s5_d, s5_glu_w, s5_glu_b,
           w_branch, w_out, ln1_g, ln1_b, ln2_g, ln2_b, ffn_w1, ffn_w3, ffn_w2, moe_router, moe_w1, moe_w3, moe_w2):
    batch, t, d = x.shape
    tc = ctx.shape[1]
    depth = w_ada.shape[0]
    assert batch + 1 <= MOD_ROWS and t % CHUNK == 0 and tc % CHUNK == 0 and t % GRID_W == 0
    lay = _Layout(d)
    assert lay.in_width == w_in.shape[-1]
    alpha = (2 * depth) ** 0.25
    rope = _rope_tables(t)
    log_gamma = jnp.log1p(-jnp.exp2(-5.0 - jnp.arange(REC_HEADS, dtype=F32)))
    lg_tab = jnp.broadcast_to(log_gamma[:, None, None], (REC_HEADS, 1, HEAD_DIM))
    n_blk = (s5_lambda_re.shape[2] * S5_STATE) // S5_LANE_BLOCK

    xl = x.reshape(batch * t, d)
    xc = ctx.reshape(batch * tc, d)
    cvec = jnp.concatenate([c, c_ctx[None], jnp.zeros((MOD_ROWS - batch - 1, d), F32)], axis=0)

    def mods_of(l):
        return _matmul(cvec, w_ada, b_ada[l], a_silu=True, w_index=l, name=f"ada{l}").reshape(MOD_ROWS, 1, 6 * d)

    mods = mods_of(0)
    hl = _modulate(xl, mods, rows_per_group=t, g0=0, scale_chunk=1, shift_chunk=0, name="mod_lat")
    hc = _modulate(xc, mods, rows_per_group=0, g0=batch, scale_chunk=1, shift_chunk=0, name="mod_ctx")

    for l in range(depth):
        ctx_out = l < depth - 1
        w_in_p = _permute_w_in(w_in, l, lay)
        b_in_p = lay.permute(b_in[l])
        zl = _matmul(hl, w_in_p, b_in_p, out_dtype=BF16, w_rows=True, name=f"in_lat{l}")
        zc = _matmul(hc, w_in_p, b_in_p, n_out=None if ctx_out else lay.small, out_dtype=BF16, w_rows=True,
                     name=f"in_ctx{l}")

        y_att = _attention(zl, zc, zl, lay, attn_q_norm[l], attn_k_norm[l], rope, batch=batch, tq_rows=t, tc=tc, t=t,
                           name=f"attn_lat{l}")
        yc_att = (_attention(zc, zc, None, lay, attn_q_norm[l], attn_k_norm[l], None, batch=batch, tq_rows=tc, tc=tc,
                             t=t, name=f"attn_ctx{l}") if ctx_out else None)

        w2h = gla_w2[l].reshape(2, GLA_RANK, REC_HEADS, HEAD_DIM)
        w2cat = jnp.zeros((REC_HEADS, LANES, 2 * HEAD_DIM), F32)
        for r in range(2):
            w2cat = w2cat.at[:, r * GLA_RANK:(r + 1) * GLA_RANK, r * HEAD_DIM:(r + 1) * HEAD_DIM].set(
                jnp.transpose(w2h[r], (1, 0, 2)))
        b2cat = jnp.transpose(gla_b2[l].reshape(2, REC_HEADS, HEAD_DIM), (1, 0, 2)).reshape(REC_HEADS, 1, 2 * HEAD_DIM)
        y_gla, yc_gla = _recurrence(zl, zc, lay, "gla", (w2cat.astype(BF16), b2cat, gla_norm[l].reshape(1, -1)),
                                    batch=batch, t=t, tc=tc, ctx_out=ctx_out, name=f"gla{l}")
        y_ret, yc_ret = _recurrence(zl, zc, lay, "ret", (lg_tab, rope), batch=batch, t=t, tc=tc, ctx_out=ctx_out,
                                    name=f"ret{l}")

        mw = lay.src["s5_u"][1]
        u0 = lay.off["s5_u"]
        u_tm = jnp.concatenate([jnp.swapaxes(zc[:, u0:u0 + mw].reshape(batch, tc, mw), 0, 1),
                                jnp.swapaxes(zl[:, u0:u0 + mw].reshape(batch, t, mw), 0, 1)], axis=0)
        u_tm = u_tm.reshape((tc + t) * batch, mw)
        tabs = _s5_prep(s5_lambda_re[l], s5_lambda_im[l], s5_log_step[l], s5_b_re[l], s5_b_im[l], batch)
        c_re_bd = _block_diag(jnp.swapaxes(s5_c_re[l], -1, -2), n_blk).astype(BF16)
        c_im_bd = _block_diag(jnp.swapaxes(s5_c_im[l], -1, -2), n_blk).astype(BF16)
        yf, yb = _s5_scan(u_tm, tabs, c_re_bd, c_im_bd, batch=batch, t=t, tc=tc)
        y_s5_tm = _s5_glu(u_tm, yf, yb, s5_d[l], s5_glu_w[l].astype(BF16), s5_glu_b[l])
        y_s5_bm = jnp.swapaxes(y_s5_tm.reshape(tc + t, batch, mw), 0, 1)
        y_s5 = y_s5_bm[:, tc:].reshape(batch * t, mw)
        yc_s5 = y_s5_bm[:, :tc].reshape(batch * tc, mw)

        wb = w_branch[l].astype(BF16)
        wo = w_out[l].astype(BF16)
        is_moe = l % 2 == 1
        h_dtype = F32 if is_moe else BF16
        ml = _matmul(_merge((y_gla, y_att, y_ret, y_s5), zl, lay, wb, name=f"merge_lat{l}"), wo, out_dtype=BF16,
                     name=f"out_lat{l}")
        xl, hl = _ln_res(xl, ml, mods, ln1_g[l], ln1_b[l], alpha=alpha, rows_per_group=t, g0=0, gate_chunk=2,
                         next_mods=mods, scale_chunk=4, shift_chunk=3, h_dtype=h_dtype, name=f"ln1_lat{l}")
        if ctx_out:
            mc = _matmul(_merge((yc_gla, yc_att, yc_ret, yc_s5), zc, lay, wb, name=f"merge_ctx{l}"), wo,
                         out_dtype=BF16, name=f"out_ctx{l}")
            xc, hc = _ln_res(xc, mc, mods, ln1_g[l], ln1_b[l], alpha=alpha, rows_per_group=0, g0=batch, gate_chunk=2,
                             next_mods=mods, scale_chunk=4, shift_chunk=3, h_dtype=h_dtype, name=f"ln1_ctx{l}")

        i = l // 2
        next_mods = mods_of(l + 1) if l + 1 < depth else None
        if is_moe:
            w1, w3, w2 = moe_w1[i], moe_w3[i], moe_w2[i]
            xl, hl = _moe_ln(xl, hl, mods, moe_router[i], w1, w3, w2, ln2_g[l], ln2_b[l], alpha=alpha,
                             rows_per_group=t, g0=0, next_mods=next_mods, name=f"moe_lat{l}")
            if ctx_out:
                xc, hc = _moe_ln(xc, hc, mods, moe_router[i], w1, w3, w2, ln2_g[l], ln2_b[l], alpha=alpha,
                                 rows_per_group=0, g0=batch, next_mods=next_mods, name=f"moe_ctx{l}")
        else:
            w1, w3, w2 = ffn_w1[i].astype(BF16), ffn_w3[i].astype(BF16), ffn_w2[i].astype(BF16)
            fl = _matmul(_ffn_up(hl, w1, w3, name=f"ffn_up_lat{l}"), w2, out_dtype=BF16, name=f"ffn_dn_lat{l}")
            xl, hl = _ln_res(xl, fl, mods, ln2_g[l], ln2_b[l], alpha=alpha, rows_per_group=t, g0=0, gate_chunk=5,
                             next_mods=next_mods, scale_chunk=1, shift_chunk=0, name=f"ln2_lat{l}")
            if ctx_out:
                fc = _matmul(_ffn_up(hc, w1, w3, name=f"ffn_up_ctx{l}"), w2, out_dtype=BF16, name=f"ffn_dn_ctx{l}")
                xc, hc = _ln_res(xc, fc, mods, ln2_g[l], ln2_b[l], alpha=alpha, rows_per_group=0, g0=batch,
                                 gate_chunk=5, next_mods=next_mods, scale_chunk=1, shift_chunk=0, name=f"ln2_ctx{l}")
        mods = next_mods

    return xl.reshape(batch, t, d)
```

```python
import functools
import math

import jax
import jax.numpy as jnp
from jax import lax
from jax.experimental import pallas as pl
from jax.experimental.pallas import tpu as pltpu

F32 = jnp.float32
BF16 = jnp.bfloat16

V7X_VMEM_BYTES = 64 * 1024 * 1024
VMEM_LIMIT = V7X_VMEM_BYTES - 8 * 1024 * 1024
TILE_BUDGET = 52 * 1024 * 1024
LANES = 128

HEAD_DIM = 128
ATT_KV_HEADS = 2
REC_HEADS = 4
GLA_RANK = 16
GLA_GATE_NORM = 16.0
CHUNK = 64
REC_BLOCK = 256
S5_STATE = 64
N_BRANCH = 4
TOP_K = 2
ROPE_THETA = 10000.0
GRID_W = 64
EPS = 1e-6
MOD_ROWS = 16


def _cparams(sem):
    return pltpu.CompilerParams(dimension_semantics=sem, vmem_limit_bytes=VMEM_LIMIT)


def _tile(n, prefs):
    for t in prefs:
        if n % t == 0:
            return t
    return n


def _mm_tiles(m, k, n, a_bytes, w_bytes, o_bytes, n_w=1):
    wide = tuple((tm, tn) for tm in (64, 32, 16, 8) for tn in (1024, 512, 256, 128))
    for tm, tn in ((1024, 1024), (1024, 512), (512, 1024), (512, 512), (256, 512), (256, 256), (128, 256),
                   (128, 128)) + wide:
        if m % tm or n % tn:
            continue
        need = 2 * (tm * k * a_bytes + n_w * k * tn * w_bytes + tm * tn * o_bytes) + n_w * tm * tn * 4
        if need <= TILE_BUDGET:
            return tm, tn
    raise ValueError(f"no matmul tiling for {(m, k, n)}")


class _Layout:
    def __init__(self, d):
        mw = d // 4
        hk = REC_HEADS * HEAD_DIM
        kv = ATT_KV_HEADS * HEAD_DIM
        src_sizes = (mw, kv, kv, hk, hk, mw, mw, 2 * GLA_RANK, hk, hk, mw, mw, mw, N_BRANCH * d)
        names = ("att_q", "att_k", "att_v", "gla_q", "gla_k", "gla_v", "gla_g", "gla_lr",
                 "ret_q", "ret_k", "ret_v", "ret_g", "s5_u", "gates")
        self.src = {}
        off = 0
        for nm, sz in zip(names, src_sizes):
            self.src[nm] = (off, sz)
            off += sz
        self.in_width = off
        order = tuple(nm for nm in names if nm not in ("gla_lr", "gates")) + ("gla_lr",)
        self.off = {}
        self.plan = []
        pos = 0
        for nm in order:
            width = max(self.src[nm][1], LANES)
            self.off[nm] = pos
            self.plan.append((nm, self.src[nm][1]))
            if width > self.src[nm][1]:
                self.plan.append((None, width - self.src[nm][1]))
            pos += width
        small = -(-pos // 1024) * 1024
        if small > pos:
            self.plan.append((None, small - pos))
        self.small = small
        self.off["gates"] = small
        self.plan.append(("gates", self.src["gates"][1]))
        self.width = small + self.src["gates"][1]

    def blk(self, name, width):
        off = self.off[name]
        assert off % width == 0, (name, off, width)
        return off // width

    def permute(self, w):
        parts = []
        for nm, sz in self.plan:
            if nm is None:
                parts.append(jnp.zeros(w.shape[:-1] + (sz,), w.dtype))
            else:
                o = self.src[nm][0]
                parts.append(w[..., o:o + sz])
        return jnp.concatenate(parts, axis=-1)


W_IN_ROW_TILE = 512


def _permute_w_in_kernel(wt_hbm, o_ref, buf, sem, *, layer, j_plain, j_lr, lr_off, lr_w):
    s, n_steps = pl.program_id(0), pl.num_programs(0)
    tn = o_ref.shape[0]

    def tile_copy(j, slot):
        shifted = jnp.where(j < j_lr, j, j - 1) * tn + lr_w
        r0 = jnp.where(j < j_plain, j * tn, jnp.where(j == j_lr, lr_off, shifted))
        return pltpu.make_async_copy(wt_hbm.at[layer, pl.ds(pl.multiple_of(r0, 8), tn), :], buf.at[slot], sem.at[slot])

    @pl.when(s == 0)
    def _():
        tile_copy(s, 0).start()

    @pl.when(s + 1 < n_steps)
    def _():
        tile_copy(s + 1, (s + 1) % 2).start()

    slot = s % 2
    tile_copy(s, slot).wait()
    row = lax.broadcasted_iota(jnp.int32, o_ref.shape, 0)
    keep = row < jnp.where(s == j_lr, lr_w, tn)
    o_ref[...] = jnp.where(keep, buf[slot], 0.0).astype(BF16)


def _permute_w_in(w_in, l, lay):
    wt = jnp.swapaxes(w_in, 1, 2)
    d = wt.shape[2]
    tn = W_IN_ROW_TILE
    lr_off, lr_w = lay.src["gla_lr"]
    g_off = lay.src["gates"][0]
    j_plain, j_lr = lr_off // tn, lay.off["gla_lr"] // tn
    assert lr_off % tn == 0 and lay.off["gla_lr"] % tn == 0 and lay.small - lay.off["gla_lr"] == tn
    assert lay.off["gla_lr"] - lr_off == g_off - (lr_off + lr_w) and lay.small - g_off == tn - lr_w
    assert lay.width % tn == 0 and lr_w % 8 == 0 and lr_off + tn <= lay.in_width
    return pl.pallas_call(
        functools.partial(_permute_w_in_kernel, layer=l, j_plain=j_plain, j_lr=j_lr, lr_off=lr_off, lr_w=lr_w),
        out_shape=jax.ShapeDtypeStruct((lay.width, d), BF16), grid=(lay.width // tn,),
        in_specs=[pl.BlockSpec(memory_space=pl.ANY)],
        out_specs=pl.BlockSpec((tn, d), lambda s: (s, 0)),
        scratch_shapes=[pltpu.VMEM((2, tn, d), F32), pltpu.SemaphoreType.DMA((2,))],
        compiler_params=_cparams(("arbitrary",)), name=f"w_in_perm{l}",
    )(wt)


def _mm_kernel(*refs, has_bias, a_silu, w_rows):
    if has_bias:
        a_ref, w_ref, b_ref, o_ref = refs
    else:
        a_ref, w_ref, o_ref = refs
    a = a_ref[...]
    if a_silu:
        a = a * jax.nn.sigmoid(a)
    contract = (((1,), (1 if w_rows else 0,)), ((), ()))
    acc = lax.dot_general(a.astype(BF16), w_ref[...].astype(BF16), contract, preferred_element_type=F32)
    if has_bias:
        acc = acc + b_ref[...]
    o_ref[...] = acc.astype(o_ref.dtype)


def _matmul(a, w, bias=None, *, n_out=None, out_dtype=F32, a_silu=False, w_index=None, w_rows=False, name):
    m, k = a.shape
    n = n_out or w.shape[0 if w_rows else -1]
    tm, tn = _mm_tiles(m, k, n, a.dtype.itemsize, w.dtype.itemsize, jnp.dtype(out_dtype).itemsize)
    if w_rows:
        w_spec = pl.BlockSpec((tn, k), lambda i, j: (j, 0))
    elif w_index is None:
        w_spec = pl.BlockSpec((k, tn), lambda i, j: (0, j))
    else:
        w_spec = pl.BlockSpec((None, k, tn), lambda i, j: (w_index, 0, j))
    in_specs = [pl.BlockSpec((tm, k), lambda i, j: (i, 0)), w_spec]
    args = [a, w]
    if bias is not None:
        in_specs.append(pl.BlockSpec((1, tn), lambda i, j: (0, j)))
        args.append(bias.reshape(1, -1))
    return pl.pallas_call(
        functools.partial(_mm_kernel, has_bias=bias is not None, a_silu=a_silu, w_rows=w_rows),
        out_shape=jax.ShapeDtypeStruct((m, n), out_dtype),
        grid=(m // tm, n // tn),
        in_specs=in_specs,
        out_specs=pl.BlockSpec((tm, tn), lambda i, j: (i, j)),
        compiler_params=_cparams(("parallel", "arbitrary")),
        name=name,
    )(*args)


def _group_map(rows_per_group, tm, g0, chunk):
    tiles = rows_per_group // tm if rows_per_group else None

    def index_map(i):
        return ((i // tiles + g0) if tiles else g0, 0, chunk)
    return index_map


def _modulate_kernel(x_ref, sc_ref, sh_ref, o_ref):
    o_ref[...] = (x_ref[...] * (1.0 + sc_ref[0]) + sh_ref[0]).astype(o_ref.dtype)


def _modulate(x, mods, *, rows_per_group, g0, scale_chunk, shift_chunk, name):
    m, d = x.shape
    tm = _tile(rows_per_group or m, (512, 256, 128, 64, 32, 16, 8))
    return pl.pallas_call(
        _modulate_kernel,
        out_shape=jax.ShapeDtypeStruct((m, d), BF16),
        grid=(m // tm,),
        in_specs=[pl.BlockSpec((tm, d), lambda i: (i, 0)),
                  pl.BlockSpec((1, 1, d), _group_map(rows_per_group, tm, g0, scale_chunk)),
                  pl.BlockSpec((1, 1, d), _group_map(rows_per_group, tm, g0, shift_chunk))],
        out_specs=pl.BlockSpec((tm, d), lambda i: (i, 0)),
        compiler_params=_cparams(("parallel",)),
        name=name,
    )(x, mods, mods)


def _layer_norm(y, gain, bias):
    mu = jnp.mean(y, axis=-1, keepdims=True)
    dlt = y - mu
    var = jnp.mean(dlt * dlt, axis=-1, keepdims=True)
    return dlt * lax.rsqrt(var + EPS) * gain + bias


def _top2_route(h, rh_ref, rl_ref, n_experts):
    hh = h.astype(BF16)
    hl = (h - hh.astype(F32)).astype(BF16)
    logits = (jnp.dot(hh, rh_ref[...], preferred_element_type=F32) + jnp.dot(hh, rl_ref[...], preferred_element_type=F32)
              + jnp.dot(hl, rh_ref[...], preferred_element_type=F32))
    lane = lax.broadcasted_iota(jnp.int32, logits.shape, 1).astype(F32)
    neg = jnp.float32(-jnp.inf)
    x1 = jnp.where(lane < n_experts, logits, neg)
    m1 = jnp.max(x1, axis=-1, keepdims=True)
    i1 = jnp.min(jnp.where(x1 == m1, lane, float(LANES)), axis=-1, keepdims=True)
    x2 = jnp.where(lane == i1, neg, x1)
    m2 = jnp.max(x2, axis=-1, keepdims=True)
    i2 = jnp.min(jnp.where(x2 == m2, lane, float(LANES)), axis=-1, keepdims=True)
    e2 = jnp.exp(m2 - m1)
    w1 = 1.0 / (1.0 + e2)
    return (jnp.where(lane == 0.0, w1, 0.0) + jnp.where(lane == 1.0, e2 * w1, 0.0)
            + jnp.where(lane == 2.0, i1, 0.0) + jnp.where(lane == 3.0, i2, 0.0))


def _ln_res_kernel(*refs, alpha, emit_h, n_experts):
    it = iter(refs)
    x_ref, m_ref, gate_ref, g_ref, b_ref = (next(it) for _ in range(5))
    if emit_h:
        sc_ref, sh_ref = next(it), next(it)
    if n_experts:
        rh_ref, rl_ref = next(it), next(it)
    xo_ref = next(it)
    xn = _layer_norm(alpha * x_ref[...] + gate_ref[0] * m_ref[...].astype(F32), g_ref[...], b_ref[...])
    xo_ref[...] = xn
    if emit_h:
        ho_ref = next(it)
        h = xn * (1.0 + sc_ref[0]) + sh_ref[0]
        ho_ref[...] = h.astype(ho_ref.dtype)
        if n_experts:
            next(it)[...] = _top2_route(h, rh_ref, rl_ref, n_experts)


def _ln_res(x, mres, mods, ln_g, ln_b, *, alpha, rows_per_group, g0, gate_chunk, next_mods=None,
            scale_chunk=None, shift_chunk=None, h_dtype=BF16, router=None, name):
    m, d = x.shape
    tm = _tile(rows_per_group or m, (256, 128, 64, 32, 16, 8))
    emit_h = next_mods is not None
    assert emit_h or router is None
    row = pl.BlockSpec((tm, d), lambda i: (i, 0))
    vec = pl.BlockSpec((1, d), lambda i: (0, 0))
    in_specs = [row, row, pl.BlockSpec((1, 1, d), _group_map(rows_per_group, tm, g0, gate_chunk)), vec, vec]
    args = [x, mres, mods, ln_g.reshape(1, d), ln_b.reshape(1, d)]
    out_shape = [jax.ShapeDtypeStruct((m, d), F32)]
    out_specs = [row]
    if emit_h:
        in_specs += [pl.BlockSpec((1, 1, d), _group_map(rows_per_group, tm, g0, scale_chunk)),
                     pl.BlockSpec((1, 1, d), _group_map(rows_per_group, tm, g0, shift_chunk))]
        args += [next_mods, next_mods]
        out_shape.append(jax.ShapeDtypeStruct((m, d), h_dtype))
        out_specs.append(row)
    n_experts = 0
    if router is not None:
        n_experts = router.shape[-1]
        rp = jnp.zeros((d, LANES), F32).at[:, :n_experts].set(router)
        rh = rp.astype(BF16)
        mat = pl.BlockSpec((d, LANES), lambda i: (0, 0))
        in_specs += [mat, mat]
        args += [rh, (rp - rh.astype(F32)).astype(BF16)]
        out_shape.append(jax.ShapeDtypeStruct((m, LANES), F32))
        out_specs.append(pl.BlockSpec((tm, LANES), lambda i: (i, 0)))
    out = pl.pallas_call(
        functools.partial(_ln_res_kernel, alpha=alpha, emit_h=emit_h, n_experts=n_experts),
        out_shape=out_shape, grid=(m // tm,), in_specs=in_specs, out_specs=out_specs,
        compiler_params=_cparams(("parallel",)), name=name,
    )(*args)
    return out if emit_h else (out[0], None)


def _rope_tables(t):
    nf = HEAD_DIM // 4
    rows = jnp.repeat(jnp.arange(t // GRID_W, dtype=F32), GRID_W)
    cols = (jnp.arange(t) % GRID_W).astype(F32)
    inv = ROPE_THETA ** (-jnp.arange(nf, dtype=F32) / nf)
    ar, ac = rows[:, None] * inv, cols[:, None] * inv
    cr, sr, cc, sc = jnp.cos(ar), jnp.sin(ar), jnp.cos(ac), jnp.sin(ac)
    zero = jnp.zeros_like(sr)
    cos = jnp.concatenate([cr, cr, cc, cc], axis=-1)
    sin_a = jnp.concatenate([-sr, zero, -sc, zero], axis=-1)
    sin_b = jnp.concatenate([zero, sr, zero, sc], axis=-1)
    return cos, sin_a, sin_b


def _rope(x, cos, sin_a, sin_b):
    return x * cos + pltpu.roll(x, 96, 1) * sin_a + pltpu.roll(x, 32, 1) * sin_b


def _rms(x, gain):
    return x * lax.rsqrt(jnp.mean(x * x, axis=-1, keepdims=True) + EPS) * gain


def _attn_kernel(*refs, n_rep, tc, t, has_lat):
    it = iter(refs)
    q_ref, kc_ref, vc_ref, qg_ref, kg_ref = (next(it) for _ in range(5))
    if has_lat:
        kl_ref, vl_ref, cosk, sak, sbk, cosq, saq, sbq = (next(it) for _ in range(8))
    o_ref, k_scr, v_scr = next(it), next(it), next(it)

    @pl.when(pl.program_id(2) == 0)
    def _():
        v_scr[:, HEAD_DIM:] = jnp.ones((v_scr.shape[0], HEAD_DIM), BF16)
        k_scr[0:tc, :] = _rms(kc_ref[...].astype(F32), kg_ref[...]).astype(BF16)
        v_scr[0:tc, 0:HEAD_DIM] = vc_ref[...].astype(BF16)
        if has_lat:
            kl = _rope(_rms(kl_ref[...].astype(F32), kg_ref[...]), cosk[...], sak[...], sbk[...])
            k_scr[tc:tc + t, :] = kl.astype(BF16)
            v_scr[tc:tc + t, 0:HEAD_DIM] = vl_ref[...].astype(BF16)

    scale = HEAD_DIM ** -0.5
    keys = k_scr[...]
    vals = v_scr[...]
    qs = []
    for r in range(n_rep):
        q = _rms(q_ref[:, r * HEAD_DIM:(r + 1) * HEAD_DIM].astype(F32), qg_ref[...])
        if has_lat:
            q = _rope(q, cosq[...], saq[...], sbq[...])
        qs.append((q * scale).astype(BF16))
    tq = q_ref.shape[0]
    s_all = lax.dot_general(jnp.concatenate(qs, axis=0), keys, (((1,), (1,)), ((), ())), preferred_element_type=F32)
    for r in range(n_rep):
        s = s_all[r * tq:(r + 1) * tq]
        e = jnp.exp(s - jnp.max(s, axis=-1, keepdims=True)).astype(BF16)
        ov = jnp.dot(e, vals, preferred_element_type=F32)
        o_ref[:, r * HEAD_DIM:(r + 1) * HEAD_DIM] = (ov[:, :HEAD_DIM] / ov[:, HEAD_DIM:]).astype(o_ref.dtype)


def _attention(zq, zc, zl, lay, q_norm, k_norm, rope, *, batch, tq_rows, tc, t, name):
    has_lat = zl is not None
    n_rep = (lay.src["att_q"][1] // HEAD_DIM) // ATT_KV_HEADS
    qw = n_rep * HEAD_DIM
    tq = _tile(tq_rows, (256, 128, 64, 32, 16, 8))
    nq = tq_rows // tq
    qb, kb, vb = lay.blk("att_q", qw), lay.blk("att_k", HEAD_DIM), lay.blk("att_v", HEAD_DIM)
    vec = pl.BlockSpec((1, HEAD_DIM), lambda b, g, i: (0, 0))
    in_specs = [pl.BlockSpec((tq, qw), lambda b, g, i: (b * nq + i, qb + g)),
                pl.BlockSpec((tc, HEAD_DIM), lambda b, g, i: (b, kb + g)),
                pl.BlockSpec((tc, HEAD_DIM), lambda b, g, i: (b, vb + g)), vec, vec]
    args = [zq, zc, zc, q_norm.reshape(1, -1), k_norm.reshape(1, -1)]
    n_keys = tc
    if has_lat:
        full = pl.BlockSpec((t, HEAD_DIM), lambda b, g, i: (0, 0))
        part = pl.BlockSpec((tq, HEAD_DIM), lambda b, g, i: (i, 0))
        in_specs += [pl.BlockSpec((t, HEAD_DIM), lambda b, g, i: (b, kb + g)),
                     pl.BlockSpec((t, HEAD_DIM), lambda b, g, i: (b, vb + g)), full, full, full, part, part, part]
        args += [zl, zl, *rope, *rope]
        n_keys = tc + t
    return pl.pallas_call(
        functools.partial(_attn_kernel, n_rep=n_rep, tc=tc, t=t, has_lat=has_lat),
        out_shape=jax.ShapeDtypeStruct((batch * tq_rows, lay.src["att_q"][1]), BF16),
        grid=(batch, ATT_KV_HEADS, nq),
        in_specs=in_specs,
        out_specs=pl.BlockSpec((tq, qw), lambda b, g, i: (b * nq + i, g)),
        scratch_shapes=[pltpu.VMEM((n_keys, HEAD_DIM), BF16), pltpu.VMEM((n_keys, 2 * HEAD_DIM), BF16)],
        compiler_params=_cparams(("parallel", "parallel", "arbitrary")),
        name=name,
    )(*args)


def _split3(x):
    hi = x.astype(BF16)
    r1 = x - hi.astype(F32)
    mid = r1.astype(BF16)
    lo = (r1 - mid.astype(F32)).astype(BF16)
    return hi, mid, lo


def _place(parts):
    n = len(parts)
    zero = jnp.zeros_like(parts[0])
    return jnp.concatenate([jnp.concatenate([p if j == c else zero for j in range(n)], axis=1)
                            for c, p in enumerate(parts)], axis=0)


def _rec_block(b, q_s, k_s, v_s, d, *, forward, want_out, blk):
    nch = blk // CHUNK
    rows = slice(b * blk, (b + 1) * blk)
    q, k, g, v = q_s[rows, :], k_s[rows, :], d["g"][rows, :], v_s[rows, :]
    shift = CHUNK.bit_length() - 1
    ri = lax.broadcasted_iota(jnp.int32, (blk, blk), 0)
    ci = lax.broadcasted_iota(jnp.int32, (blk, blk), 1)
    same = jnp.right_shift(ri, shift) == jnp.right_shift(ci, shift)
    tri = (jnp.where(same, ri - ci, -1) >= 0).astype(BF16)
    c3 = jnp.dot(tri, jnp.concatenate(_split3(g), axis=1), preferred_element_type=F32)
    cum = c3[:, :HEAD_DIM] + c3[:, HEAD_DIM:2 * HEAD_DIM] + c3[:, 2 * HEAD_DIM:]
    qe, ke, qd, kd = [], [], [], []
    for c in range(nch):
        cs = slice(c * CHUNK, (c + 1) * CHUNK)
        if forward:
            a = cum[cs]
            ref, tot = a[CHUNK // 2:CHUNK // 2 + 1], a[CHUNK - 1:CHUNK]
        else:
            a = cum[cs][CHUNK - 1:CHUNK] - cum[cs] + g[cs]
            ref, tot = a[CHUNK // 2 - 1:CHUNK // 2], a[0:1]
        qe.append(q[cs] * jnp.exp(a - ref))
        ke.append(k[cs] * jnp.exp(ref - a))
        qd.append((q[cs] * jnp.exp(a)).astype(BF16))
        kd.append((k[cs] * jnp.exp(tot - a)).astype(BF16))
        d["dec"][(b * nch + c) * 8:(b * nch + c + 1) * 8, :] = jnp.broadcast_to(jnp.exp(tot), (8, HEAD_DIM))
    if want_out:
        s = lax.dot_general(jnp.concatenate(qe, axis=0).astype(BF16), jnp.concatenate(ke, axis=0).astype(BF16),
                            (((1,), (1,)), ((), ())), preferred_element_type=F32)
        if forward:
            mask = jnp.where(same, ri - ci, -1) >= 0
        else:
            mask = jnp.where(same, ci - ri, 0) > 0
        d["o"][rows, :] = jnp.dot(jnp.where(mask, s, 0.0).astype(BF16), v, preferred_element_type=F32)
        d["qd"][rows, :] = _place(qd)
    v_t = jnp.transpose(v.astype(F32)).astype(BF16)
    d["dst"][b] = jnp.dot(v_t, _place(kd), preferred_element_type=F32)


_REC_DIR_SCRATCH = ("o", "dst", "stc", "qd", "dec")


def _rec_kernel(*refs, kind, ctx_out, t, tc, blk):
    it = iter(refs)
    ql, kl, vl, gl, qc, kc, vc, gc = (next(it) for _ in range(8))
    if kind == "gla":
        lrl, lrc, w2, b2, norm = (next(it) for _ in range(5))
    else:
        lg, cos, sa, sb = (next(it) for _ in range(4))
    yl = next(it)
    yc = next(it) if ctx_out else None
    q_s, k_s, v_s, gf_s, gb_s = (next(it) for _ in range(5))
    dirs = [dict(zip(("g",) + _REC_DIR_SCRATCH, (g_s,) + tuple(next(it) for _ in _REC_DIR_SCRATCH)))
            for g_s in (gf_s, gb_s)]
    scale = HEAD_DIM ** -0.5
    nch, nb, nbc = blk // CHUNK, (tc + t) // blk, tc // blk
    ctx_rows, lat_rows = slice(0, tc), slice(tc, tc + t)

    for rows, q_in, k_in, v_in, lr_in, roped in ((ctx_rows, qc, kc, vc, lrc if kind == "gla" else None, False),
                                                 (lat_rows, ql, kl, vl, lrl if kind == "gla" else None, True)):
        q, k = q_in[...].astype(F32), k_in[...].astype(F32)
        if kind == "gla":
            logits = jnp.dot(lr_in[...].astype(BF16), w2[...], preferred_element_type=F32) + b2[...]
            lsig = (jnp.minimum(logits, 0.0) - jnp.log1p(jnp.exp(-jnp.abs(logits)))) * (1.0 / GLA_GATE_NORM)
            gf_s[rows, :] = lsig[:, :HEAD_DIM]
            gb_s[rows, :] = lsig[:, HEAD_DIM:]
            q = q * scale
        else:
            gf_s[rows, :] = jnp.broadcast_to(lg[...], q.shape)
            gb_s[rows, :] = jnp.broadcast_to(lg[...], q.shape)
            if roped:
                q = _rope(q, cos[...], sa[...], sb[...])
                k = _rope(k, cos[...], sa[...], sb[...])
            k = k * scale
        q_s[rows, :] = q
        k_s[rows, :] = k
        v_s[rows, :] = v_in[...].astype(BF16)

    for b in range(nb):
        for d, fwd in zip(dirs, (True, False)):
            _rec_block(b, q_s, k_s, v_s, d, forward=fwd, want_out=ctx_out or b >= nbc, blk=blk)

    for d, fwd in zip(dirs, (True, False)):
        order = range(nb) if fwd else list(range(nbc - 1, -1, -1)) + list(range(nb - 1, nbc - 1, -1))
        st = jnp.zeros(d["dst"].shape[1:2] + (HEAD_DIM,), F32)
        for b in order:
            for c in (range(nch) if fwd else range(nch - 1, -1, -1)):
                lanes = slice(c * HEAD_DIM, (c + 1) * HEAD_DIM)
                d["stc"][b, :, lanes] = st.astype(BF16)
                st = st * d["dec"][(b * nch + c) * 8:(b * nch + c) * 8 + 1, :] + d["dst"][b, :, lanes]

    for b in range(0 if ctx_out else nbc, nb):
        rows = slice(b * blk, (b + 1) * blk)
        for d in dirs:
            d["o"][rows, :] += lax.dot_general(d["qd"][rows, :], d["stc"][b], (((1,), (1,)), ((), ())),
                                               preferred_element_type=F32)

    def finish(rows, g_ref, y_ref):
        o = dirs[0]["o"][rows, :] + dirs[1]["o"][rows, :]
        if kind == "gla":
            o = _rms(o, norm[...])
        else:
            mu = jnp.mean(o, axis=-1, keepdims=True)
            dlt = o - mu
            o = dlt * lax.rsqrt(jnp.mean(dlt * dlt, axis=-1, keepdims=True) + EPS)
        g = g_ref[...].astype(F32)
        y_ref[...] = (o * (g * jax.nn.sigmoid(g))).astype(y_ref.dtype)

    finish(lat_rows, gl, yl)
    if ctx_out:
        finish(ctx_rows, gc, yc)


def _recurrence(zl, zc, lay, kind, params, *, batch, t, tc, ctx_out, name):
    dv = lay.src[kind + "_v"][1] // REC_HEADS
    qb, kb = lay.blk(kind + "_q", HEAD_DIM), lay.blk(kind + "_k", HEAD_DIM)
    vb, gb = lay.blk(kind + "_v", dv), lay.blk(kind + "_g", dv)

    def blocks(rows):
        return [pl.BlockSpec((rows, HEAD_DIM), lambda b, h: (b, qb + h)),
                pl.BlockSpec((rows, HEAD_DIM), lambda b, h: (b, kb + h)),
                pl.BlockSpec((rows, dv), lambda b, h: (b, vb + h)),
                pl.BlockSpec((rows, dv), lambda b, h: (b, gb + h))]

    in_specs = blocks(t) + blocks(tc)
    args = [zl] * 4 + [zc] * 4
    if kind == "gla":
        w2cat, b2cat, norm = params
        lrb = lay.blk("gla_lr", LANES)
        in_specs += [pl.BlockSpec((t, LANES), lambda b, h: (b, lrb)), pl.BlockSpec((tc, LANES), lambda b, h: (b, lrb)),
                     pl.BlockSpec((None, LANES, 2 * HEAD_DIM), lambda b, h: (h, 0, 0)),
                     pl.BlockSpec((None, 1, 2 * HEAD_DIM), lambda b, h: (h, 0, 0)),
                     pl.BlockSpec((1, dv), lambda b, h: (0, 0))]
        args += [zl, zc, w2cat, b2cat, norm]
    else:
        lg, rope = params
        full = pl.BlockSpec((t, HEAD_DIM), lambda b, h: (0, 0))
        in_specs += [pl.BlockSpec((None, 1, HEAD_DIM), lambda b, h: (h, 0, 0)), full, full, full]
        args += [lg, *rope]
    out_shape = [jax.ShapeDtypeStruct((batch * t, REC_HEADS * dv), BF16)]
    out_specs = [pl.BlockSpec((t, dv), lambda b, h: (b, h))]
    if ctx_out:
        out_shape.append(jax.ShapeDtypeStruct((batch * tc, REC_HEADS * dv), BF16))
        out_specs.append(pl.BlockSpec((tc, dv), lambda b, h: (b, h)))

    blk = math.gcd(math.gcd(t, tc), REC_BLOCK)
    rows, nb, nch = tc + t, (tc + t) // blk, blk // CHUNK
    per_dir = {"o": pltpu.VMEM((rows, dv), F32), "dst": pltpu.VMEM((nb, dv, nch * HEAD_DIM), F32),
               "stc": pltpu.VMEM((nb, dv, nch * HEAD_DIM), BF16), "qd": pltpu.VMEM((rows, nch * HEAD_DIM), BF16),
               "dec": pltpu.VMEM((nb * nch * 8, HEAD_DIM), F32)}
    scratch = ([pltpu.VMEM((rows, HEAD_DIM), F32)] * 2 + [pltpu.VMEM((rows, dv), BF16)]
               + [pltpu.VMEM((rows, HEAD_DIM), F32)] * 2 + [per_dir[n] for n in _REC_DIR_SCRATCH] * 2)
    out = pl.pallas_call(
        functools.partial(_rec_kernel, kind=kind, ctx_out=ctx_out, t=t, tc=tc, blk=blk),
        out_shape=out_shape, grid=(batch, REC_HEADS), in_specs=in_specs, out_specs=out_specs,
        scratch_shapes=scratch,
        compiler_params=_cparams(("parallel", "parallel")), name=name,
    )(*args)
    return out[0], (out[1] if ctx_out else None)


S5_LANE_BLOCK = 1024
S5_TIME_BLOCK = 32


def _s5_prep_kernel(lr_ref, li_ref, ls_ref, br_ref, bi_ref, ar_ref, ai_ref, bbr_ref, bbi_ref):
    lam_r, lam_i = lr_ref[0], li_ref[0]
    dt = jnp.exp(ls_ref[0])
    mag = jnp.exp(lam_r * dt)
    a_r = mag * jnp.cos(lam_i * dt)
    a_i = mag * jnp.sin(lam_i * dt)
    den = lam_r * lam_r + lam_i * lam_i
    z_r = ((a_r - 1.0) * lam_r + a_i * lam_i) / den
    z_i = (a_i * lam_r - (a_r - 1.0) * lam_i) / den
    ar_ref[0] = jnp.broadcast_to(a_r, ar_ref.shape[1:])
    ai_ref[0] = jnp.broadcast_to(a_i, ai_ref.shape[1:])
    nblk, _, cols = br_ref.shape[1:]
    for j in range(nblk):
        zr = z_r[:, j * cols:(j + 1) * cols]
        zi = z_i[:, j * cols:(j + 1) * cols]
        bbr_ref[0, j] = (zr * br_ref[0, j] - zi * bi_ref[0, j]).astype(BF16)
        bbi_ref[0, j] = (zr * bi_ref[0, j] + zi * br_ref[0, j]).astype(BF16)


def _block_diag(w, n_blk):
    *lead, g, r, c = w.shape
    gl = g // n_blk
    w = w.reshape(*lead, n_blk, gl, r, c)
    eye = jnp.eye(gl, dtype=w.dtype)
    out = w[..., :, :, None, :] * eye[:, None, :, None]
    return out.reshape(*lead, n_blk, gl * r, gl * c)


def _s5_prep(lam_re, lam_im, log_step, b_re, b_im, batch):
    _, g, p = lam_re.shape
    n_blk = (g * p) // S5_LANE_BLOCK
    flat = lambda a: a.reshape(2, 1, g * p)
    ls = jnp.broadcast_to(log_step[:, :, None], (2, g, p))
    brd = _block_diag(jnp.swapaxes(b_re, -1, -2), n_blk)
    bid = _block_diag(jnp.swapaxes(b_im, -1, -2), n_blk)
    vec = pl.BlockSpec((1, 1, g * p), lambda d: (d, 0, 0))
    mat = pl.BlockSpec((1,) + brd.shape[1:], lambda d: (d, 0, 0, 0))
    tab = pl.BlockSpec((1, batch, g * p), lambda d: (d, 0, 0))
    return pl.pallas_call(
        _s5_prep_kernel,
        out_shape=[jax.ShapeDtypeStruct((2, batch, g * p), F32)] * 2 + [jax.ShapeDtypeStruct(brd.shape, BF16)] * 2,
        grid=(2,), in_specs=[vec, vec, vec, mat, mat], out_specs=[tab, tab, mat, mat],
        compiler_params=_cparams(("parallel",)), name="s5_prep",
    )(flat(lam_re), flat(lam_im), flat(ls), brd, bid)


def _s5_kernel(uf_ref, ub_ref, ar_ref, ai_ref, bbr_ref, bbi_ref, cr_ref, ci_ref, yf_ref, yb_ref,
               xr_scr, xi_scr, sr_scr, si_scr, *, batch, tb):
    n_blk, kin, _ = bbr_ref.shape[1:]
    kout = cr_ref.shape[3]

    @pl.when(pl.program_id(0) == 0)
    def _():
        sr_scr[...] = jnp.zeros_like(sr_scr)
        si_scr[...] = jnp.zeros_like(si_scr)

    dirs = tuple(enumerate(((uf_ref, yf_ref), (ub_ref, yb_ref))))
    blocks = [slice(j * S5_LANE_BLOCK, (j + 1) * S5_LANE_BLOCK) for j in range(n_blk)]
    for d, (u_ref, _) in dirs:
        u = u_ref[...].astype(BF16)
        for j, lanes in enumerate(blocks):
            uj = u[:, j * kin:(j + 1) * kin]
            xr_scr[d, :, lanes] = jnp.dot(uj, bbr_ref[d, j], preferred_element_type=F32)
            xi_scr[d, :, lanes] = jnp.dot(uj, bbi_ref[d, j], preferred_element_type=F32)
    for d, _ in dirs:
        for lanes in blocks:
            a_r, a_i = ar_ref[d, :, lanes], ai_ref[d, :, lanes]
            s_r, s_i = sr_scr[d, :, lanes], si_scr[d, :, lanes]
            for s in (range(tb) if d == 0 else range(tb - 1, -1, -1)):
                rows = slice(s * batch, (s + 1) * batch)
                s_r, s_i = (a_r * s_r - a_i * s_i + xr_scr[d, rows, lanes],
                            a_r * s_i + a_i * s_r + xi_scr[d, rows, lanes])
                xr_scr[d, rows, lanes] = s_r
                xi_scr[d, rows, lanes] = s_i
            sr_scr[d, :, lanes] = s_r
            si_scr[d, :, lanes] = s_i
    for d, (_, y_ref) in dirs:
        for j, lanes in enumerate(blocks):
            y = (jnp.dot(xr_scr[d, :, lanes].astype(BF16), cr_ref[d, j], preferred_element_type=F32)
                 - jnp.dot(xi_scr[d, :, lanes].astype(BF16), ci_ref[d, j], preferred_element_type=F32))
            y_ref[:, j * kout:(j + 1) * kout] = y.astype(y_ref.dtype)


def _s5_scan(u_tm, tabs, c_re_bd, c_im_bd, *, batch, t, tc):
    a_r, a_i, bbr, bbi = tabs
    rows, width = u_tm.shape
    tb = _tile(math.gcd(t, tc), (S5_TIME_BLOCK, 16, 8, 4, 2, 1))
    nbc, nbl = tc // tb, t // tb
    nb = nbc + nbl
    blk = tb * batch

    def bwd(i):
        return jnp.where(i < nbc, nbc - 1 - i, nb - 1 - (i - nbc))

    full4 = lambda a: pl.BlockSpec(a.shape, lambda i: (0, 0, 0, 0), pipeline_mode=pl.Buffered(1))
    full3 = lambda a: pl.BlockSpec(a.shape, lambda i: (0, 0, 0), pipeline_mode=pl.Buffered(1))
    n_state = a_r.shape[-1]
    return pl.pallas_call(
        functools.partial(_s5_kernel, batch=batch, tb=tb),
        out_shape=[jax.ShapeDtypeStruct((rows, width), BF16)] * 2,
        grid=(nb,),
        in_specs=[pl.BlockSpec((blk, width), lambda i: (i, 0)), pl.BlockSpec((blk, width), lambda i: (bwd(i), 0)),
                  full3(a_r), full3(a_i), full4(bbr), full4(bbi), full4(c_re_bd), full4(c_im_bd)],
        out_specs=[pl.BlockSpec((blk, width), lambda i: (i, 0)), pl.BlockSpec((blk, width), lambda i: (bwd(i), 0))],
        scratch_shapes=[pltpu.VMEM((2, blk, n_state), F32)] * 2 + [pltpu.VMEM((2, batch, n_state), F32)] * 2,
        compiler_params=_cparams(("arbitrary",)), name="s5_scan",
    )(u_tm, u_tm, a_r, a_i, bbr, bbi, c_re_bd, c_im_bd)


def _glu_kernel(u_ref, yf_ref, yb_ref, d_ref, w_ref, b_ref, o_ref):
    y = u_ref[...].astype(F32) * d_ref[...] + yf_ref[...].astype(F32) + yb_ref[...].astype(F32)
    z = jax.nn.gelu(y, approximate=True)
    gate = jnp.dot(z.astype(BF16), w_ref[...], preferred_element_type=F32) + b_ref[...]
    o_ref[...] = (z * jax.nn.sigmoid(gate)).astype(o_ref.dtype)


def _s5_glu(u_tm, yf, yb, d, glu_w, glu_b):
    rows, width = u_tm.shape
    tm = _tile(rows, (512, 256, 128, 64, 32, 16, 8))
    row = pl.BlockSpec((tm, width), lambda i: (i, 0))
    vec = pl.BlockSpec((1, width), lambda i: (0, 0))
    return pl.pallas_call(
        _glu_kernel, out_shape=jax.ShapeDtypeStruct((rows, width), BF16), grid=(rows // tm,),
        in_specs=[row, row, row, vec, pl.BlockSpec((width, width), lambda i: (0, 0)), vec], out_specs=row,
        compiler_params=_cparams(("parallel",)), name="s5_glu",
    )(u_tm, yf, yb, d.reshape(1, width), glu_w, glu_b.reshape(1, width))


def _merge_kernel(y0, y1, y2, y3, g0, g1, g2, g3, wb_ref, o_ref):
    acc = None
    for i, (y_ref, g_ref) in enumerate(((y0, g0), (y1, g1), (y2, g2), (y3, g3))):
        term = jax.nn.sigmoid(g_ref[...].astype(F32)) * jnp.dot(y_ref[...], wb_ref[i], preferred_element_type=F32)
        acc = term if acc is None else acc + term
    o_ref[...] = acc.astype(o_ref.dtype)


def _merge(ys, z, lay, w_branch, *, name):
    m, mw = ys[0].shape
    d = w_branch.shape[-1]
    tm = _tile(m, (1024, 512, 256, 128, 64, 32, 16))
    tn = _tile(d, (512, 256, 128))
    g_off = lay.off["gates"]
    assert g_off % tn == 0

    def gate_spec(i):
        return pl.BlockSpec((tm, tn), lambda r, c: (r, (g_off + i * d) // tn + c))

    yspec = pl.BlockSpec((tm, mw), lambda r, c: (r, 0))
    return pl.pallas_call(
        _merge_kernel, out_shape=jax.ShapeDtypeStruct((m, d), BF16), grid=(m // tm, d // tn),
        in_specs=[yspec] * N_BRANCH + [gate_spec(i) for i in range(N_BRANCH)]
        + [pl.BlockSpec((N_BRANCH, mw, tn), lambda r, c: (0, 0, c))],
        out_specs=pl.BlockSpec((tm, tn), lambda r, c: (r, c)),
        compiler_params=_cparams(("parallel", "arbitrary")), name=name,
    )(*ys, z, z, z, z, w_branch)


def _ffn_up_kernel(h_ref, w1_ref, w3_ref, o_ref):
    h = h_ref[...]
    a = jnp.dot(h, w1_ref[...], preferred_element_type=F32)
    b = jnp.dot(h, w3_ref[...], preferred_element_type=F32)
    o_ref[...] = (a * jax.nn.sigmoid(a) * b).astype(o_ref.dtype)


def _ffn_up(h, w1, w3, *, name):
    m, k = h.shape
    f = w1.shape[-1]
    tm, tn = _mm_tiles(m, k, f, 2, 2, 2, n_w=2)
    w_spec = pl.BlockSpec((k, tn), lambda i, j: (0, j))
    return pl.pallas_call(
        _ffn_up_kernel, out_shape=jax.ShapeDtypeStruct((m, f), BF16), grid=(m // tm, f // tn),
        in_specs=[pl.BlockSpec((tm, k), lambda i, j: (i, 0)), w_spec, w_spec],
        out_specs=pl.BlockSpec((tm, tn), lambda i, j: (i, j)),
        compiler_params=_cparams(("parallel", "arbitrary")), name=name,
    )(h, w1, w3)


MOE_ROW_TILE = 512
MOE_GATHER_TILE = 256
MOE_COMBINE_TILE = 128
DMA_UNROLL = 8
DMA_PRIORITIES = 2


def _moe_plan(expert_idx, n_experts, tile):
    m = expert_idx.shape[0]
    flat = expert_idx.reshape(-1)
    onehot = (flat[:, None] == jnp.arange(n_experts, dtype=jnp.int32)[None, :]).astype(jnp.int32)
    rank = jnp.take_along_axis(jnp.cumsum(onehot, axis=0) - onehot, flat[:, None], axis=1)[:, 0]
    count = jnp.sum(onehot, axis=0)
    group = ((count + tile - 1) // tile) * tile
    group_end = jnp.cumsum(group)
    dest = (group_end - group)[flat] + rank
    rows = m * TOP_K + n_experts * tile
    row_src = jnp.zeros((rows,), jnp.int32).at[dest].set(jnp.arange(m * TOP_K, dtype=jnp.int32) // TOP_K)
    n_tiles = rows // tile
    n_used = group_end[-1] // tile
    tile_id = jnp.arange(n_tiles, dtype=jnp.int32)
    owner = jnp.searchsorted(group_end, jnp.minimum(tile_id, n_used - 1) * tile, side="right").astype(jnp.int32)
    return dest.reshape(m, TOP_K), row_src, jnp.minimum(owner, n_experts - 1), n_used.reshape(1).astype(jnp.int32)


def _row_copy(src_hbm, src_row, buf, slot, dst_row, sem):
    return pltpu.make_async_copy(src_hbm.at[pl.ds(src_row, 1), :], buf.at[slot, pl.ds(dst_row, 1), :], sem.at[slot])


def _gather_rows(idx_ref, src_hbm, buf, sem, slot, n):
    def body(p, carry):
        for prio in range(DMA_PRIORITIES):
            r = DMA_PRIORITIES * p + prio
            _row_copy(src_hbm, idx_ref[0, r], buf, slot, r, sem).start(priority=prio)
        return carry
    lax.fori_loop(0, n // DMA_PRIORITIES, body, 0, unroll=DMA_UNROLL // DMA_PRIORITIES)


def _wait_rows(src_hbm, buf, sem, slot, n):
    def body(r, carry):
        _row_copy(src_hbm, 0, buf, slot, r, sem).wait()
        return carry
    lax.fori_loop(0, n, body, 0, unroll=DMA_UNROLL)


def _gather_kernel(cur_ref, nxt_ref, src_hbm, o_ref, buf, sem):
    s, n_steps = pl.program_id(0), pl.num_programs(0)
    n = buf.shape[1]

    @pl.when(s == 0)
    def _():
        _gather_rows(cur_ref, src_hbm, buf, sem, 0, n)

    @pl.when(s + 1 < n_steps)
    def _():
        _gather_rows(nxt_ref, src_hbm, buf, sem, (s + 1) % 2, n)

    slot = s % 2
    _wait_rows(src_hbm, buf, sem, slot, n)
    o_ref[...] = buf[slot].astype(o_ref.dtype)


def _moe_gather(h, row_src):
    d = h.shape[1]
    rows = row_src.shape[0]
    tg = _tile(rows, (MOE_GATHER_TILE, 128, 64, 32, 16, 8))
    steps = rows // tg
    idx = row_src.reshape(steps, 1, tg)
    smem = lambda f: pl.BlockSpec((None, 1, tg), f, memory_space=pltpu.SMEM)
    return pl.pallas_call(
        _gather_kernel, out_shape=jax.ShapeDtypeStruct((rows, d), BF16), grid=(steps,),
        in_specs=[smem(lambda s: (s, 0, 0)), smem(lambda s: (jnp.minimum(s + 1, steps - 1), 0, 0)),
                  pl.BlockSpec(memory_space=pl.ANY)],
        out_specs=pl.BlockSpec((tg, d), lambda s: (s, 0)),
        scratch_shapes=[pltpu.VMEM((2, tg, d), F32), pltpu.SemaphoreType.DMA((2,))],
        compiler_params=_cparams(("arbitrary",)), name="moe_gather",
    )(idx, idx, h)


def _moe_grouped_kernel(te_ref, nu_ref, x_ref, *refs, n_w):
    w_refs, o_ref, w_scrs = refs[:n_w], refs[n_w], refs[n_w + 1:]
    i = pl.program_id(1)

    @pl.when(jnp.logical_or(i == 0, te_ref[i] != te_ref[jnp.maximum(i - 1, 0)]))
    def _():
        for w_ref, w_scr in zip(w_refs, w_scrs):
            w_scr[...] = w_ref[...].astype(BF16)

    @pl.when(i < nu_ref[0])
    def _():
        if n_w == 2:
            _ffn_up_kernel(x_ref, *w_scrs, o_ref)
        else:
            o_ref[...] = jnp.dot(x_ref[...], w_scrs[0][...], preferred_element_type=F32).astype(o_ref.dtype)

    @pl.when(i >= nu_ref[0])
    def _():
        o_ref[...] = jnp.zeros_like(o_ref)


def _moe_grouped(a, weights, tile_expert, n_used, *, tn, out_dtype, name):
    rows, k = a.shape
    n = weights[0].shape[-1]
    tm = MOE_ROW_TILE
    used = lambda i, nu: jnp.minimum(i, nu[0] - 1)
    w_spec = pl.BlockSpec((None, k, tn), lambda j, i, te, nu: (te[i], 0, j))
    return pl.pallas_call(
        functools.partial(_moe_grouped_kernel, n_w=len(weights)),
        out_shape=jax.ShapeDtypeStruct((rows, n), out_dtype),
        grid_spec=pltpu.PrefetchScalarGridSpec(
            num_scalar_prefetch=2, grid=(n // tn, rows // tm),
            in_specs=[pl.BlockSpec((tm, k), lambda j, i, te, nu: (used(i, nu), 0))] + [w_spec] * len(weights),
            out_specs=pl.BlockSpec((tm, tn), lambda j, i, te, nu: (i, j)),
            scratch_shapes=[pltpu.VMEM((k, tn), BF16)] * len(weights)),
        compiler_params=_cparams(("arbitrary", "arbitrary")), name=name,
    )(tile_expert, n_used, a, *weights)


def _moe_combine_kernel(*refs, alpha, emit_h):
    it = iter(refs)
    cur_ref, nxt_ref, x_ref, wts_ref, gate_ref, g_ref, b_ref = (next(it) for _ in range(7))
    if emit_h:
        sc_ref, sh_ref = next(it), next(it)
    y_hbm, xo_ref = next(it), next(it)
    ho_ref = next(it) if emit_h else None
    buf, sem = next(it), next(it)
    s, n_steps = pl.program_id(0), pl.num_programs(0)
    n = buf.shape[1]
    tm = n // TOP_K

    @pl.when(s == 0)
    def _():
        _gather_rows(cur_ref, y_hbm, buf, sem, 0, n)

    @pl.when(s + 1 < n_steps)
    def _():
        _gather_rows(nxt_ref, y_hbm, buf, sem, (s + 1) % 2, n)

    slot = s % 2
    _wait_rows(y_hbm, buf, sem, slot, n)
    f = wts_ref[:, 0:1] * buf[slot, 0:tm, :] + wts_ref[:, 1:2] * buf[slot, tm:2 * tm, :]
    xn = _layer_norm(alpha * x_ref[...] + gate_ref[0] * f, g_ref[...], b_ref[...])
    xo_ref[...] = xn
    if emit_h:
        ho_ref[...] = (xn * (1.0 + sc_ref[0]) + sh_ref[0]).astype(ho_ref.dtype)


def _moe_combine_ln(x, y_sorted, dest, wts, mods, ln_g, ln_b, *, alpha, rows_per_group, g0, gate_chunk,
                    next_mods, scale_chunk, shift_chunk, name):
    m, d = x.shape
    tm = _tile(rows_per_group or m, (MOE_COMBINE_TILE, 64, 32, 16, 8))
    steps = m // tm
    idx = jnp.swapaxes(dest.reshape(steps, tm, TOP_K), 1, 2).reshape(steps, 1, TOP_K * tm)
    emit_h = next_mods is not None
    row = pl.BlockSpec((tm, d), lambda s: (s, 0))
    vec = pl.BlockSpec((1, d), lambda s: (0, 0))
    smem = lambda f: pl.BlockSpec((None, 1, TOP_K * tm), f, memory_space=pltpu.SMEM)
    in_specs = [smem(lambda s: (s, 0, 0)), smem(lambda s: (jnp.minimum(s + 1, steps - 1), 0, 0)), row,
                pl.BlockSpec((tm, LANES), lambda s: (s, 0)),
                pl.BlockSpec((1, 1, d), _group_map(rows_per_group, tm, g0, gate_chunk)), vec, vec]
    args = [idx, idx, x, wts, mods, ln_g.reshape(1, d), ln_b.reshape(1, d)]
    out_shape = [jax.ShapeDtypeStruct((m, d), F32)]
    out_specs = [row]
    if emit_h:
        in_specs += [pl.BlockSpec((1, 1, d), _group_map(rows_per_group, tm, g0, scale_chunk)),
                     pl.BlockSpec((1, 1, d), _group_map(rows_per_group, tm, g0, shift_chunk))]
        args += [next_mods, next_mods]
        out_shape.append(jax.ShapeDtypeStruct((m, d), BF16))
        out_specs.append(row)
    in_specs.append(pl.BlockSpec(memory_space=pl.ANY))
    args.append(y_sorted)
    out = pl.pallas_call(
        functools.partial(_moe_combine_kernel, alpha=alpha, emit_h=emit_h),
        out_shape=out_shape, grid=(steps,), in_specs=in_specs, out_specs=out_specs,
        scratch_shapes=[pltpu.VMEM((2, TOP_K * tm, d), F32), pltpu.SemaphoreType.DMA((2,))],
        compiler_params=_cparams(("arbitrary",)), name=name,
    )(*args)
    return out if emit_h else (out[0], None)


def _moe_ln(x, h, route, mods, w1, w3, w2, ln_g, ln_b, *, alpha, rows_per_group, g0, next_mods, name):
    n_experts = w1.shape[0]
    expert_idx = route[:, TOP_K:2 * TOP_K].astype(jnp.int32)
    dest, row_src, tile_expert, n_used = _moe_plan(expert_idx, n_experts, MOE_ROW_TILE)
    xs = _moe_gather(h, row_src)
    up = _moe_grouped(xs, (w1, w3), tile_expert, n_used, tn=_tile(w1.shape[-1], (512, 256, 128)), out_dtype=BF16,
                      name=name + "_up")
    ys = _moe_grouped(up, (w2,), tile_expert, n_used, tn=_tile(w2.shape[-1], (1024, 512, 256, 128)), out_dtype=F32,
                      name=name + "_dn")
    return _moe_combine_ln(x, ys, dest, route, mods, ln_g, ln_b, alpha=alpha, rows_per_group=rows_per_group, g0=g0,
                           gate_chunk=5, next_mods=next_mods, scale_chunk=1, shift_chunk=0, name=name + "_ln")


def kernel(x, c, ctx, c_ctx, w_ada, b_ada, w_in, b_in, attn_q_norm, attn_k_norm, gla_w2, gla_b2, gla_norm,
           s5_lambda_re, s5_lambda_im, s5_log_step, s5_b_re, s5_b_im, s5_c_re, s5_c_im, s5_d, s5_glu_w, s5_glu_b,
           w_branch, w_out, ln1_g, ln1_b, ln2_g, ln2_b, ffn_w1, ffn_w3, ffn_w2, moe_router, moe_w1, moe_w3, moe_w2):
    batch, t, d = x.shape
    tc = ctx.shape[1]
    depth = w_ada.shape[0]
    assert batch + 1 <= MOD_ROWS and t % CHUNK == 0 and tc % CHUNK == 0 and t % GRID_W == 0
    lay = _Layout(d)
    assert lay.in_width == w_in.shape[-1]
    alpha = (2 * depth) ** 0.25
    rope = _rope_tables(t)
    log_gamma = jnp.log1p(-jnp.exp2(-5.0 - jnp.arange(REC_HEADS, dtype=F32)))
    lg_tab = jnp.broadcast_to(log_gamma[:, None, None], (REC_HEADS, 1, HEAD_DIM))
    n_blk = (s5_lambda_re.shape[2] * S5_STATE) // S5_LANE_BLOCK

    xl = x.reshape(batch * t, d)
    xc = ctx.reshape(batch * tc, d)
    cvec = jnp.concatenate([c, c_ctx[None], jnp.zeros((MOD_ROWS - batch - 1, d), F32)], axis=0)

    def mods_of(l):
        return _matmul(cvec, w_ada, b_ada[l], a_silu=True, w_index=l, name=f"ada{l}").reshape(MOD_ROWS, 1, 6 * d)

    mods = mods_of(0)
    hl = _modulate(xl, mods, rows_per_group=t, g0=0, scale_chunk=1, shift_chunk=0, name="mod_lat")
    hc = _modulate(xc, mods, rows_per_group=0, g0=batch, scale_chunk=1, shift_chunk=0, name="mod_ctx")

    for l in range(depth):
        ctx_out = l < depth - 1
        w_in_p = _permute_w_in(w_in, l, lay)
        b_in_p = lay.permute(b_in[l])
        zl = _matmul(hl, w_in_p, b_in_p, out_dtype=BF16, w_rows=True, name=f"in_lat{l}")
        zc = _matmul(hc, w_in_p, b_in_p, n_out=None if ctx_out else lay.small, out_dtype=BF16, w_rows=True,
                     name=f"in_ctx{l}")

        y_att = _attention(zl, zc, zl, lay, attn_q_norm[l], attn_k_norm[l], rope, batch=batch, tq_rows=t, tc=tc, t=t,
                           name=f"attn_lat{l}")
        yc_att = (_attention(zc, zc, None, lay, attn_q_norm[l], attn_k_norm[l], None, batch=batch, tq_rows=tc, tc=tc,
                             t=t, name=f"attn_ctx{l}") if ctx_out else None)

        w2h = gla_w2[l].reshape(2, GLA_RANK, REC_HEADS, HEAD_DIM)
        w2cat = jnp.zeros((REC_HEADS, LANES, 2 * HEAD_DIM), F32)
        for r in range(2):
            w2cat = w2cat.at[:, r * GLA_RANK:(r + 1) * GLA_RANK, r * HEAD_DIM:(r + 1) * HEAD_DIM].set(
                jnp.transpose(w2h[r], (1, 0, 2)))
        b2cat = jnp.transpose(gla_b2[l].reshape(2, REC_HEADS, HEAD_DIM), (1, 0, 2)).reshape(REC_HEADS, 1, 2 * HEAD_DIM)
        y_gla, yc_gla = _recurrence(zl, zc, lay, "gla", (w2cat.astype(BF16), b2cat, gla_norm[l].reshape(1, -1)),
                                    batch=batch, t=t, tc=tc, ctx_out=ctx_out, name=f"gla{l}")
        y_ret, yc_ret = _recurrence(zl, zc, lay, "ret", (lg_tab, rope), batch=batch, t=t, tc=tc, ctx_out=ctx_out,
                                    name=f"ret{l}")

        mw = lay.src["s5_u"][1]
        u0 = lay.off["s5_u"]
        u_tm = jnp.concatenate([jnp.swapaxes(zc[:, u0:u0 + mw].reshape(batch, tc, mw), 0, 1),
                                jnp.swapaxes(zl[:, u0:u0 + mw].reshape(batch, t, mw), 0, 1)], axis=0)
        u_tm = u_tm.reshape((tc + t) * batch, mw)
        tabs = _s5_prep(s5_lambda_re[l], s5_lambda_im[l], s5_log_step[l], s5_b_re[l], s5_b_im[l], batch)
        c_re_bd = _block_diag(jnp.swapaxes(s5_c_re[l], -1, -2), n_blk).astype(BF16)
        c_im_bd = _block_diag(jnp.swapaxes(s5_c_im[l], -1, -2), n_blk).astype(BF16)
        yf, yb = _s5_scan(u_tm, tabs, c_re_bd, c_im_bd, batch=batch, t=t, tc=tc)
        y_s5_tm = _s5_glu(u_tm, yf, yb, s5_d[l], s5_glu_w[l].astype(BF16), s5_glu_b[l])
        y_s5_bm = jnp.swapaxes(y_s5_tm.reshape(tc + t, batch, mw), 0, 1)
        y_s5 = y_s5_bm[:, tc:].reshape(batch * t, mw)
        yc_s5 = y_s5_bm[:, :tc].reshape(batch * tc, mw)

        wb = w_branch[l].astype(BF16)
        wo = w_out[l].astype(BF16)
        i = l // 2
        is_moe = l % 2 == 1
        ln1_extra = dict(h_dtype=F32, router=moe_router[i]) if is_moe else {}
        ml = _matmul(_merge((y_gla, y_att, y_ret, y_s5), zl, lay, wb, name=f"merge_lat{l}"), wo, out_dtype=BF16,
                     name=f"out_lat{l}")
        xl, hl, *route_l = _ln_res(xl, ml, mods, ln1_g[l], ln1_b[l], alpha=alpha, rows_per_group=t, g0=0, gate_chunk=2,
                                   next_mods=mods, scale_chunk=4, shift_chunk=3, name=f"ln1_lat{l}", **ln1_extra)
        if ctx_out:
            mc = _matmul(_merge((yc_gla, yc_att, yc_ret, yc_s5), zc, lay, wb, name=f"merge_ctx{l}"), wo,
                         out_dtype=BF16, name=f"out_ctx{l}")
            xc, hc, *route_c = _ln_res(xc, mc, mods, ln1_g[l], ln1_b[l], alpha=alpha, rows_per_group=0, g0=batch,
                                       gate_chunk=2, next_mods=mods, scale_chunk=4, shift_chunk=3,
                                       name=f"ln1_ctx{l}", **ln1_extra)

        next_mods = mods_of(l + 1) if l + 1 < depth else None
        if is_moe:
            w1, w3, w2 = moe_w1[i], moe_w3[i], moe_w2[i]
            xl, hl = _moe_ln(xl, hl, route_l[0], mods, w1, w3, w2, ln2_g[l], ln2_b[l], alpha=alpha,
                             rows_per_group=t, g0=0, next_mods=next_mods, name=f"moe_lat{l}")
            if ctx_out:
                xc, hc = _moe_ln(xc, hc, route_c[0], mods, w1, w3, w2, ln2_g[l], ln2_b[l], alpha=alpha,
                                 rows_per_group=0, g0=batch, next_mods=next_mods, name=f"moe_ctx{l}")
        else:
            w1, w3, w2 = ffn_w1[i].astype(BF16), ffn_w3[i].astype(BF16), ffn_w2[i].astype(BF16)
            fl = _matmul(_ffn_up(hl, w1, w3, name=f"ffn_up_lat{l}"), w2, out_dtype=BF16, name=f"ffn_dn_lat{l}")
            xl, hl = _ln_res(xl, fl, mods, ln2_g[l], ln2_b[l], alpha=alpha, rows_per_group=t, g0=0, gate_chunk=5,
                             next_mods=next_mods, scale_chunk=1, shift_chunk=0, name=f"ln2_lat{l}")
            if ctx_out:
                fc = _matmul(_ffn_up(hc, w1, w3, name=f"ffn_up_ctx{l}"), w2, out_dtype=BF16, name=f"ffn_dn_ctx{l}")
                xc, hc = _ln_res(xc, fc, mods, ln2_g[l], ln2_b[l], alpha=alpha, rows_per_group=0, g0=batch,
                                 gate_chunk=5, next_mods=next_mods, scale_chunk=1, shift_chunk=0, name=f"ln2_ctx{l}")
        mods = next_mods

    return xl.reshape(batch, t, d)
```

```python
import functools
import math

import jax
import jax.numpy as jnp
from jax import lax
from jax.experimental import pallas as pl
from jax.experimental.pallas import tpu as pltpu

F32 = jnp.float32
BF16 = jnp.bfloat16

V7X_VMEM_BYTES = 64 * 1024 * 1024
VMEM_LIMIT = V7X_VMEM_BYTES - 8 * 1024 * 1024
TILE_BUDGET = 52 * 1024 * 1024
LANES = 128

HEAD_DIM = 128
ATT_KV_HEADS = 2
REC_HEADS = 4
GLA_RANK = 16
GLA_GATE_NORM = 16.0
CHUNK = 64
REC_BLOCK = 256
S5_STATE = 64
N_BRANCH = 4
TOP_K = 2
ROPE_THETA = 10000.0
GRID_W = 64
EPS = 1e-6
MOD_ROWS = 16


def _cparams(sem):
    return pltpu.CompilerParams(dimension_semantics=sem, vmem_limit_bytes=VMEM_LIMIT)


def _tile(n, prefs):
    for t in prefs:
        if n % t == 0:
            return t
    return n


def _mm_tiles(m, k, n, a_bytes, w_bytes, o_bytes, n_w=1):
    wide = tuple((tm, tn) for tm in (64, 32, 16, 8) for tn in (1024, 512, 256, 128))
    for tm, tn in ((1024, 1024), (1024, 512), (512, 1024), (512, 512), (256, 512), (256, 256), (128, 256),
                   (128, 128)) + wide:
        if m % tm or n % tn:
            continue
        need = 2 * (tm * k * a_bytes + n_w * k * tn * w_bytes + tm * tn * o_bytes) + n_w * tm * tn * 4
        if need <= TILE_BUDGET:
            return tm, tn
    raise ValueError(f"no matmul tiling for {(m, k, n)}")


class _Layout:
    def __init__(self, d):
        mw = d // 4
        hk = REC_HEADS * HEAD_DIM
        kv = ATT_KV_HEADS * HEAD_DIM
        src_sizes = (mw, kv, kv, hk, hk, mw, mw, 2 * GLA_RANK, hk, hk, mw, mw, mw, N_BRANCH * d)
        names = ("att_q", "att_k", "att_v", "gla_q", "gla_k", "gla_v", "gla_g", "gla_lr",
                 "ret_q", "ret_k", "ret_v", "ret_g", "s5_u", "gates")
        self.src = {}
        off = 0
        for nm, sz in zip(names, src_sizes):
            self.src[nm] = (off, sz)
            off += sz
        self.in_width = off
        order = tuple(nm for nm in names if nm not in ("gla_lr", "gates")) + ("gla_lr",)
        self.off = {}
        self.plan = []
        pos = 0
        for nm in order:
            width = max(self.src[nm][1], LANES)
            self.off[nm] = pos
            self.plan.append((nm, self.src[nm][1]))
            if width > self.src[nm][1]:
                self.plan.append((None, width - self.src[nm][1]))
            pos += width
        small = -(-pos // 1024) * 1024
        if small > pos:
            self.plan.append((None, small - pos))
        self.small = small
        self.off["gates"] = small
        self.plan.append(("gates", self.src["gates"][1]))
        self.width = small + self.src["gates"][1]

    def blk(self, name, width):
        off = self.off[name]
        assert off % width == 0, (name, off, width)
        return off // width

    def permute(self, w):
        parts = []
        for nm, sz in self.plan:
            if nm is None:
                parts.append(jnp.zeros(w.shape[:-1] + (sz,), w.dtype))
            else:
                o = self.src[nm][0]
                parts.append(w[..., o:o + sz])
        return jnp.concatenate(parts, axis=-1)


W_IN_ROW_TILE = 512


def _permute_w_in_kernel(wt_hbm, o_ref, buf, sem, *, layer, j_plain, j_lr, lr_off, lr_w):
    s, n_steps = pl.program_id(0), pl.num_programs(0)
    tn = o_ref.shape[0]

    def tile_copy(j, slot):
        shifted = jnp.where(j < j_lr, j, j - 1) * tn + lr_w
        r0 = jnp.where(j < j_plain, j * tn, jnp.where(j == j_lr, lr_off, shifted))
        return pltpu.make_async_copy(wt_hbm.at[layer, pl.ds(pl.multiple_of(r0, 8), tn), :], buf.at[slot], sem.at[slot])

    @pl.when(s == 0)
    def _():
        tile_copy(s, 0).start()

    @pl.when(s + 1 < n_steps)
    def _():
        tile_copy(s + 1, (s + 1) % 2).start()

    slot = s % 2
    tile_copy(s, slot).wait()
    row = lax.broadcasted_iota(jnp.int32, o_ref.shape, 0)
    keep = row < jnp.where(s == j_lr, lr_w, tn)
    o_ref[...] = jnp.where(keep, buf[slot], 0.0).astype(BF16)


def _permute_w_in(w_in, l, lay):
    wt = jnp.swapaxes(w_in, 1, 2)
    d = wt.shape[2]
    tn = W_IN_ROW_TILE
    lr_off, lr_w = lay.src["gla_lr"]
    g_off = lay.src["gates"][0]
    j_plain, j_lr = lr_off // tn, lay.off["gla_lr"] // tn
    assert lr_off % tn == 0 and lay.off["gla_lr"] % tn == 0 and lay.small - lay.off["gla_lr"] == tn
    assert lay.off["gla_lr"] - lr_off == g_off - (lr_off + lr_w) and lay.small - g_off == tn - lr_w
    assert lay.width % tn == 0 and lr_w % 8 == 0 and lr_off + tn <= lay.in_width
    return pl.pallas_call(
        functools.partial(_permute_w_in_kernel, layer=l, j_plain=j_plain, j_lr=j_lr, lr_off=lr_off, lr_w=lr_w),
        out_shape=jax.ShapeDtypeStruct((lay.width, d), BF16), grid=(lay.width // tn,),
        in_specs=[pl.BlockSpec(memory_space=pl.ANY)],
        out_specs=pl.BlockSpec((tn, d), lambda s: (s, 0)),
        scratch_shapes=[pltpu.VMEM((2, tn, d), F32), pltpu.SemaphoreType.DMA((2,))],
        compiler_params=_cparams(("arbitrary",)), name=f"w_in_perm{l}",
    )(wt)


def _mm_kernel(*refs, has_bias, a_silu, w_rows):
    if has_bias:
        a_ref, w_ref, b_ref, o_ref = refs
    else:
        a_ref, w_ref, o_ref = refs
    a = a_ref[...]
    if a_silu:
        a = a * jax.nn.sigmoid(a)
    contract = (((1,), (1 if w_rows else 0,)), ((), ()))
    acc = lax.dot_general(a.astype(BF16), w_ref[...].astype(BF16), contract, preferred_element_type=F32)
    if has_bias:
        acc = acc + b_ref[...]
    o_ref[...] = acc.astype(o_ref.dtype)


def _matmul(a, w, bias=None, *, n_out=None, out_dtype=F32, a_silu=False, w_index=None, w_rows=False, name):
    m, k = a.shape
    n = n_out or w.shape[0 if w_rows else -1]
    tm, tn = _mm_tiles(m, k, n, a.dtype.itemsize, w.dtype.itemsize, jnp.dtype(out_dtype).itemsize)
    if w_rows:
        w_spec = pl.BlockSpec((tn, k), lambda i, j: (j, 0))
    elif w_index is None:
        w_spec = pl.BlockSpec((k, tn), lambda i, j: (0, j))
    else:
        w_spec = pl.BlockSpec((None, k, tn), lambda i, j: (w_index, 0, j))
    in_specs = [pl.BlockSpec((tm, k), lambda i, j: (i, 0)), w_spec]
    args = [a, w]
    if bias is not None:
        in_specs.append(pl.BlockSpec((1, tn), lambda i, j: (0, j)))
        args.append(bias.reshape(1, -1))
    return pl.pallas_call(
        functools.partial(_mm_kernel, has_bias=bias is not None, a_silu=a_silu, w_rows=w_rows),
        out_shape=jax.ShapeDtypeStruct((m, n), out_dtype),
        grid=(m // tm, n // tn),
        in_specs=in_specs,
        out_specs=pl.BlockSpec((tm, tn), lambda i, j: (i, j)),
        compiler_params=_cparams(("parallel", "arbitrary")),
        name=name,
    )(*args)


def _group_map(rows_per_group, tm, g0, chunk):
    tiles = rows_per_group // tm if rows_per_group else None

    def index_map(i):
        return ((i // tiles + g0) if tiles else g0, 0, chunk)
    return index_map


def _modulate_kernel(x_ref, sc_ref, sh_ref, o_ref):
    o_ref[...] = (x_ref[...] * (1.0 + sc_ref[0]) + sh_ref[0]).astype(o_ref.dtype)


def _modulate(x, mods, *, rows_per_group, g0, scale_chunk, shift_chunk, name):
    m, d = x.shape
    tm = _tile(rows_per_group or m, (512, 256, 128, 64, 32, 16, 8))
    return pl.pallas_call(
        _modulate_kernel,
        out_shape=jax.ShapeDtypeStruct((m, d), BF16),
        grid=(m // tm,),
        in_specs=[pl.BlockSpec((tm, d), lambda i: (i, 0)),
                  pl.BlockSpec((1, 1, d), _group_map(rows_per_group, tm, g0, scale_chunk)),
                  pl.BlockSpec((1, 1, d), _group_map(rows_per_group, tm, g0, shift_chunk))],
        out_specs=pl.BlockSpec((tm, d), lambda i: (i, 0)),
        compiler_params=_cparams(("parallel",)),
        name=name,
    )(x, mods, mods)


def _layer_norm(y, gain, bias):
    mu = jnp.mean(y, axis=-1, keepdims=True)
    dlt = y - mu
    var = jnp.mean(dlt * dlt, axis=-1, keepdims=True)
    return dlt * lax.rsqrt(var + EPS) * gain + bias


def _top2_route(h, rh_ref, rl_ref, n_experts):
    hh = h.astype(BF16)
    hl = (h - hh.astype(F32)).astype(BF16)
    logits = (jnp.dot(hh, rh_ref[...], preferred_element_type=F32) + jnp.dot(hh, rl_ref[...], preferred_element_type=F32)
              + jnp.dot(hl, rh_ref[...], preferred_element_type=F32))
    lane = lax.broadcasted_iota(jnp.int32, logits.shape, 1).astype(F32)
    neg = jnp.float32(-jnp.inf)
    x1 = jnp.where(lane < n_experts, logits, neg)
    m1 = jnp.max(x1, axis=-1, keepdims=True)
    i1 = jnp.min(jnp.where(x1 == m1, lane, float(LANES)), axis=-1, keepdims=True)
    x2 = jnp.where(lane == i1, neg, x1)
    m2 = jnp.max(x2, axis=-1, keepdims=True)
    i2 = jnp.min(jnp.where(x2 == m2, lane, float(LANES)), axis=-1, keepdims=True)
    e2 = jnp.exp(m2 - m1)
    w1 = 1.0 / (1.0 + e2)
    return (jnp.where(lane == 0.0, w1, 0.0) + jnp.where(lane == 1.0, e2 * w1, 0.0)
            + jnp.where(lane == 2.0, i1, 0.0) + jnp.where(lane == 3.0, i2, 0.0))


HI16 = 0xFFFF0000


def _pack_bf16_pairs(h):
    half = h.shape[1] // 2
    bits = lax.bitcast_convert_type(h.astype(BF16).astype(F32), jnp.uint32)
    return jnp.right_shift(bits[:, :half], jnp.uint32(16)) | (bits[:, half:] & jnp.uint32(HI16))


def _unpack_bf16_pairs(w):
    lo = lax.bitcast_convert_type(jnp.left_shift(w, jnp.uint32(16)), F32)
    hi = lax.bitcast_convert_type(w & jnp.uint32(HI16), F32)
    return jnp.concatenate([lo, hi], axis=1).astype(BF16)


def _ln_res_kernel(*refs, alpha, emit_h, n_experts):
    it = iter(refs)
    x_ref, m_ref, gate_ref, g_ref, b_ref = (next(it) for _ in range(5))
    if emit_h:
        sc_ref, sh_ref = next(it), next(it)
    if n_experts:
        rh_ref, rl_ref = next(it), next(it)
    xo_ref = next(it)
    xn = _layer_norm(alpha * x_ref[...] + gate_ref[0] * m_ref[...].astype(F32), g_ref[...], b_ref[...])
    xo_ref[...] = xn
    if emit_h:
        ho_ref = next(it)
        h = xn * (1.0 + sc_ref[0]) + sh_ref[0]
        if n_experts:
            ho_ref[...] = _pack_bf16_pairs(h)
            next(it)[...] = _top2_route(h, rh_ref, rl_ref, n_experts)
        else:
            ho_ref[...] = h.astype(ho_ref.dtype)


def _ln_res(x, mres, mods, ln_g, ln_b, *, alpha, rows_per_group, g0, gate_chunk, next_mods=None,
            scale_chunk=None, shift_chunk=None, router=None, name):
    m, d = x.shape
    tm = _tile(rows_per_group or m, (256, 128, 64, 32, 16, 8))
    emit_h = next_mods is not None
    assert emit_h or router is None
    row = pl.BlockSpec((tm, d), lambda i: (i, 0))
    vec = pl.BlockSpec((1, d), lambda i: (0, 0))
    in_specs = [row, row, pl.BlockSpec((1, 1, d), _group_map(rows_per_group, tm, g0, gate_chunk)), vec, vec]
    args = [x, mres, mods, ln_g.reshape(1, d), ln_b.reshape(1, d)]
    out_shape = [jax.ShapeDtypeStruct((m, d), F32)]
    out_specs = [row]
    if emit_h:
        in_specs += [pl.BlockSpec((1, 1, d), _group_map(rows_per_group, tm, g0, scale_chunk)),
                     pl.BlockSpec((1, 1, d), _group_map(rows_per_group, tm, g0, shift_chunk))]
        args += [next_mods, next_mods]
        if router is None:
            out_shape.append(jax.ShapeDtypeStruct((m, d), BF16))
            out_specs.append(row)
        else:
            out_shape.append(jax.ShapeDtypeStruct((m, d // 2), jnp.uint32))
            out_specs.append(pl.BlockSpec((tm, d // 2), lambda i: (i, 0)))
    n_experts = 0
    if router is not None:
        n_experts = router.shape[-1]
        rp = jnp.zeros((d, LANES), F32).at[:, :n_experts].set(router)
        rh = rp.astype(BF16)
        mat = pl.BlockSpec((d, LANES), lambda i: (0, 0))
        in_specs += [mat, mat]
        args += [rh, (rp - rh.astype(F32)).astype(BF16)]
        out_shape.append(jax.ShapeDtypeStruct((m, LANES), F32))
        out_specs.append(pl.BlockSpec((tm, LANES), lambda i: (i, 0)))
    out = pl.pallas_call(
        functools.partial(_ln_res_kernel, alpha=alpha, emit_h=emit_h, n_experts=n_experts),
        out_shape=out_shape, grid=(m // tm,), in_specs=in_specs, out_specs=out_specs,
        compiler_params=_cparams(("parallel",)), name=name,
    )(*args)
    return out if emit_h else (out[0], None)


def _rope_tables(t):
    nf = HEAD_DIM // 4
    rows = jnp.repeat(jnp.arange(t // GRID_W, dtype=F32), GRID_W)
    cols = (jnp.arange(t) % GRID_W).astype(F32)
    inv = ROPE_THETA ** (-jnp.arange(nf, dtype=F32) / nf)
    ar, ac = rows[:, None] * inv, cols[:, None] * inv
    cr, sr, cc, sc = jnp.cos(ar), jnp.sin(ar), jnp.cos(ac), jnp.sin(ac)
    zero = jnp.zeros_like(sr)
    cos = jnp.concatenate([cr, cr, cc, cc], axis=-1)
    sin_a = jnp.concatenate([-sr, zero, -sc, zero], axis=-1)
    sin_b = jnp.concatenate([zero, sr, zero, sc], axis=-1)
    return cos, sin_a, sin_b


def _rope(x, cos, sin_a, sin_b):
    return x * cos + pltpu.roll(x, 96, 1) * sin_a + pltpu.roll(x, 32, 1) * sin_b


def _rms(x, gain):
    return x * lax.rsqrt(jnp.mean(x * x, axis=-1, keepdims=True) + EPS) * gain


def _attn_kernel(*refs, n_rep, tc, t, has_lat):
    it = iter(refs)
    q_ref, kc_ref, vc_ref, qg_ref, kg_ref = (next(it) for _ in range(5))
    if has_lat:
        kl_ref, vl_ref, cosk, sak, sbk, cosq, saq, sbq = (next(it) for _ in range(8))
    o_ref, k_scr, v_scr = next(it), next(it), next(it)

    @pl.when(pl.program_id(2) == 0)
    def _():
        v_scr[:, HEAD_DIM:] = jnp.ones((v_scr.shape[0], HEAD_DIM), BF16)
        k_scr[0:tc, :] = _rms(kc_ref[...].astype(F32), kg_ref[...]).astype(BF16)
        v_scr[0:tc, 0:HEAD_DIM] = vc_ref[...].astype(BF16)
        if has_lat:
            kl = _rope(_rms(kl_ref[...].astype(F32), kg_ref[...]), cosk[...], sak[...], sbk[...])
            k_scr[tc:tc + t, :] = kl.astype(BF16)
            v_scr[tc:tc + t, 0:HEAD_DIM] = vl_ref[...].astype(BF16)

    scale = HEAD_DIM ** -0.5
    keys = k_scr[...]
    vals = v_scr[...]
    qs = []
    for r in range(n_rep):
        q = _rms(q_ref[:, r * HEAD_DIM:(r + 1) * HEAD_DIM].astype(F32), qg_ref[...])
        if has_lat:
            q = _rope(q, cosq[...], saq[...], sbq[...])
        qs.append((q * scale).astype(BF16))
    tq = q_ref.shape[0]
    s_all = lax.dot_general(jnp.concatenate(qs, axis=0), keys, (((1,), (1,)), ((), ())), preferred_element_type=F32)
    for r in range(n_rep):
        s = s_all[r * tq:(r + 1) * tq]
        e = jnp.exp(s - jnp.max(s, axis=-1, keepdims=True)).astype(BF16)
        ov = jnp.dot(e, vals, preferred_element_type=F32)
        o_ref[:, r * HEAD_DIM:(r + 1) * HEAD_DIM] = (ov[:, :HEAD_DIM] / ov[:, HEAD_DIM:]).astype(o_ref.dtype)


def _attention(zq, zc, zl, lay, q_norm, k_norm, rope, *, batch, tq_rows, tc, t, name):
    has_lat = zl is not None
    n_rep = (lay.src["att_q"][1] // HEAD_DIM) // ATT_KV_HEADS
    qw = n_rep * HEAD_DIM
    tq = _tile(tq_rows, (256, 128, 64, 32, 16, 8))
    nq = tq_rows // tq
    qb, kb, vb = lay.blk("att_q", qw), lay.blk("att_k", HEAD_DIM), lay.blk("att_v", HEAD_DIM)
    vec = pl.BlockSpec((1, HEAD_DIM), lambda b, g, i: (0, 0))
    in_specs = [pl.BlockSpec((tq, qw), lambda b, g, i: (b * nq + i, qb + g)),
                pl.BlockSpec((tc, HEAD_DIM), lambda b, g, i: (b, kb + g)),
                pl.BlockSpec((tc, HEAD_DIM), lambda b, g, i: (b, vb + g)), vec, vec]
    args = [zq, zc, zc, q_norm.reshape(1, -1), k_norm.reshape(1, -1)]
    n_keys = tc
    if has_lat:
        full = pl.BlockSpec((t, HEAD_DIM), lambda b, g, i: (0, 0))
        part = pl.BlockSpec((tq, HEAD_DIM), lambda b, g, i: (i, 0))
        in_specs += [pl.BlockSpec((t, HEAD_DIM), lambda b, g, i: (b, kb + g)),
                     pl.BlockSpec((t, HEAD_DIM), lambda b, g, i: (b, vb + g)), full, full, full, part, part, part]
        args += [zl, zl, *rope, *rope]
        n_keys = tc + t
    return pl.pallas_call(
        functools.partial(_attn_kernel, n_rep=n_rep, tc=tc, t=t, has_lat=has_lat),
        out_shape=jax.ShapeDtypeStruct((batch * tq_rows, lay.src["att_q"][1]), BF16),
        grid=(batch, ATT_KV_HEADS, nq),
        in_specs=in_specs,
        out_specs=pl.BlockSpec((tq, qw), lambda b, g, i: (b * nq + i, g)),
        scratch_shapes=[pltpu.VMEM((n_keys, HEAD_DIM), BF16), pltpu.VMEM((n_keys, 2 * HEAD_DIM), BF16)],
        compiler_params=_cparams(("parallel", "parallel", "arbitrary")),
        name=name,
    )(*args)


def _split3(x):
    hi = x.astype(BF16)
    r1 = x - hi.astype(F32)
    mid = r1.astype(BF16)
    lo = (r1 - mid.astype(F32)).astype(BF16)
    return hi, mid, lo


def _place(parts):
    n = len(parts)
    zero = jnp.zeros_like(parts[0])
    return jnp.concatenate([jnp.concatenate([p if j == c else zero for j in range(n)], axis=1)
                            for c, p in enumerate(parts)], axis=0)


def _rec_block(b, q_s, k_s, v_s, d, *, forward, want_out, blk):
    nch = blk // CHUNK
    rows = slice(b * blk, (b + 1) * blk)
    q, k, g, v = q_s[rows, :], k_s[rows, :], d["g"][rows, :], v_s[rows, :]
    shift = CHUNK.bit_length() - 1
    ri = lax.broadcasted_iota(jnp.int32, (blk, blk), 0)
    ci = lax.broadcasted_iota(jnp.int32, (blk, blk), 1)
    same = jnp.right_shift(ri, shift) == jnp.right_shift(ci, shift)
    tri = (jnp.where(same, ri - ci, -1) >= 0).astype(BF16)
    c3 = jnp.dot(tri, jnp.concatenate(_split3(g), axis=1), preferred_element_type=F32)
    cum = c3[:, :HEAD_DIM] + c3[:, HEAD_DIM:2 * HEAD_DIM] + c3[:, 2 * HEAD_DIM:]
    qe, ke, qd, kd = [], [], [], []
    for c in range(nch):
        cs = slice(c * CHUNK, (c + 1) * CHUNK)
        if forward:
            a = cum[cs]
            ref, tot = a[CHUNK // 2:CHUNK // 2 + 1], a[CHUNK - 1:CHUNK]
        else:
            a = cum[cs][CHUNK - 1:CHUNK] - cum[cs] + g[cs]
            ref, tot = a[CHUNK // 2 - 1:CHUNK // 2], a[0:1]
        qe.append(q[cs] * jnp.exp(a - ref))
        ke.append(k[cs] * jnp.exp(ref - a))
        qd.append((q[cs] * jnp.exp(a)).astype(BF16))
        kd.append((k[cs] * jnp.exp(tot - a)).astype(BF16))
        d["dec"][(b * nch + c) * 8:(b * nch + c + 1) * 8, :] = jnp.broadcast_to(jnp.exp(tot), (8, HEAD_DIM))
    if want_out:
        s = lax.dot_general(jnp.concatenate(qe, axis=0).astype(BF16), jnp.concatenate(ke, axis=0).astype(BF16),
                            (((1,), (1,)), ((), ())), preferred_element_type=F32)
        if forward:
            mask = jnp.where(same, ri - ci, -1) >= 0
        else:
            mask = jnp.where(same, ci - ri, 0) > 0
        d["o"][rows, :] = jnp.dot(jnp.where(mask, s, 0.0).astype(BF16), v, preferred_element_type=F32)
        d["qd"][rows, :] = _place(qd)
    v_t = jnp.transpose(v.astype(F32)).astype(BF16)
    d["dst"][b] = jnp.dot(v_t, _place(kd), preferred_element_type=F32)


_REC_DIR_SCRATCH = ("o", "dst", "stc", "qd", "dec")


def _rec_kernel(*refs, kind, ctx_out, t, tc, blk):
    it = iter(refs)
    ql, kl, vl, gl, qc, kc, vc, gc = (next(it) for _ in range(8))
    if kind == "gla":
        lrl, lrc, w2, b2, norm = (next(it) for _ in range(5))
    else:
        lg, cos, sa, sb = (next(it) for _ in range(4))
    yl = next(it)
    yc = next(it) if ctx_out else None
    q_s, k_s, v_s, gf_s, gb_s = (next(it) for _ in range(5))
    dirs = [dict(zip(("g",) + _REC_DIR_SCRATCH, (g_s,) + tuple(next(it) for _ in _REC_DIR_SCRATCH)))
            for g_s in (gf_s, gb_s)]
    scale = HEAD_DIM ** -0.5
    nch, nb, nbc = blk // CHUNK, (tc + t) // blk, tc // blk
    ctx_rows, lat_rows = slice(0, tc), slice(tc, tc + t)

    for rows, q_in, k_in, v_in, lr_in, roped in ((ctx_rows, qc, kc, vc, lrc if kind == "gla" else None, False),
                                                 (lat_rows, ql, kl, vl, lrl if kind == "gla" else None, True)):
        q, k = q_in[...].astype(F32), k_in[...].astype(F32)
        if kind == "gla":
            logits = jnp.dot(lr_in[...].astype(BF16), w2[...], preferred_element_type=F32) + b2[...]
            lsig = (jnp.minimum(logits, 0.0) - jnp.log1p(jnp.exp(-jnp.abs(logits)))) * (1.0 / GLA_GATE_NORM)
            gf_s[rows, :] = lsig[:, :HEAD_DIM]
            gb_s[rows, :] = lsig[:, HEAD_DIM:]
            q = q * scale
        else:
            gf_s[rows, :] = jnp.broadcast_to(lg[...], q.shape)
            gb_s[rows, :] = jnp.broadcast_to(lg[...], q.shape)
            if roped:
                q = _rope(q, cos[...], sa[...], sb[...])
                k = _rope(k, cos[...], sa[...], sb[...])
            k = k * scale
        q_s[rows, :] = q
        k_s[rows, :] = k
        v_s[rows, :] = v_in[...].astype(BF16)

    for b in range(nb):
        for d, fwd in zip(dirs, (True, False)):
            _rec_block(b, q_s, k_s, v_s, d, forward=fwd, want_out=ctx_out or b >= nbc, blk=blk)

    for d, fwd in zip(dirs, (True, False)):
        order = range(nb) if fwd else list(range(nbc - 1, -1, -1)) + list(range(nb - 1, nbc - 1, -1))
        st = jnp.zeros(d["dst"].shape[1:2] + (HEAD_DIM,), F32)
        for b in order:
            for c in (range(nch) if fwd else range(nch - 1, -1, -1)):
                lanes = slice(c * HEAD_DIM, (c + 1) * HEAD_DIM)
                d["stc"][b, :, lanes] = st.astype(BF16)
                st = st * d["dec"][(b * nch + c) * 8:(b * nch + c) * 8 + 1, :] + d["dst"][b, :, lanes]

    for b in range(0 if ctx_out else nbc, nb):
        rows = slice(b * blk, (b + 1) * blk)
        for d in dirs:
            d["o"][rows, :] += lax.dot_general(d["qd"][rows, :], d["stc"][b], (((1,), (1,)), ((), ())),
                                               preferred_element_type=F32)

    def finish(rows, g_ref, y_ref):
        o = dirs[0]["o"][rows, :] + dirs[1]["o"][rows, :]
        if kind == "gla":
            o = _rms(o, norm[...])
        else:
            mu = jnp.mean(o, axis=-1, keepdims=True)
            dlt = o - mu
            o = dlt * lax.rsqrt(jnp.mean(dlt * dlt, axis=-1, keepdims=True) + EPS)
        g = g_ref[...].astype(F32)
        y_ref[...] = (o * (g * jax.nn.sigmoid(g))).astype(y_ref.dtype)

    finish(lat_rows, gl, yl)
    if ctx_out:
        finish(ctx_rows, gc, yc)


def _recurrence(zl, zc, lay, kind, params, *, batch, t, tc, ctx_out, name):
    dv = lay.src[kind + "_v"][1] // REC_HEADS
    qb, kb = lay.blk(kind + "_q", HEAD_DIM), lay.blk(kind + "_k", HEAD_DIM)
    vb, gb = lay.blk(kind + "_v", dv), lay.blk(kind + "_g", dv)

    def blocks(rows):
        return [pl.BlockSpec((rows, HEAD_DIM), lambda b, h: (b, qb + h)),
                pl.BlockSpec((rows, HEAD_DIM), lambda b, h: (b, kb + h)),
                pl.BlockSpec((rows, dv), lambda b, h: (b, vb + h)),
                pl.BlockSpec((rows, dv), lambda b, h: (b, gb + h))]

    in_specs = blocks(t) + blocks(tc)
    args = [zl] * 4 + [zc] * 4
    if kind == "gla":
        w2cat, b2cat, norm = params
        lrb = lay.blk("gla_lr", LANES)
        in_specs += [pl.BlockSpec((t, LANES), lambda b, h: (b, lrb)), pl.BlockSpec((tc, LANES), lambda b, h: (b, lrb)),
                     pl.BlockSpec((None, LANES, 2 * HEAD_DIM), lambda b, h: (h, 0, 0)),
                     pl.BlockSpec((None, 1, 2 * HEAD_DIM), lambda b, h: (h, 0, 0)),
                     pl.BlockSpec((1, dv), lambda b, h: (0, 0))]
        args += [zl, zc, w2cat, b2cat, norm]
    else:
        lg, rope = params
        full = pl.BlockSpec((t, HEAD_DIM), lambda b, h: (0, 0))
        in_specs += [pl.BlockSpec((None, 1, HEAD_DIM), lambda b, h: (h, 0, 0)), full, full, full]
        args += [lg, *rope]
    out_shape = [jax.ShapeDtypeStruct((batch * t, REC_HEADS * dv), BF16)]
    out_specs = [pl.BlockSpec((t, dv), lambda b, h: (b, h))]
    if ctx_out:
        out_shape.append(jax.ShapeDtypeStruct((batch * tc, REC_HEADS * dv), BF16))
        out_specs.append(pl.BlockSpec((tc, dv), lambda b, h: (b, h)))

    blk = math.gcd(math.gcd(t, tc), REC_BLOCK)
    rows, nb, nch = tc + t, (tc + t) // blk, blk // CHUNK
    per_dir = {"o": pltpu.VMEM((rows, dv), F32), "dst": pltpu.VMEM((nb, dv, nch * HEAD_DIM), F32),
               "stc": pltpu.VMEM((nb, dv, nch * HEAD_DIM), BF16), "qd": pltpu.VMEM((rows, nch * HEAD_DIM), BF16),
               "dec": pltpu.VMEM((nb * nch * 8, HEAD_DIM), F32)}
    scratch = ([pltpu.VMEM((rows, HEAD_DIM), F32)] * 2 + [pltpu.VMEM((rows, dv), BF16)]
               + [pltpu.VMEM((rows, HEAD_DIM), F32)] * 2 + [per_dir[n] for n in _REC_DIR_SCRATCH] * 2)
    out = pl.pallas_call(
        functools.partial(_rec_kernel, kind=kind, ctx_out=ctx_out, t=t, tc=tc, blk=blk),
        out_shape=out_shape, grid=(batch, REC_HEADS), in_specs=in_specs, out_specs=out_specs,
        scratch_shapes=scratch,
        compiler_params=_cparams(("parallel", "parallel")), name=name,
    )(*args)
    return out[0], (out[1] if ctx_out else None)


S5_LANE_BLOCK = 1024
S5_TIME_BLOCK = 32


def _s5_prep_kernel(lr_ref, li_ref, ls_ref, br_ref, bi_ref, ar_ref, ai_ref, bbr_ref, bbi_ref):
    lam_r, lam_i = lr_ref[0], li_ref[0]
    dt = jnp.exp(ls_ref[0])
    mag = jnp.exp(lam_r * dt)
    a_r = mag * jnp.cos(lam_i * dt)
    a_i = mag * jnp.sin(lam_i * dt)
    den = lam_r * lam_r + lam_i * lam_i
    z_r = ((a_r - 1.0) * lam_r + a_i * lam_i) / den
    z_i = (a_i * lam_r - (a_r - 1.0) * lam_i) / den
    ar_ref[0] = jnp.broadcast_to(a_r, ar_ref.shape[1:])
    ai_ref[0] = jnp.broadcast_to(a_i, ai_ref.shape[1:])
    nblk, _, cols = br_ref.shape[1:]
    for j in range(nblk):
        zr = z_r[:, j * cols:(j + 1) * cols]
        zi = z_i[:, j * cols:(j + 1) * cols]
        bbr_ref[0, j] = (zr * br_ref[0, j] - zi * bi_ref[0, j]).astype(BF16)
        bbi_ref[0, j] = (zr * bi_ref[0, j] + zi * br_ref[0, j]).astype(BF16)


def _block_diag(w, n_blk):
    *lead, g, r, c = w.shape
    gl = g // n_blk
    w = w.reshape(*lead, n_blk, gl, r, c)
    eye = jnp.eye(gl, dtype=w.dtype)
    out = w[..., :, :, None, :] * eye[:, None, :, None]
    return out.reshape(*lead, n_blk, gl * r, gl * c)


def _s5_prep(lam_re, lam_im, log_step, b_re, b_im, batch):
    _, g, p = lam_re.shape
    n_blk = (g * p) // S5_LANE_BLOCK
    flat = lambda a: a.reshape(2, 1, g * p)
    ls = jnp.broadcast_to(log_step[:, :, None], (2, g, p))
    brd = _block_diag(jnp.swapaxes(b_re, -1, -2), n_blk)
    bid = _block_diag(jnp.swapaxes(b_im, -1, -2), n_blk)
    vec = pl.BlockSpec((1, 1, g * p), lambda d: (d, 0, 0))
    mat = pl.BlockSpec((1,) + brd.shape[1:], lambda d: (d, 0, 0, 0))
    tab = pl.BlockSpec((1, batch, g * p), lambda d: (d, 0, 0))
    return pl.pallas_call(
        _s5_prep_kernel,
        out_shape=[jax.ShapeDtypeStruct((2, batch, g * p), F32)] * 2 + [jax.ShapeDtypeStruct(brd.shape, BF16)] * 2,
        grid=(2,), in_specs=[vec, vec, vec, mat, mat], out_specs=[tab, tab, mat, mat],
        compiler_params=_cparams(("parallel",)), name="s5_prep",
    )(flat(lam_re), flat(lam_im), flat(ls), brd, bid)


def _s5_kernel(uf_ref, ub_ref, ar_ref, ai_ref, bbr_ref, bbi_ref, cr_ref, ci_ref, yf_ref, yb_ref,
               xr_scr, xi_scr, sr_scr, si_scr, *, batch, tb):
    n_blk, kin, _ = bbr_ref.shape[1:]
    kout = cr_ref.shape[3]

    @pl.when(pl.program_id(0) == 0)
    def _():
        sr_scr[...] = jnp.zeros_like(sr_scr)
        si_scr[...] = jnp.zeros_like(si_scr)

    dirs = tuple(enumerate(((uf_ref, yf_ref), (ub_ref, yb_ref))))
    blocks = [slice(j * S5_LANE_BLOCK, (j + 1) * S5_LANE_BLOCK) for j in range(n_blk)]
    for d, (u_ref, _) in dirs:
        u = u_ref[...].astype(BF16)
        for j, lanes in enumerate(blocks):
            uj = u[:, j * kin:(j + 1) * kin]
            xr_scr[d, :, lanes] = jnp.dot(uj, bbr_ref[d, j], preferred_element_type=F32)
            xi_scr[d, :, lanes] = jnp.dot(uj, bbi_ref[d, j], preferred_element_type=F32)
    for d, _ in dirs:
        for lanes in blocks:
            a_r, a_i = ar_ref[d, :, lanes], ai_ref[d, :, lanes]
            s_r, s_i = sr_scr[d, :, lanes], si_scr[d, :, lanes]
            for s in (range(tb) if d == 0 else range(tb - 1, -1, -1)):
                rows = slice(s * batch, (s + 1) * batch)
                s_r, s_i = (a_r * s_r - a_i * s_i + xr_scr[d, rows, lanes],
                            a_r * s_i + a_i * s_r + xi_scr[d, rows, lanes])
                xr_scr[d, rows, lanes] = s_r
                xi_scr[d, rows, lanes] = s_i
            sr_scr[d, :, lanes] = s_r
            si_scr[d, :, lanes] = s_i
    for d, (_, y_ref) in dirs:
        for j, lanes in enumerate(blocks):
            y = (jnp.dot(xr_scr[d, :, lanes].astype(BF16), cr_ref[d, j], preferred_element_type=F32)
                 - jnp.dot(xi_scr[d, :, lanes].astype(BF16), ci_ref[d, j], preferred_element_type=F32))
            y_ref[:, j * kout:(j + 1) * kout] = y.astype(y_ref.dtype)


def _s5_scan(u_tm, tabs, c_re_bd, c_im_bd, *, batch, t, tc):
    a_r, a_i, bbr, bbi = tabs
    rows, width = u_tm.shape
    tb = _tile(math.gcd(t, tc), (S5_TIME_BLOCK, 16, 8, 4, 2, 1))
    nbc, nbl = tc // tb, t // tb
    nb = nbc + nbl
    blk = tb * batch

    def bwd(i):
        return jnp.where(i < nbc, nbc - 1 - i, nb - 1 - (i - nbc))

    full4 = lambda a: pl.BlockSpec(a.shape, lambda i: (0, 0, 0, 0), pipeline_mode=pl.Buffered(1))
    full3 = lambda a: pl.BlockSpec(a.shape, lambda i: (0, 0, 0), pipeline_mode=pl.Buffered(1))
    n_state = a_r.shape[-1]
    return pl.pallas_call(
        functools.partial(_s5_kernel, batch=batch, tb=tb),
        out_shape=[jax.ShapeDtypeStruct((rows, width), BF16)] * 2,
        grid=(nb,),
        in_specs=[pl.BlockSpec((blk, width), lambda i: (i, 0)), pl.BlockSpec((blk, width), lambda i: (bwd(i), 0)),
                  full3(a_r), full3(a_i), full4(bbr), full4(bbi), full4(c_re_bd), full4(c_im_bd)],
        out_specs=[pl.BlockSpec((blk, width), lambda i: (i, 0)), pl.BlockSpec((blk, width), lambda i: (bwd(i), 0))],
        scratch_shapes=[pltpu.VMEM((2, blk, n_state), F32)] * 2 + [pltpu.VMEM((2, batch, n_state), F32)] * 2,
        compiler_params=_cparams(("arbitrary",)), name="s5_scan",
    )(u_tm, u_tm, a_r, a_i, bbr, bbi, c_re_bd, c_im_bd)


def _glu_kernel(u_ref, yf_ref, yb_ref, d_ref, w_ref, b_ref, o_ref):
    y = u_ref[...].astype(F32) * d_ref[...] + yf_ref[...].astype(F32) + yb_ref[...].astype(F32)
    z = jax.nn.gelu(y, approximate=True)
    gate = jnp.dot(z.astype(BF16), w_ref[...], preferred_element_type=F32) + b_ref[...]
    o_ref[...] = (z * jax.nn.sigmoid(gate)).astype(o_ref.dtype)


def _s5_glu(u_tm, yf, yb, d, glu_w, glu_b):
    rows, width = u_tm.shape
    tm = _tile(rows, (512, 256, 128, 64, 32, 16, 8))
    row = pl.BlockSpec((tm, width), lambda i: (i, 0))
    vec = pl.BlockSpec((1, width), lambda i: (0, 0))
    return pl.pallas_call(
        _glu_kernel, out_shape=jax.ShapeDtypeStruct((rows, width), BF16), grid=(rows // tm,),
        in_specs=[row, row, row, vec, pl.BlockSpec((width, width), lambda i: (0, 0)), vec], out_specs=row,
        compiler_params=_cparams(("parallel",)), name="s5_glu",
    )(u_tm, yf, yb, d.reshape(1, width), glu_w, glu_b.reshape(1, width))


def _merge_kernel(y0, y1, y2, y3, g0, g1, g2, g3, wb_ref, o_ref):
    acc = None
    for i, (y_ref, g_ref) in enumerate(((y0, g0), (y1, g1), (y2, g2), (y3, g3))):
        term = jax.nn.sigmoid(g_ref[...].astype(F32)) * jnp.dot(y_ref[...], wb_ref[i], preferred_element_type=F32)
        acc = term if acc is None else acc + term
    o_ref[...] = acc.astype(o_ref.dtype)


def _merge(ys, z, lay, w_branch, *, name):
    m, mw = ys[0].shape
    d = w_branch.shape[-1]
    tm = _tile(m, (1024, 512, 256, 128, 64, 32, 16))
    tn = _tile(d, (512, 256, 128))
    g_off = lay.off["gates"]
    assert g_off % tn == 0

    def gate_spec(i):
        return pl.BlockSpec((tm, tn), lambda r, c: (r, (g_off + i * d) // tn + c))

    yspec = pl.BlockSpec((tm, mw), lambda r, c: (r, 0))
    return pl.pallas_call(
        _merge_kernel, out_shape=jax.ShapeDtypeStruct((m, d), BF16), grid=(m // tm, d // tn),
        in_specs=[yspec] * N_BRANCH + [gate_spec(i) for i in range(N_BRANCH)]
        + [pl.BlockSpec((N_BRANCH, mw, tn), lambda r, c: (0, 0, c))],
        out_specs=pl.BlockSpec((tm, tn), lambda r, c: (r, c)),
        compiler_params=_cparams(("parallel", "arbitrary")), name=name,
    )(*ys, z, z, z, z, w_branch)


def _ffn_up_kernel(h_ref, w1_ref, w3_ref, o_ref):
    h = h_ref[...]
    a = jnp.dot(h, w1_ref[...], preferred_element_type=F32)
    b = jnp.dot(h, w3_ref[...], preferred_element_type=F32)
    o_ref[...] = (a * jax.nn.sigmoid(a) * b).astype(o_ref.dtype)


def _ffn_up(h, w1, w3, *, name):
    m, k = h.shape
    f = w1.shape[-1]
    tm, tn = _mm_tiles(m, k, f, 2, 2, 2, n_w=2)
    w_spec = pl.BlockSpec((k, tn), lambda i, j: (0, j))
    return pl.pallas_call(
        _ffn_up_kernel, out_shape=jax.ShapeDtypeStruct((m, f), BF16), grid=(m // tm, f // tn),
        in_specs=[pl.BlockSpec((tm, k), lambda i, j: (i, 0)), w_spec, w_spec],
        out_specs=pl.BlockSpec((tm, tn), lambda i, j: (i, j)),
        compiler_params=_cparams(("parallel", "arbitrary")), name=name,
    )(h, w1, w3)


MOE_ROW_TILE = 512
MOE_GATHER_TILE = 256
MOE_COMBINE_TILE = 128
DMA_UNROLL = 8
DMA_PRIORITIES = 2


def _moe_plan(expert_idx, n_experts, tile):
    m = expert_idx.shape[0]
    flat = expert_idx.reshape(-1)
    onehot = (flat[:, None] == jnp.arange(n_experts, dtype=jnp.int32)[None, :]).astype(jnp.int32)
    rank = jnp.take_along_axis(jnp.cumsum(onehot, axis=0) - onehot, flat[:, None], axis=1)[:, 0]
    count = jnp.sum(onehot, axis=0)
    group = ((count + tile - 1) // tile) * tile
    group_end = jnp.cumsum(group)
    dest = (group_end - group)[flat] + rank
    rows = m * TOP_K + n_experts * tile
    row_src = jnp.zeros((rows,), jnp.int32).at[dest].set(jnp.arange(m * TOP_K, dtype=jnp.int32) // TOP_K)
    n_tiles = rows // tile
    n_used = group_end[-1] // tile
    tile_id = jnp.arange(n_tiles, dtype=jnp.int32)
    owner = jnp.searchsorted(group_end, jnp.minimum(tile_id, n_used - 1) * tile, side="right").astype(jnp.int32)
    return dest.reshape(m, TOP_K), row_src, jnp.minimum(owner, n_experts - 1), n_used.reshape(1).astype(jnp.int32)


def _row_copy(src_hbm, src_row, buf, slot, dst_row, sem):
    return pltpu.make_async_copy(src_hbm.at[pl.ds(src_row, 1), :], buf.at[slot, pl.ds(dst_row, 1), :], sem.at[slot])


def _gather_rows(idx_ref, src_hbm, buf, sem, slot, n):
    def body(p, carry):
        for prio in range(DMA_PRIORITIES):
            r = DMA_PRIORITIES * p + prio
            _row_copy(src_hbm, idx_ref[0, r], buf, slot, r, sem).start(priority=prio)
        return carry
    lax.fori_loop(0, n // DMA_PRIORITIES, body, 0, unroll=DMA_UNROLL // DMA_PRIORITIES)


def _wait_rows(src_hbm, buf, sem, slot, n):
    def body(r, carry):
        _row_copy(src_hbm, 0, buf, slot, r, sem).wait()
        return carry
    lax.fori_loop(0, n, body, 0, unroll=DMA_UNROLL)


def _gather_kernel(cur_ref, nxt_ref, src_hbm, o_ref, buf, sem):
    s, n_steps = pl.program_id(0), pl.num_programs(0)
    n = buf.shape[1]

    @pl.when(s == 0)
    def _():
        _gather_rows(cur_ref, src_hbm, buf, sem, 0, n)

    @pl.when(s + 1 < n_steps)
    def _():
        _gather_rows(nxt_ref, src_hbm, buf, sem, (s + 1) % 2, n)

    slot = s % 2
    _wait_rows(src_hbm, buf, sem, slot, n)
    o_ref[...] = _unpack_bf16_pairs(buf[slot])


def _moe_gather(h, row_src):
    d = 2 * h.shape[1]
    rows = row_src.shape[0]
    tg = _tile(rows, (MOE_GATHER_TILE, 128, 64, 32, 16, 8))
    steps = rows // tg
    idx = row_src.reshape(steps, 1, tg)
    smem = lambda f: pl.BlockSpec((None, 1, tg), f, memory_space=pltpu.SMEM)
    return pl.pallas_call(
        _gather_kernel, out_shape=jax.ShapeDtypeStruct((rows, d), BF16), grid=(steps,),
        in_specs=[smem(lambda s: (s, 0, 0)), smem(lambda s: (jnp.minimum(s + 1, steps - 1), 0, 0)),
                  pl.BlockSpec(memory_space=pl.ANY)],
        out_specs=pl.BlockSpec((tg, d), lambda s: (s, 0)),
        scratch_shapes=[pltpu.VMEM((2, tg, d // 2), jnp.uint32), pltpu.SemaphoreType.DMA((2,))],
        compiler_params=_cparams(("arbitrary",)), name="moe_gather",
    )(idx, idx, h)


def _moe_grouped_kernel(te_ref, nu_ref, x_ref, *refs, n_w):
    w_refs, o_ref, w_scrs = refs[:n_w], refs[n_w], refs[n_w + 1:]
    i = pl.program_id(1)

    @pl.when(jnp.logical_or(i == 0, te_ref[i] != te_ref[jnp.maximum(i - 1, 0)]))
    def _():
        for w_ref, w_scr in zip(w_refs, w_scrs):
            w_scr[...] = w_ref[...].astype(BF16)

    @pl.when(i < nu_ref[0])
    def _():
        if n_w == 2:
            _ffn_up_kernel(x_ref, *w_scrs, o_ref)
        else:
            o_ref[...] = jnp.dot(x_ref[...], w_scrs[0][...], preferred_element_type=F32).astype(o_ref.dtype)

    @pl.when(i >= nu_ref[0])
    def _():
        o_ref[...] = jnp.zeros_like(o_ref)


def _moe_grouped(a, weights, tile_expert, n_used, *, tn, out_dtype, name):
    rows, k = a.shape
    n = weights[0].shape[-1]
    tm = MOE_ROW_TILE
    used = lambda i, nu: jnp.minimum(i, nu[0] - 1)
    w_spec = pl.BlockSpec((None, k, tn), lambda j, i, te, nu: (te[i], 0, j))
    return pl.pallas_call(
        functools.partial(_moe_grouped_kernel, n_w=len(weights)),
        out_shape=jax.ShapeDtypeStruct((rows, n), out_dtype),
        grid_spec=pltpu.PrefetchScalarGridSpec(
            num_scalar_prefetch=2, grid=(n // tn, rows // tm),
            in_specs=[pl.BlockSpec((tm, k), lambda j, i, te, nu: (used(i, nu), 0))] + [w_spec] * len(weights),
            out_specs=pl.BlockSpec((tm, tn), lambda j, i, te, nu: (i, j)),
            scratch_shapes=[pltpu.VMEM((k, tn), BF16)] * len(weights)),
        compiler_params=_cparams(("arbitrary", "arbitrary")), name=name,
    )(tile_expert, n_used, a, *weights)


def _moe_combine_kernel(*refs, alpha, emit_h):
    it = iter(refs)
    cur_ref, nxt_ref, x_ref, wts_ref, gate_ref, g_ref, b_ref = (next(it) for _ in range(7))
    if emit_h:
        sc_ref, sh_ref = next(it), next(it)
    y_hbm, xo_ref = next(it), next(it)
    ho_ref = next(it) if emit_h else None
    buf, sem = next(it), next(it)
    s, n_steps = pl.program_id(0), pl.num_programs(0)
    n = buf.shape[1]
    tm = n // TOP_K

    @pl.when(s == 0)
    def _():
        _gather_rows(cur_ref, y_hbm, buf, sem, 0, n)

    @pl.when(s + 1 < n_steps)
    def _():
        _gather_rows(nxt_ref, y_hbm, buf, sem, (s + 1) % 2, n)

    slot = s % 2
    _wait_rows(y_hbm, buf, sem, slot, n)
    f = wts_ref[:, 0:1] * buf[slot, 0:tm, :] + wts_ref[:, 1:2] * buf[slot, tm:2 * tm, :]
    xn = _layer_norm(alpha * x_ref[...] + gate_ref[0] * f, g_ref[...], b_ref[...])
    xo_ref[...] = xn
    if emit_h:
        ho_ref[...] = (xn * (1.0 + sc_ref[0]) + sh_ref[0]).astype(ho_ref.dtype)


def _moe_combine_ln(x, y_sorted, dest, wts, mods, ln_g, ln_b, *, alpha, rows_per_group, g0, gate_chunk,
                    next_mods, scale_chunk, shift_chunk, name):
    m, d = x.shape
    tm = _tile(rows_per_group or m, (MOE_COMBINE_TILE, 64, 32, 16, 8))
    steps = m // tm
    idx = jnp.swapaxes(dest.reshape(steps, tm, TOP_K), 1, 2).reshape(steps, 1, TOP_K * tm)
    emit_h = next_mods is not None
    row = pl.BlockSpec((tm, d), lambda s: (s, 0))
    vec = pl.BlockSpec((1, d), lambda s: (0, 0))
    smem = lambda f: pl.BlockSpec((None, 1, TOP_K * tm), f, memory_space=pltpu.SMEM)
    in_specs = [smem(lambda s: (s, 0, 0)), smem(lambda s: (jnp.minimum(s + 1, steps - 1), 0, 0)), row,
                pl.BlockSpec((tm, LANES), lambda s: (s, 0)),
                pl.BlockSpec((1, 1, d), _group_map(rows_per_group, tm, g0, gate_chunk)), vec, vec]
    args = [idx, idx, x, wts, mods, ln_g.reshape(1, d), ln_b.reshape(1, d)]
    out_shape = [jax.ShapeDtypeStruct((m, d), F32)]
    out_specs = [row]
    if emit_h:
        in_specs += [pl.BlockSpec((1, 1, d), _group_map(rows_per_group, tm, g0, scale_chunk)),
                     pl.BlockSpec((1, 1, d), _group_map(rows_per_group, tm, g0, shift_chunk))]
        args += [next_mods, next_mods]
        out_shape.append(jax.ShapeDtypeStruct((m, d), BF16))
        out_specs.append(row)
    in_specs.append(pl.BlockSpec(memory_space=pl.ANY))
    args.append(y_sorted)
    out = pl.pallas_call(
        functools.partial(_moe_combine_kernel, alpha=alpha, emit_h=emit_h),
        out_shape=out_shape, grid=(steps,), in_specs=in_specs, out_specs=out_specs,
        scratch_shapes=[pltpu.VMEM((2, TOP_K * tm, d), F32), pltpu.SemaphoreType.DMA((2,))],
        compiler_params=_cparams(("arbitrary",)), name=name,
    )(*args)
    return out if emit_h else (out[0], None)


def _moe_ln(x, h, route, mods, w1, w3, w2, ln_g, ln_b, *, alpha, rows_per_group, g0, next_mods, name):
    n_experts = w1.shape[0]
    expert_idx = route[:, TOP_K:2 * TOP_K].astype(jnp.int32)
    dest, row_src, tile_expert, n_used = _moe_plan(expert_idx, n_experts, MOE_ROW_TILE)
    xs = _moe_gather(h, row_src)
    up = _moe_grouped(xs, (w1, w3), tile_expert, n_used, tn=_tile(w1.shape[-1], (512, 256, 128)), out_dtype=BF16,
                      name=name + "_up")
    ys = _moe_grouped(up, (w2,), tile_expert, n_used, tn=_tile(w2.shape[-1], (1024, 512, 256, 128)), out_dtype=F32,
                      name=name + "_dn")
    return _moe_combine_ln(x, ys, dest, route, mods, ln_g, ln_b, alpha=alpha, rows_per_group=rows_per_group, g0=g0,
                           gate_chunk=5, next_mods=next_mods, scale_chunk=1, shift_chunk=0, name=name + "_ln")


def kernel(x, c, ctx, c_ctx, w_ada, b_ada, w_in, b_in, attn_q_norm, attn_k_norm, gla_w2, gla_b2, gla_norm,
           s5_lambda_re, s5_lambda_im, s5_log_step, s5_b_re, s5_b_im, s5_c_re, s5_c_im, s5_d, s5_glu_w, s5_glu_b,
           w_branch, w_out, ln1_g, ln1_b, ln2_g, ln2_b, ffn_w1, ffn_w3, ffn_w2, moe_router, moe_w1, moe_w3, moe_w2):
    batch, t, d = x.shape
    tc = ctx.shape[1]
    depth = w_ada.shape[0]
    assert batch + 1 <= MOD_ROWS and t % CHUNK == 0 and tc % CHUNK == 0 and t % GRID_W == 0
    lay = _Layout(d)
    assert lay.in_width == w_in.shape[-1]
    alpha = (2 * depth) ** 0.25
    rope = _rope_tables(t)
    log_gamma = jnp.log1p(-jnp.exp2(-5.0 - jnp.arange(REC_HEADS, dtype=F32)))
    lg_tab = jnp.broadcast_to(log_gamma[:, None, None], (REC_HEADS, 1, HEAD_DIM))
    n_blk = (s5_lambda_re.shape[2] * S5_STATE) // S5_LANE_BLOCK

    xl = x.reshape(batch * t, d)
    xc = ctx.reshape(batch * tc, d)
    cvec = jnp.concatenate([c, c_ctx[None], jnp.zeros((MOD_ROWS - batch - 1, d), F32)], axis=0)

    def mods_of(l):
        return _matmul(cvec, w_ada, b_ada[l], a_silu=True, w_index=l, name=f"ada{l}").reshape(MOD_ROWS, 1, 6 * d)

    mods = mods_of(0)
    hl = _modulate(xl, mods, rows_per_group=t, g0=0, scale_chunk=1, shift_chunk=0, name="mod_lat")
    hc = _modulate(xc, mods, rows_per_group=0, g0=batch, scale_chunk=1, shift_chunk=0, name="mod_ctx")

    for l in range(depth):
        ctx_out = l < depth - 1
        w_in_p = _permute_w_in(w_in, l, lay)
        b_in_p = lay.permute(b_in[l])
        zl = _matmul(hl, w_in_p, b_in_p, out_dtype=BF16, w_rows=True, name=f"in_lat{l}")
        zc = _matmul(hc, w_in_p, b_in_p, n_out=None if ctx_out else lay.small, out_dtype=BF16, w_rows=True,
                     name=f"in_ctx{l}")

        y_att = _attention(zl, zc, zl, lay, attn_q_norm[l], attn_k_norm[l], rope, batch=batch, tq_rows=t, tc=tc, t=t,
                           name=f"attn_lat{l}")
        yc_att = (_attention(zc, zc, None, lay, attn_q_norm[l], attn_k_norm[l], None, batch=batch, tq_rows=tc, tc=tc,
                             t=t, name=f"attn_ctx{l}") if ctx_out else None)

        w2h = gla_w2[l].reshape(2, GLA_RANK, REC_HEADS, HEAD_DIM)
        w2cat = jnp.zeros((REC_HEADS, LANES, 2 * HEAD_DIM), F32)
        for r in range(2):
            w2cat = w2cat.at[:, r * GLA_RANK:(r + 1) * GLA_RANK, r * HEAD_DIM:(r + 1) * HEAD_DIM].set(
                jnp.transpose(w2h[r], (1, 0, 2)))
        b2cat = jnp.transpose(gla_b2[l].reshape(2, REC_HEADS, HEAD_DIM), (1, 0, 2)).reshape(REC_HEADS, 1, 2 * HEAD_DIM)
        y_gla, yc_gla = _recurrence(zl, zc, lay, "gla", (w2cat.astype(BF16), b2cat, gla_norm[l].reshape(1, -1)),
                                    batch=batch, t=t, tc=tc, ctx_out=ctx_out, name=f"gla{l}")
        y_ret, yc_ret = _recurrence(zl, zc, lay, "ret", (lg_tab, rope), batch=batch, t=t, tc=tc, ctx_out=ctx_out,
                                    name=f"ret{l}")

        mw = lay.src["s5_u"][1]
        u0 = lay.off["s5_u"]
        u_tm = jnp.concatenate([jnp.swapaxes(zc[:, u0:u0 + mw].reshape(batch, tc, mw), 0, 1),
                                jnp.swapaxes(zl[:, u0:u0 + mw].reshape(batch, t, mw), 0, 1)], axis=0)
        u_tm = u_tm.reshape((tc + t) * batch, mw)
        tabs = _s5_prep(s5_lambda_re[l], s5_lambda_im[l], s5_log_step[l], s5_b_re[l], s5_b_im[l], batch)
        c_re_bd = _block_diag(jnp.swapaxes(s5_c_re[l], -1, -2), n_blk).astype(BF16)
        c_im_bd = _block_diag(jnp.swapaxes(s5_c_im[l], -1, -2), n_blk).astype(BF16)
        yf, yb = _s5_scan(u_tm, tabs, c_re_bd, c_im_bd, batch=batch, t=t, tc=tc)
        y_s5_tm = _s5_glu(u_tm, yf, yb, s5_d[l], s5_glu_w[l].astype(BF16), s5_glu_b[l])
        y_s5_bm = jnp.swapaxes(y_s5_tm.reshape(tc + t, batch, mw), 0, 1)
        y_s5 = y_s5_bm[:, tc:].reshape(batch * t, mw)
        yc_s5 = y_s5_bm[:, :tc].reshape(batch * tc, mw)

        wb = w_branch[l].astype(BF16)
        wo = w_out[l].astype(BF16)
        i = l // 2
        is_moe = l % 2 == 1
        ln1_extra = dict(router=moe_router[i]) if is_moe else {}
        ml = _matmul(_merge((y_gla, y_att, y_ret, y_s5), zl, lay, wb, name=f"merge_lat{l}"), wo, out_dtype=BF16,
                     name=f"out_lat{l}")
        xl, hl, *route_l = _ln_res(xl, ml, mods, ln1_g[l], ln1_b[l], alpha=alpha, rows_per_group=t, g0=0, gate_chunk=2,
                                   next_mods=mods, scale_chunk=4, shift_chunk=3, name=f"ln1_lat{l}", **ln1_extra)
        if ctx_out:
            mc = _matmul(_merge((yc_gla, yc_att, yc_ret, yc_s5), zc, lay, wb, name=f"merge_ctx{l}"), wo,
                         out_dtype=BF16, name=f"out_ctx{l}")
            xc, hc, *route_c = _ln_res(xc, mc, mods, ln1_g[l], ln1_b[l], alpha=alpha, rows_per_group=0, g0=batch,
                                       gate_chunk=2, next_mods=mods, scale_chunk=4, shift_chunk=3,
                                       name=f"ln1_ctx{l}", **ln1_extra)

        next_mods = mods_of(l + 1) if l + 1 < depth else None
        if is_moe:
            w1, w3, w2 = moe_w1[i], moe_w3[i], moe_w2[i]
            xl, hl = _moe_ln(xl, hl, route_l[0], mods, w1, w3, w2, ln2_g[l], ln2_b[l], alpha=alpha,
                             rows_per_group=t, g0=0, next_mods=next_mods, name=f"moe_lat{l}")
            if ctx_out:
                xc, hc = _moe_ln(xc, hc, route_c[0], mods, w1, w3, w2, ln2_g[l], ln2_b[l], alpha=alpha,
                                 rows_per_group=0, g0=batch, next_mods=next_mods, name=f"moe_ctx{l}")
        else:
            w1, w3, w2 = ffn_w1[i].astype(BF16), ffn_w3[i].astype(BF16), ffn_w2[i].astype(BF16)
            fl = _matmul(_ffn_up(hl, w1, w3, name=f"ffn_up_lat{l}"), w2, out_dtype=BF16, name=f"ffn_dn_lat{l}")
            xl, hl = _ln_res(xl, fl, mods, ln2_g[l], ln2_b[l], alpha=alpha, rows_per_group=t, g0=0, gate_chunk=5,
                             next_mods=next_mods, scale_chunk=1, shift_chunk=0, name=f"ln2_lat{l}")
            if ctx_out:
                fc = _matmul(_ffn_up(hc, w1, w3, name=f"ffn_up_ctx{l}"), w2, out_dtype=BF16, name=f"ffn_dn_ctx{l}")
                xc, hc = _ln_res(xc, fc, mods, ln2_g[l], ln2_b[l], alpha=alpha, rows_per_group=0, g0=batch,
                                 gate_chunk=5, next_mods=next_mods, scale_chunk=1, shift_chunk=0, name=f"ln2_ctx{l}")
        mods = next_mods

    return xl.reshape(batch, t, d)
```

```python
import functools
import math

import jax
import jax.numpy as jnp
from jax import lax
from jax.experimental import pallas as pl
from jax.experimental.pallas import tpu as pltpu

F32 = jnp.float32
BF16 = jnp.bfloat16

V7X_VMEM_BYTES = 64 * 1024 * 1024
VMEM_LIMIT = V7X_VMEM_BYTES - 8 * 1024 * 1024
TILE_BUDGET = 52 * 1024 * 1024
LANES = 128

HEAD_DIM = 128
ATT_KV_HEADS = 2
REC_HEADS = 4
GLA_RANK = 16
GLA_GATE_NORM = 16.0
CHUNK = 64
REC_BLOCK = 256
S5_STATE = 64
N_BRANCH = 4
TOP_K = 2
ROPE_THETA = 10000.0
GRID_W = 64
EPS = 1e-6
MOD_ROWS = 16


def _cparams(sem):
    return pltpu.CompilerParams(dimension_semantics=sem, vmem_limit_bytes=VMEM_LIMIT)


def _tile(n, prefs):
    for t in prefs:
        if n % t == 0:
            return t
    return n


def _mm_tiles(m, k, n, a_bytes, w_bytes, o_bytes, n_w=1):
    wide = tuple((tm, tn) for tm in (64, 32, 16, 8) for tn in (1024, 512, 256, 128))
    for tm, tn in ((1024, 1024), (1024, 512), (512, 1024), (512, 512), (256, 512), (256, 256), (128, 256),
                   (128, 128)) + wide:
        if m % tm or n % tn:
            continue
        need = 2 * (tm * k * a_bytes + n_w * k * tn * w_bytes + tm * tn * o_bytes) + n_w * tm * tn * 4
        if need <= TILE_BUDGET:
            return tm, tn
    raise ValueError(f"no matmul tiling for {(m, k, n)}")


class _Layout:
    def __init__(self, d):
        mw = d // 4
        hk = REC_HEADS * HEAD_DIM
        kv = ATT_KV_HEADS * HEAD_DIM
        src_sizes = (mw, kv, kv, hk, hk, mw, mw, 2 * GLA_RANK, hk, hk, mw, mw, mw, N_BRANCH * d)
        names = ("att_q", "att_k", "att_v", "gla_q", "gla_k", "gla_v", "gla_g", "gla_lr",
                 "ret_q", "ret_k", "ret_v", "ret_g", "s5_u", "gates")
        self.src = {}
        off = 0
        for nm, sz in zip(names, src_sizes):
            self.src[nm] = (off, sz)
            off += sz
        self.in_width = off
        order = tuple(nm for nm in names if nm not in ("gla_lr", "gates")) + ("gla_lr",)
        self.off = {}
        self.plan = []
        pos = 0
        for nm in order:
            width = max(self.src[nm][1], LANES)
            self.off[nm] = pos
            self.plan.append((nm, self.src[nm][1]))
            if width > self.src[nm][1]:
                self.plan.append((None, width - self.src[nm][1]))
            pos += width
        small = -(-pos // 1024) * 1024
        if small > pos:
            self.plan.append((None, small - pos))
        self.small = small
        self.off["gates"] = small
        self.plan.append(("gates", self.src["gates"][1]))
        self.width = small + self.src["gates"][1]

    def blk(self, name, width):
        off = self.off[name]
        assert off % width == 0, (name, off, width)
        return off // width

    def permute(self, w):
        parts = []
        for nm, sz in self.plan:
            if nm is None:
                parts.append(jnp.zeros(w.shape[:-1] + (sz,), w.dtype))
            else:
                o = self.src[nm][0]
                parts.append(w[..., o:o + sz])
        return jnp.concatenate(parts, axis=-1)


W_IN_ROW_TILE = 512


def _permute_w_in_kernel(wt_hbm, o_ref, buf, sem, *, layer, j_plain, j_lr, lr_off, lr_w):
    s, n_steps = pl.program_id(0), pl.num_programs(0)
    tn = o_ref.shape[0]

    def tile_copy(j, slot):
        shifted = jnp.where(j < j_lr, j, j - 1) * tn + lr_w
        r0 = jnp.where(j < j_plain, j * tn, jnp.where(j == j_lr, lr_off, shifted))
        return pltpu.make_async_copy(wt_hbm.at[layer, pl.ds(pl.multiple_of(r0, 8), tn), :], buf.at[slot], sem.at[slot])

    @pl.when(s == 0)
    def _():
        tile_copy(s, 0).start()

    @pl.when(s + 1 < n_steps)
    def _():
        tile_copy(s + 1, (s + 1) % 2).start()

    slot = s % 2
    tile_copy(s, slot).wait()
    row = lax.broadcasted_iota(jnp.int32, o_ref.shape, 0)
    keep = row < jnp.where(s == j_lr, lr_w, tn)
    o_ref[...] = jnp.where(keep, buf[slot], 0.0).astype(BF16)


def _permute_w_in(w_in, l, lay):
    wt = jnp.swapaxes(w_in, 1, 2)
    d = wt.shape[2]
    tn = W_IN_ROW_TILE
    lr_off, lr_w = lay.src["gla_lr"]
    g_off = lay.src["gates"][0]
    j_plain, j_lr = lr_off // tn, lay.off["gla_lr"] // tn
    assert lr_off % tn == 0 and lay.off["gla_lr"] % tn == 0 and lay.small - lay.off["gla_lr"] == tn
    assert lay.off["gla_lr"] - lr_off == g_off - (lr_off + lr_w) and lay.small - g_off == tn - lr_w
    assert lay.width % tn == 0 and lr_w % 8 == 0 and lr_off + tn <= lay.in_width
    return pl.pallas_call(
        functools.partial(_permute_w_in_kernel, layer=l, j_plain=j_plain, j_lr=j_lr, lr_off=lr_off, lr_w=lr_w),
        out_shape=jax.ShapeDtypeStruct((lay.width, d), BF16), grid=(lay.width // tn,),
        in_specs=[pl.BlockSpec(memory_space=pl.ANY)],
        out_specs=pl.BlockSpec((tn, d), lambda s: (s, 0)),
        scratch_shapes=[pltpu.VMEM((2, tn, d), F32), pltpu.SemaphoreType.DMA((2,))],
        compiler_params=_cparams(("arbitrary",)), name=f"w_in_perm{l}",
    )(wt)


def _mm_kernel(*refs, has_bias, a_silu, w_rows):
    if has_bias:
        a_ref, w_ref, b_ref, o_ref = refs
    else:
        a_ref, w_ref, o_ref = refs
    a = a_ref[...]
    if a_silu:
        a = a * jax.nn.sigmoid(a)
    contract = (((1,), (1 if w_rows else 0,)), ((), ()))
    acc = lax.dot_general(a.astype(BF16), w_ref[...].astype(BF16), contract, preferred_element_type=F32)
    if has_bias:
        acc = acc + b_ref[...]
    o_ref[...] = acc.astype(o_ref.dtype)


def _matmul(a, w, bias=None, *, n_out=None, out_dtype=F32, a_silu=False, w_index=None, w_rows=False, name):
    m, k = a.shape
    n = n_out or w.shape[0 if w_rows else -1]
    tm, tn = _mm_tiles(m, k, n, a.dtype.itemsize, w.dtype.itemsize, jnp.dtype(out_dtype).itemsize)
    if w_rows:
        w_spec = pl.BlockSpec((tn, k), lambda i, j: (j, 0))
    elif w_index is None:
        w_spec = pl.BlockSpec((k, tn), lambda i, j: (0, j))
    else:
        w_spec = pl.BlockSpec((None, k, tn), lambda i, j: (w_index, 0, j))
    in_specs = [pl.BlockSpec((tm, k), lambda i, j: (i, 0)), w_spec]
    args = [a, w]
    if bias is not None:
        in_specs.append(pl.BlockSpec((1, tn), lambda i, j: (0, j)))
        args.append(bias.reshape(1, -1))
    return pl.pallas_call(
        functools.partial(_mm_kernel, has_bias=bias is not None, a_silu=a_silu, w_rows=w_rows),
        out_shape=jax.ShapeDtypeStruct((m, n), out_dtype),
        grid=(m // tm, n // tn),
        in_specs=in_specs,
        out_specs=pl.BlockSpec((tm, tn), lambda i, j: (i, j)),
        compiler_params=_cparams(("parallel", "arbitrary")),
        name=name,
    )(*args)


def _group_map(rows_per_group, tm, g0, chunk):
    tiles = rows_per_group // tm if rows_per_group else None

    def index_map(i):
        return ((i // tiles + g0) if tiles else g0, 0, chunk)
    return index_map


def _modulate_kernel(x_ref, sc_ref, sh_ref, o_ref):
    o_ref[...] = (x_ref[...] * (1.0 + sc_ref[0]) + sh_ref[0]).astype(o_ref.dtype)


def _modulate(x, mods, *, rows_per_group, g0, scale_chunk, shift_chunk, name):
    m, d = x.shape
    tm = _tile(rows_per_group or m, (512, 256, 128, 64, 32, 16, 8))
    return pl.pallas_call(
        _modulate_kernel,
        out_shape=jax.ShapeDtypeStruct((m, d), BF16),
        grid=(m // tm,),
        in_specs=[pl.BlockSpec((tm, d), lambda i: (i, 0)),
                  pl.BlockSpec((1, 1, d), _group_map(rows_per_group, tm, g0, scale_chunk)),
                  pl.BlockSpec((1, 1, d), _group_map(rows_per_group, tm, g0, shift_chunk))],
        out_specs=pl.BlockSpec((tm, d), lambda i: (i, 0)),
        compiler_params=_cparams(("parallel",)),
        name=name,
    )(x, mods, mods)


def _layer_norm(y, gain, bias):
    mu = jnp.mean(y, axis=-1, keepdims=True)
    dlt = y - mu
    var = jnp.mean(dlt * dlt, axis=-1, keepdims=True)
    return dlt * lax.rsqrt(var + EPS) * gain + bias


def _top2_route(h, rh_ref, rl_ref, n_experts):
    hh = h.astype(BF16)
    hl = (h - hh.astype(F32)).astype(BF16)
    logits = (jnp.dot(hh, rh_ref[...], preferred_element_type=F32) + jnp.dot(hh, rl_ref[...], preferred_element_type=F32)
              + jnp.dot(hl, rh_ref[...], preferred_element_type=F32))
    lane = lax.broadcasted_iota(jnp.int32, logits.shape, 1).astype(F32)
    neg = jnp.float32(-jnp.inf)
    x1 = jnp.where(lane < n_experts, logits, neg)
    m1 = jnp.max(x1, axis=-1, keepdims=True)
    i1 = jnp.min(jnp.where(x1 == m1, lane, float(LANES)), axis=-1, keepdims=True)
    x2 = jnp.where(lane == i1, neg, x1)
    m2 = jnp.max(x2, axis=-1, keepdims=True)
    i2 = jnp.min(jnp.where(x2 == m2, lane, float(LANES)), axis=-1, keepdims=True)
    e2 = jnp.exp(m2 - m1)
    w1 = 1.0 / (1.0 + e2)
    return (jnp.where(lane == 0.0, w1, 0.0) + jnp.where(lane == 1.0, e2 * w1, 0.0)
            + jnp.where(lane == 2.0, i1, 0.0) + jnp.where(lane == 3.0, i2, 0.0))


HI16 = 0xFFFF0000


def _pack_bf16_pairs(h):
    half = h.shape[1] // 2
    bits = lax.bitcast_convert_type(h.astype(BF16).astype(F32), jnp.uint32)
    return jnp.right_shift(bits[:, :half], jnp.uint32(16)) | (bits[:, half:] & jnp.uint32(HI16))


def _unpack_bf16_pairs(w):
    lo = lax.bitcast_convert_type(jnp.left_shift(w, jnp.uint32(16)), F32)
    hi = lax.bitcast_convert_type(w & jnp.uint32(HI16), F32)
    return jnp.concatenate([lo, hi], axis=1).astype(BF16)


def _ln_res_kernel(*refs, alpha, emit_h, n_experts):
    it = iter(refs)
    x_ref, m_ref, gate_ref, g_ref, b_ref = (next(it) for _ in range(5))
    if emit_h:
        sc_ref, sh_ref = next(it), next(it)
    if n_experts:
        rh_ref, rl_ref = next(it), next(it)
    xo_ref = next(it)
    xn = _layer_norm(alpha * x_ref[...] + gate_ref[0] * m_ref[...].astype(F32), g_ref[...], b_ref[...])
    xo_ref[...] = xn
    if emit_h:
        ho_ref = next(it)
        h = xn * (1.0 + sc_ref[0]) + sh_ref[0]
        if n_experts:
            ho_ref[...] = _pack_bf16_pairs(h)
            next(it)[...] = _top2_route(h, rh_ref, rl_ref, n_experts)
        else:
            ho_ref[...] = h.astype(ho_ref.dtype)


def _ln_res(x, mres, mods, ln_g, ln_b, *, alpha, rows_per_group, g0, gate_chunk, next_mods=None,
            scale_chunk=None, shift_chunk=None, router=None, name):
    m, d = x.shape
    tm = _tile(rows_per_group or m, (256, 128, 64, 32, 16, 8))
    emit_h = next_mods is not None
    assert emit_h or router is None
    row = pl.BlockSpec((tm, d), lambda i: (i, 0))
    vec = pl.BlockSpec((1, d), lambda i: (0, 0))
    in_specs = [row, row, pl.BlockSpec((1, 1, d), _group_map(rows_per_group, tm, g0, gate_chunk)), vec, vec]
    args = [x, mres, mods, ln_g.reshape(1, d), ln_b.reshape(1, d)]
    out_shape = [jax.ShapeDtypeStruct((m, d), F32)]
    out_specs = [row]
    if emit_h:
        in_specs += [pl.BlockSpec((1, 1, d), _group_map(rows_per_group, tm, g0, scale_chunk)),
                     pl.BlockSpec((1, 1, d), _group_map(rows_per_group, tm, g0, shift_chunk))]
        args += [next_mods, next_mods]
        if router is None:
            out_shape.append(jax.ShapeDtypeStruct((m, d), BF16))
            out_specs.append(row)
        else:
            out_shape.append(jax.ShapeDtypeStruct((m, d // 2), jnp.uint32))
            out_specs.append(pl.BlockSpec((tm, d // 2), lambda i: (i, 0)))
    n_experts = 0
    if router is not None:
        n_experts = router.shape[-1]
        rp = jnp.zeros((d, LANES), F32).at[:, :n_experts].set(router)
        rh = rp.astype(BF16)
        mat = pl.BlockSpec((d, LANES), lambda i: (0, 0))
        in_specs += [mat, mat]
        args += [rh, (rp - rh.astype(F32)).astype(BF16)]
        out_shape.append(jax.ShapeDtypeStruct((m, LANES), F32))
        out_specs.append(pl.BlockSpec((tm, LANES), lambda i: (i, 0)))
    out = pl.pallas_call(
        functools.partial(_ln_res_kernel, alpha=alpha, emit_h=emit_h, n_experts=n_experts),
        out_shape=out_shape, grid=(m // tm,), in_specs=in_specs, out_specs=out_specs,
        compiler_params=_cparams(("parallel",)), name=name,
    )(*args)
    return out if emit_h else (out[0], None)


def _rope_tables(t):
    nf = HEAD_DIM // 4
    rows = jnp.repeat(jnp.arange(t // GRID_W, dtype=F32), GRID_W)
    cols = (jnp.arange(t) % GRID_W).astype(F32)
    inv = ROPE_THETA ** (-jnp.arange(nf, dtype=F32) / nf)
    ar, ac = rows[:, None] * inv, cols[:, None] * inv
    cr, sr, cc, sc = jnp.cos(ar), jnp.sin(ar), jnp.cos(ac), jnp.sin(ac)
    zero = jnp.zeros_like(sr)
    cos = jnp.concatenate([cr, cr, cc, cc], axis=-1)
    sin_a = jnp.concatenate([-sr, zero, -sc, zero], axis=-1)
    sin_b = jnp.concatenate([zero, sr, zero, sc], axis=-1)
    return cos, sin_a, sin_b


def _rope(x, cos, sin_a, sin_b):
    return x * cos + pltpu.roll(x, 96, 1) * sin_a + pltpu.roll(x, 32, 1) * sin_b


def _rms(x, gain):
    return x * lax.rsqrt(jnp.mean(x * x, axis=-1, keepdims=True) + EPS) * gain


def _attn_kernel(*refs, n_rep, tc, t, has_lat):
    it = iter(refs)
    q_ref, kc_ref, vc_ref, qg_ref, kg_ref = (next(it) for _ in range(5))
    if has_lat:
        kl_ref, vl_ref, cosk, sak, sbk, cosq, saq, sbq = (next(it) for _ in range(8))
    o_ref, k_scr, v_scr = next(it), next(it), next(it)

    @pl.when(pl.program_id(2) == 0)
    def _():
        v_scr[:, HEAD_DIM:] = jnp.ones((v_scr.shape[0], HEAD_DIM), BF16)
        k_scr[0:tc, :] = _rms(kc_ref[...].astype(F32), kg_ref[...]).astype(BF16)
        v_scr[0:tc, 0:HEAD_DIM] = vc_ref[...].astype(BF16)
        if has_lat:
            kl = _rope(_rms(kl_ref[...].astype(F32), kg_ref[...]), cosk[...], sak[...], sbk[...])
            k_scr[tc:tc + t, :] = kl.astype(BF16)
            v_scr[tc:tc + t, 0:HEAD_DIM] = vl_ref[...].astype(BF16)

    scale = HEAD_DIM ** -0.5
    keys = k_scr[...]
    vals = v_scr[...]
    qs = []
    for r in range(n_rep):
        q = _rms(q_ref[:, r * HEAD_DIM:(r + 1) * HEAD_DIM].astype(F32), qg_ref[...])
        if has_lat:
            q = _rope(q, cosq[...], saq[...], sbq[...])
        qs.append((q * scale).astype(BF16))
    tq = q_ref.shape[0]
    s_all = lax.dot_general(jnp.concatenate(qs, axis=0), keys, (((1,), (1,)), ((), ())), preferred_element_type=F32)
    for r in range(n_rep):
        s = s_all[r * tq:(r + 1) * tq]
        e = jnp.exp(s - jnp.max(s, axis=-1, keepdims=True)).astype(BF16)
        ov = jnp.dot(e, vals, preferred_element_type=F32)
        o_ref[:, r * HEAD_DIM:(r + 1) * HEAD_DIM] = (ov[:, :HEAD_DIM] / ov[:, HEAD_DIM:]).astype(o_ref.dtype)


def _attention(zq, zc, zl, lay, q_norm, k_norm, rope, *, batch, tq_rows, tc, t, name):
    has_lat = zl is not None
    n_rep = (lay.src["att_q"][1] // HEAD_DIM) // ATT_KV_HEADS
    qw = n_rep * HEAD_DIM
    tq = _tile(tq_rows, (256, 128, 64, 32, 16, 8))
    nq = tq_rows // tq
    qb, kb, vb = lay.blk("att_q", qw), lay.blk("att_k", HEAD_DIM), lay.blk("att_v", HEAD_DIM)
    vec = pl.BlockSpec((1, HEAD_DIM), lambda b, g, i: (0, 0))
    in_specs = [pl.BlockSpec((tq, qw), lambda b, g, i: (b * nq + i, qb + g)),
                pl.BlockSpec((tc, HEAD_DIM), lambda b, g, i: (b, kb + g)),
                pl.BlockSpec((tc, HEAD_DIM), lambda b, g, i: (b, vb + g)), vec, vec]
    args = [zq, zc, zc, q_norm.reshape(1, -1), k_norm.reshape(1, -1)]
    n_keys = tc
    if has_lat:
        full = pl.BlockSpec((t, HEAD_DIM), lambda b, g, i: (0, 0))
        part = pl.BlockSpec((tq, HEAD_DIM), lambda b, g, i: (i, 0))
        in_specs += [pl.BlockSpec((t, HEAD_DIM), lambda b, g, i: (b, kb + g)),
                     pl.BlockSpec((t, HEAD_DIM), lambda b, g, i: (b, vb + g)), full, full, full, part, part, part]
        args += [zl, zl, *rope, *rope]
        n_keys = tc + t
    return pl.pallas_call(
        functools.partial(_attn_kernel, n_rep=n_rep, tc=tc, t=t, has_lat=has_lat),
        out_shape=jax.ShapeDtypeStruct((batch * tq_rows, lay.src["att_q"][1]), BF16),
        grid=(batch, ATT_KV_HEADS, nq),
        in_specs=in_specs,
        out_specs=pl.BlockSpec((tq, qw), lambda b, g, i: (b * nq + i, g)),
        scratch_shapes=[pltpu.VMEM((n_keys, HEAD_DIM), BF16), pltpu.VMEM((n_keys, 2 * HEAD_DIM), BF16)],
        compiler_params=_cparams(("parallel", "parallel", "arbitrary")),
        name=name,
    )(*args)


def _place(parts):
    n = len(parts)
    zero = jnp.zeros_like(parts[0])
    return jnp.concatenate([jnp.concatenate([p if j == c else zero for j in range(n)], axis=1)
                            for c, p in enumerate(parts)], axis=0)


def _rec_block(b, q_s, k_s, v_s, d, *, forward, want_out, blk):
    nch = blk // CHUNK
    rows = slice(b * blk, (b + 1) * blk)
    q, k, g, v = q_s[rows, :], k_s[rows, :], d["g"][rows, :], v_s[rows, :]
    shift = CHUNK.bit_length() - 1
    ri = lax.broadcasted_iota(jnp.int32, (blk, blk), 0)
    ci = lax.broadcasted_iota(jnp.int32, (blk, blk), 1)
    same = jnp.right_shift(ri, shift) == jnp.right_shift(ci, shift)
    row_in_chunk = lax.broadcasted_iota(jnp.int32, g.shape, 0) & (CHUNK - 1)
    cum = g
    step = 1
    while step < CHUNK:
        cum = cum + jnp.where(row_in_chunk >= step, pltpu.roll(cum, step, 0), 0.0)
        step *= 2
    qe, ke, qd, kd = [], [], [], []
    for c in range(nch):
        cs = slice(c * CHUNK, (c + 1) * CHUNK)
        if forward:
            a = cum[cs]
            ref, tot = a[CHUNK // 2:CHUNK // 2 + 1], a[CHUNK - 1:CHUNK]
        else:
            a = cum[cs][CHUNK - 1:CHUNK] - cum[cs] + g[cs]
            ref, tot = a[CHUNK // 2 - 1:CHUNK // 2], a[0:1]
        qe.append(q[cs] * jnp.exp(a - ref))
        ke.append(k[cs] * jnp.exp(ref - a))
        qd.append((q[cs] * jnp.exp(a)).astype(BF16))
        kd.append((k[cs] * jnp.exp(tot - a)).astype(BF16))
        d["dec"][(b * nch + c) * 8:(b * nch + c + 1) * 8, :] = jnp.broadcast_to(jnp.exp(tot), (8, HEAD_DIM))
    if want_out:
        s = lax.dot_general(jnp.concatenate(qe, axis=0).astype(BF16), jnp.concatenate(ke, axis=0).astype(BF16),
                            (((1,), (1,)), ((), ())), preferred_element_type=F32)
        if forward:
            mask = jnp.where(same, ri - ci, -1) >= 0
        else:
            mask = jnp.where(same, ci - ri, 0) > 0
        d["o"][rows, :] = jnp.dot(jnp.where(mask, s, 0.0).astype(BF16), v, preferred_element_type=F32)
        d["qd"][rows, :] = _place(qd)
    v_t = jnp.transpose(v.astype(F32)).astype(BF16)
    d["dst"][b] = jnp.dot(v_t, _place(kd), preferred_element_type=F32)


_REC_DIR_SCRATCH = ("o", "dst", "stc", "qd", "dec")


def _rec_kernel(*refs, kind, ctx_out, t, tc, blk):
    it = iter(refs)
    ql, kl, vl, gl, qc, kc, vc, gc = (next(it) for _ in range(8))
    if kind == "gla":
        lrl, lrc, w2, b2, norm = (next(it) for _ in range(5))
    else:
        lg, cos, sa, sb = (next(it) for _ in range(4))
    yl = next(it)
    yc = next(it) if ctx_out else None
    q_s, k_s, v_s, gf_s, gb_s = (next(it) for _ in range(5))
    dirs = [dict(zip(("g",) + _REC_DIR_SCRATCH, (g_s,) + tuple(next(it) for _ in _REC_DIR_SCRATCH)))
            for g_s in (gf_s, gb_s)]
    scale = HEAD_DIM ** -0.5
    nch, nb, nbc = blk // CHUNK, (tc + t) // blk, tc // blk
    ctx_rows, lat_rows = slice(0, tc), slice(tc, tc + t)

    for rows, q_in, k_in, v_in, lr_in, roped in ((ctx_rows, qc, kc, vc, lrc if kind == "gla" else None, False),
                                                 (lat_rows, ql, kl, vl, lrl if kind == "gla" else None, True)):
        q, k = q_in[...].astype(F32), k_in[...].astype(F32)
        if kind == "gla":
            logits = jnp.dot(lr_in[...].astype(BF16), w2[...], preferred_element_type=F32) + b2[...]
            lsig = (jnp.minimum(logits, 0.0) - jnp.log1p(jnp.exp(-jnp.abs(logits)))) * (1.0 / GLA_GATE_NORM)
            gf_s[rows, :] = lsig[:, :HEAD_DIM]
            gb_s[rows, :] = lsig[:, HEAD_DIM:]
            q = q * scale
        else:
            gf_s[rows, :] = jnp.broadcast_to(lg[...], q.shape)
            gb_s[rows, :] = jnp.broadcast_to(lg[...], q.shape)
            if roped:
                q = _rope(q, cos[...], sa[...], sb[...])
                k = _rope(k, cos[...], sa[...], sb[...])
            k = k * scale
        q_s[rows, :] = q
        k_s[rows, :] = k
        v_s[rows, :] = v_in[...].astype(BF16)

    for b in range(nb):
        for d, fwd in zip(dirs, (True, False)):
            _rec_block(b, q_s, k_s, v_s, d, forward=fwd, want_out=ctx_out or b >= nbc, blk=blk)

    for d, fwd in zip(dirs, (True, False)):
        order = range(nb) if fwd else list(range(nbc - 1, -1, -1)) + list(range(nb - 1, nbc - 1, -1))
        st = jnp.zeros(d["dst"].shape[1:2] + (HEAD_DIM,), F32)
        for b in order:
            for c in (range(nch) if fwd else range(nch - 1, -1, -1)):
                lanes = slice(c * HEAD_DIM, (c + 1) * HEAD_DIM)
                d["stc"][b, :, lanes] = st.astype(BF16)
                st = st * d["dec"][(b * nch + c) * 8:(b * nch + c) * 8 + 1, :] + d["dst"][b, :, lanes]

    for b in range(0 if ctx_out else nbc, nb):
        rows = slice(b * blk, (b + 1) * blk)
        for d in dirs:
            d["o"][rows, :] += lax.dot_general(d["qd"][rows, :], d["stc"][b], (((1,), (1,)), ((), ())),
                                               preferred_element_type=F32)

    def finish(rows, g_ref, y_ref):
        o = dirs[0]["o"][rows, :] + dirs[1]["o"][rows, :]
        if kind == "gla":
            o = _rms(o, norm[...])
        else:
            mu = jnp.mean(o, axis=-1, keepdims=True)
            dlt = o - mu
            o = dlt * lax.rsqrt(jnp.mean(dlt * dlt, axis=-1, keepdims=True) + EPS)
        g = g_ref[...].astype(F32)
        y_ref[...] = (o * (g * jax.nn.sigmoid(g))).astype(y_ref.dtype)

    finish(lat_rows, gl, yl)
    if ctx_out:
        finish(ctx_rows, gc, yc)


def _recurrence(zl, zc, lay, kind, params, *, batch, t, tc, ctx_out, name):
    dv = lay.src[kind + "_v"][1] // REC_HEADS
    qb, kb = lay.blk(kind + "_q", HEAD_DIM), lay.blk(kind + "_k", HEAD_DIM)
    vb, gb = lay.blk(kind + "_v", dv), lay.blk(kind + "_g", dv)

    def blocks(rows):
        return [pl.BlockSpec((rows, HEAD_DIM), lambda b, h: (b, qb + h)),
                pl.BlockSpec((rows, HEAD_DIM), lambda b, h: (b, kb + h)),
                pl.BlockSpec((rows, dv), lambda b, h: (b, vb + h)),
                pl.BlockSpec((rows, dv), lambda b, h: (b, gb + h))]

    in_specs = blocks(t) + blocks(tc)
    args = [zl] * 4 + [zc] * 4
    if kind == "gla":
        w2cat, b2cat, norm = params
        lrb = lay.blk("gla_lr", LANES)
        in_specs += [pl.BlockSpec((t, LANES), lambda b, h: (b, lrb)), pl.BlockSpec((tc, LANES), lambda b, h: (b, lrb)),
                     pl.BlockSpec((None, LANES, 2 * HEAD_DIM), lambda b, h: (h, 0, 0)),
                     pl.BlockSpec((None, 1, 2 * HEAD_DIM), lambda b, h: (h, 0, 0)),
                     pl.BlockSpec((1, dv), lambda b, h: (0, 0))]
        args += [zl, zc, w2cat, b2cat, norm]
    else:
        lg, rope = params
        full = pl.BlockSpec((t, HEAD_DIM), lambda b, h: (0, 0))
        in_specs += [pl.BlockSpec((None, 1, HEAD_DIM), lambda b, h: (h, 0, 0)), full, full, full]
        args += [lg, *rope]
    out_shape = [jax.ShapeDtypeStruct((batch * t, REC_HEADS * dv), BF16)]
    out_specs = [pl.BlockSpec((t, dv), lambda b, h: (b, h))]
    if ctx_out:
        out_shape.append(jax.ShapeDtypeStruct((batch * tc, REC_HEADS * dv), BF16))
        out_specs.append(pl.BlockSpec((tc, dv), lambda b, h: (b, h)))

    blk = math.gcd(math.gcd(t, tc), REC_BLOCK)
    rows, nb, nch = tc + t, (tc + t) // blk, blk // CHUNK
    per_dir = {"o": pltpu.VMEM((rows, dv), F32), "dst": pltpu.VMEM((nb, dv, nch * HEAD_DIM), F32),
               "stc": pltpu.VMEM((nb, dv, nch * HEAD_DIM), BF16), "qd": pltpu.VMEM((rows, nch * HEAD_DIM), BF16),
               "dec": pltpu.VMEM((nb * nch * 8, HEAD_DIM), F32)}
    scratch = ([pltpu.VMEM((rows, HEAD_DIM), F32)] * 2 + [pltpu.VMEM((rows, dv), BF16)]
               + [pltpu.VMEM((rows, HEAD_DIM), F32)] * 2 + [per_dir[n] for n in _REC_DIR_SCRATCH] * 2)
    out = pl.pallas_call(
        functools.partial(_rec_kernel, kind=kind, ctx_out=ctx_out, t=t, tc=tc, blk=blk),
        out_shape=out_shape, grid=(batch, REC_HEADS), in_specs=in_specs, out_specs=out_specs,
        scratch_shapes=scratch,
        compiler_params=_cparams(("parallel", "parallel")), name=name,
    )(*args)
    return out[0], (out[1] if ctx_out else None)


S5_LANE_BLOCK = 1024
S5_TIME_BLOCK = 32


def _s5_prep_kernel(lr_ref, li_ref, ls_ref, br_ref, bi_ref, ar_ref, ai_ref, bbr_ref, bbi_ref):
    lam_r, lam_i = lr_ref[0], li_ref[0]
    dt = jnp.exp(ls_ref[0])
    mag = jnp.exp(lam_r * dt)
    a_r = mag * jnp.cos(lam_i * dt)
    a_i = mag * jnp.sin(lam_i * dt)
    den = lam_r * lam_r + lam_i * lam_i
    z_r = ((a_r - 1.0) * lam_r + a_i * lam_i) / den
    z_i = (a_i * lam_r - (a_r - 1.0) * lam_i) / den
    ar_ref[0] = jnp.broadcast_to(a_r, ar_ref.shape[1:])
    ai_ref[0] = jnp.broadcast_to(a_i, ai_ref.shape[1:])
    nblk, _, cols = br_ref.shape[1:]
    for j in range(nblk):
        zr = z_r[:, j * cols:(j + 1) * cols]
        zi = z_i[:, j * cols:(j + 1) * cols]
        bbr_ref[0, j] = (zr * br_ref[0, j] - zi * bi_ref[0, j]).astype(BF16)
        bbi_ref[0, j] = (zr * bi_ref[0, j] + zi * br_ref[0, j]).astype(BF16)


def _block_diag(w, n_blk):
    *lead, g, r, c = w.shape
    gl = g // n_blk
    w = w.reshape(*lead, n_blk, gl, r, c)
    eye = jnp.eye(gl, dtype=w.dtype)
    out = w[..., :, :, None, :] * eye[:, None, :, None]
    return out.reshape(*lead, n_blk, gl * r, gl * c)


def _s5_prep(lam_re, lam_im, log_step, b_re, b_im, batch):
    _, g, p = lam_re.shape
    n_blk = (g * p) // S5_LANE_BLOCK
    flat = lambda a: a.reshape(2, 1, g * p)
    ls = jnp.broadcast_to(log_step[:, :, None], (2, g, p))
    brd = _block_diag(jnp.swapaxes(b_re, -1, -2), n_blk)
    bid = _block_diag(jnp.swapaxes(b_im, -1, -2), n_blk)
    vec = pl.BlockSpec((1, 1, g * p), lambda d: (d, 0, 0))
    mat = pl.BlockSpec((1,) + brd.shape[1:], lambda d: (d, 0, 0, 0))
    tab = pl.BlockSpec((1, batch, g * p), lambda d: (d, 0, 0))
    return pl.pallas_call(
        _s5_prep_kernel,
        out_shape=[jax.ShapeDtypeStruct((2, batch, g * p), F32)] * 2 + [jax.ShapeDtypeStruct(brd.shape, BF16)] * 2,
        grid=(2,), in_specs=[vec, vec, vec, mat, mat], out_specs=[tab, tab, mat, mat],
        compiler_params=_cparams(("parallel",)), name="s5_prep",
    )(flat(lam_re), flat(lam_im), flat(ls), brd, bid)


def _s5_kernel(uf_ref, ub_ref, ar_ref, ai_ref, bbr_ref, bbi_ref, cr_ref, ci_ref, yf_ref, yb_ref,
               xr_scr, xi_scr, sr_scr, si_scr, *, batch, tb):
    n_blk, kin, _ = bbr_ref.shape[1:]
    kout = cr_ref.shape[3]

    @pl.when(pl.program_id(0) == 0)
    def _():
        sr_scr[...] = jnp.zeros_like(sr_scr)
        si_scr[...] = jnp.zeros_like(si_scr)

    dirs = tuple(enumerate(((uf_ref, yf_ref), (ub_ref, yb_ref))))
    blocks = [slice(j * S5_LANE_BLOCK, (j + 1) * S5_LANE_BLOCK) for j in range(n_blk)]
    for d, (u_ref, _) in dirs:
        u = u_ref[...].astype(BF16)
        for j, lanes in enumerate(blocks):
            uj = u[:, j * kin:(j + 1) * kin]
            xr_scr[d, :, lanes] = jnp.dot(uj, bbr_ref[d, j], preferred_element_type=F32)
            xi_scr[d, :, lanes] = jnp.dot(uj, bbi_ref[d, j], preferred_element_type=F32)
    for d, _ in dirs:
        for lanes in blocks:
            a_r, a_i = ar_ref[d, :, lanes], ai_ref[d, :, lanes]
            s_r, s_i = sr_scr[d, :, lanes], si_scr[d, :, lanes]
            for s in (range(tb) if d == 0 else range(tb - 1, -1, -1)):
                rows = slice(s * batch, (s + 1) * batch)
                s_r, s_i = (a_r * s_r - a_i * s_i + xr_scr[d, rows, lanes],
                            a_r * s_i + a_i * s_r + xi_scr[d, rows, lanes])
                xr_scr[d, rows, lanes] = s_r
                xi_scr[d, rows, lanes] = s_i
            sr_scr[d, :, lanes] = s_r
            si_scr[d, :, lanes] = s_i
    for d, (_, y_ref) in dirs:
        for j, lanes in enumerate(blocks):
            y = (jnp.dot(xr_scr[d, :, lanes].astype(BF16), cr_ref[d, j], preferred_element_type=F32)
                 - jnp.dot(xi_scr[d, :, lanes].astype(BF16), ci_ref[d, j], preferred_element_type=F32))
            y_ref[:, j * kout:(j + 1) * kout] = y.astype(y_ref.dtype)


def _s5_scan(u_tm, tabs, c_re_bd, c_im_bd, *, batch, t, tc):
    a_r, a_i, bbr, bbi = tabs
    rows, width = u_tm.shape
    tb = _tile(math.gcd(t, tc), (S5_TIME_BLOCK, 16, 8, 4, 2, 1))
    nbc, nbl = tc // tb, t // tb
    nb = nbc + nbl
    blk = tb * batch

    def bwd(i):
        return jnp.where(i < nbc, nbc - 1 - i, nb - 1 - (i - nbc))

    full4 = lambda a: pl.BlockSpec(a.shape, lambda i: (0, 0, 0, 0), pipeline_mode=pl.Buffered(1))
    full3 = lambda a: pl.BlockSpec(a.shape, lambda i: (0, 0, 0), pipeline_mode=pl.Buffered(1))
    n_state = a_r.shape[-1]
    return pl.pallas_call(
        functools.partial(_s5_kernel, batch=batch, tb=tb),
        out_shape=[jax.ShapeDtypeStruct((rows, width), BF16)] * 2,
        grid=(nb,),
        in_specs=[pl.BlockSpec((blk, width), lambda i: (i, 0)), pl.BlockSpec((blk, width), lambda i: (bwd(i), 0)),
                  full3(a_r), full3(a_i), full4(bbr), full4(bbi), full4(c_re_bd), full4(c_im_bd)],
        out_specs=[pl.BlockSpec((blk, width), lambda i: (i, 0)), pl.BlockSpec((blk, width), lambda i: (bwd(i), 0))],
        scratch_shapes=[pltpu.VMEM((2, blk, n_state), F32)] * 2 + [pltpu.VMEM((2, batch, n_state), F32)] * 2,
        compiler_params=_cparams(("arbitrary",)), name="s5_scan",
    )(u_tm, u_tm, a_r, a_i, bbr, bbi, c_re_bd, c_im_bd)


def _glu_kernel(u_ref, yf_ref, yb_ref, d_ref, w_ref, b_ref, o_ref):
    y = u_ref[...].astype(F32) * d_ref[...] + yf_ref[...].astype(F32) + yb_ref[...].astype(F32)
    z = jax.nn.gelu(y, approximate=True)
    gate = jnp.dot(z.astype(BF16), w_ref[...], preferred_element_type=F32) + b_ref[...]
    o_ref[...] = (z * jax.nn.sigmoid(gate)).astype(o_ref.dtype)


def _s5_glu(u_tm, yf, yb, d, glu_w, glu_b):
    rows, width = u_tm.shape
    tm = _tile(rows, (512, 256, 128, 64, 32, 16, 8))
    row = pl.BlockSpec((tm, width), lambda i: (i, 0))
    vec = pl.BlockSpec((1, width), lambda i: (0, 0))
    return pl.pallas_call(
        _glu_kernel, out_shape=jax.ShapeDtypeStruct((rows, width), BF16), grid=(rows // tm,),
        in_specs=[row, row, row, vec, pl.BlockSpec((width, width), lambda i: (0, 0)), vec], out_specs=row,
        compiler_params=_cparams(("parallel",)), name="s5_glu",
    )(u_tm, yf, yb, d.reshape(1, width), glu_w, glu_b.reshape(1, width))


def _merge_kernel(y0, y1, y2, y3, g0, g1, g2, g3, wb_ref, o_ref):
    acc = None
    for i, (y_ref, g_ref) in enumerate(((y0, g0), (y1, g1), (y2, g2), (y3, g3))):
        term = jax.nn.sigmoid(g_ref[...].astype(F32)) * jnp.dot(y_ref[...], wb_ref[i], preferred_element_type=F32)
        acc = term if acc is None else acc + term
    o_ref[...] = acc.astype(o_ref.dtype)


def _merge(ys, z, lay, w_branch, *, name):
    m, mw = ys[0].shape
    d = w_branch.shape[-1]
    tm = _tile(m, (1024, 512, 256, 128, 64, 32, 16))
    tn = _tile(d, (512, 256, 128))
    g_off = lay.off["gates"]
    assert g_off % tn == 0

    def gate_spec(i):
        return pl.BlockSpec((tm, tn), lambda r, c: (r, (g_off + i * d) // tn + c))

    yspec = pl.BlockSpec((tm, mw), lambda r, c: (r, 0))
    return pl.pallas_call(
        _merge_kernel, out_shape=jax.ShapeDtypeStruct((m, d), BF16), grid=(m // tm, d // tn),
        in_specs=[yspec] * N_BRANCH + [gate_spec(i) for i in range(N_BRANCH)]
        + [pl.BlockSpec((N_BRANCH, mw, tn), lambda r, c: (0, 0, c))],
        out_specs=pl.BlockSpec((tm, tn), lambda r, c: (r, c)),
        compiler_params=_cparams(("parallel", "arbitrary")), name=name,
    )(*ys, z, z, z, z, w_branch)


def _ffn_up_kernel(h_ref, w1_ref, w3_ref, o_ref):
    h = h_ref[...]
    a = jnp.dot(h, w1_ref[...], preferred_element_type=F32)
    b = jnp.dot(h, w3_ref[...], preferred_element_type=F32)
    o_ref[...] = (a * jax.nn.sigmoid(a) * b).astype(o_ref.dtype)


def _ffn_up(h, w1, w3, *, name):
    m, k = h.shape
    f = w1.shape[-1]
    tm, tn = _mm_tiles(m, k, f, 2, 2, 2, n_w=2)
    w_spec = pl.BlockSpec((k, tn), lambda i, j: (0, j))
    return pl.pallas_call(
        _ffn_up_kernel, out_shape=jax.ShapeDtypeStruct((m, f), BF16), grid=(m // tm, f // tn),
        in_specs=[pl.BlockSpec((tm, k), lambda i, j: (i, 0)), w_spec, w_spec],
        out_specs=pl.BlockSpec((tm, tn), lambda i, j: (i, j)),
        compiler_params=_cparams(("parallel", "arbitrary")), name=name,
    )(h, w1, w3)


MOE_ROW_TILE = 512
MOE_GATHER_TILE = 256
MOE_COMBINE_TILE = 128
DMA_UNROLL = 8
DMA_PRIORITIES = 2


def _moe_plan(expert_idx, n_experts, tile):
    m = expert_idx.shape[0]
    flat = expert_idx.reshape(-1)
    onehot = (flat[:, None] == jnp.arange(n_experts, dtype=jnp.int32)[None, :]).astype(jnp.int32)
    rank = jnp.take_along_axis(jnp.cumsum(onehot, axis=0) - onehot, flat[:, None], axis=1)[:, 0]
    count = jnp.sum(onehot, axis=0)
    group = ((count + tile - 1) // tile) * tile
    group_end = jnp.cumsum(group)
    dest = (group_end - group)[flat] + rank
    rows = m * TOP_K + n_experts * tile
    row_src = jnp.zeros((rows,), jnp.int32).at[dest].set(jnp.arange(m * TOP_K, dtype=jnp.int32) // TOP_K)
    n_tiles = rows // tile
    n_used = group_end[-1] // tile
    tile_id = jnp.arange(n_tiles, dtype=jnp.int32)
    owner = jnp.searchsorted(group_end, jnp.minimum(tile_id, n_used - 1) * tile, side="right").astype(jnp.int32)
    return dest.reshape(m, TOP_K), row_src, jnp.minimum(owner, n_experts - 1), n_used.reshape(1).astype(jnp.int32)


def _row_copy(src_hbm, src_row, buf, slot, dst_row, sem):
    return pltpu.make_async_copy(src_hbm.at[pl.ds(src_row, 1), :], buf.at[slot, pl.ds(dst_row, 1), :], sem.at[slot])


def _gather_rows(idx_ref, src_hbm, buf, sem, slot, n):
    def body(p, carry):
        for prio in range(DMA_PRIORITIES):
            r = DMA_PRIORITIES * p + prio
            _row_copy(src_hbm, idx_ref[0, r], buf, slot, r, sem).start(priority=prio)
        return carry
    lax.fori_loop(0, n // DMA_PRIORITIES, body, 0, unroll=DMA_UNROLL // DMA_PRIORITIES)


def _wait_rows(src_hbm, buf, sem, slot, n):
    def body(r, carry):
        _row_copy(src_hbm, 0, buf, slot, r, sem).wait()
        return carry
    lax.fori_loop(0, n, body, 0, unroll=DMA_UNROLL)


def _gather_kernel(cur_ref, nxt_ref, src_hbm, o_ref, buf, sem):
    s, n_steps = pl.program_id(0), pl.num_programs(0)
    n = buf.shape[1]

    @pl.when(s == 0)
    def _():
        _gather_rows(cur_ref, src_hbm, buf, sem, 0, n)

    @pl.when(s + 1 < n_steps)
    def _():
        _gather_rows(nxt_ref, src_hbm, buf, sem, (s + 1) % 2, n)

    slot = s % 2
    _wait_rows(src_hbm, buf, sem, slot, n)
    o_ref[...] = _unpack_bf16_pairs(buf[slot])


def _moe_gather(h, row_src):
    d = 2 * h.shape[1]
    rows = row_src.shape[0]
    tg = _tile(rows, (MOE_GATHER_TILE, 128, 64, 32, 16, 8))
    steps = rows // tg
    idx = row_src.reshape(steps, 1, tg)
    smem = lambda f: pl.BlockSpec((None, 1, tg), f, memory_space=pltpu.SMEM)
    return pl.pallas_call(
        _gather_kernel, out_shape=jax.ShapeDtypeStruct((rows, d), BF16), grid=(steps,),
        in_specs=[smem(lambda s: (s, 0, 0)), smem(lambda s: (jnp.minimum(s + 1, steps - 1), 0, 0)),
                  pl.BlockSpec(memory_space=pl.ANY)],
        out_specs=pl.BlockSpec((tg, d), lambda s: (s, 0)),
        scratch_shapes=[pltpu.VMEM((2, tg, d // 2), jnp.uint32), pltpu.SemaphoreType.DMA((2,))],
        compiler_params=_cparams(("arbitrary",)), name="moe_gather",
    )(idx, idx, h)


def _moe_grouped_kernel(te_ref, nu_ref, x_ref, *refs, n_w):
    w_refs, o_ref, w_scrs = refs[:n_w], refs[n_w], refs[n_w + 1:]
    i = pl.program_id(1)

    @pl.when(jnp.logical_or(i == 0, te_ref[i] != te_ref[jnp.maximum(i - 1, 0)]))
    def _():
        for w_ref, w_scr in zip(w_refs, w_scrs):
            w_scr[...] = w_ref[...].astype(BF16)

    @pl.when(i < nu_ref[0])
    def _():
        if n_w == 2:
            _ffn_up_kernel(x_ref, *w_scrs, o_ref)
        else:
            o_ref[...] = jnp.dot(x_ref[...], w_scrs[0][...], preferred_element_type=F32).astype(o_ref.dtype)

    @pl.when(i >= nu_ref[0])
    def _():
        o_ref[...] = jnp.zeros_like(o_ref)


def _moe_grouped(a, weights, tile_expert, n_used, *, tn, out_dtype, name):
    rows, k = a.shape
    n = weights[0].shape[-1]
    tm = MOE_ROW_TILE
    used = lambda i, nu: jnp.minimum(i, nu[0] - 1)
    w_spec = pl.BlockSpec((None, k, tn), lambda j, i, te, nu: (te[i], 0, j))
    return pl.pallas_call(
        functools.partial(_moe_grouped_kernel, n_w=len(weights)),
        out_shape=jax.ShapeDtypeStruct((rows, n), out_dtype),
        grid_spec=pltpu.PrefetchScalarGridSpec(
            num_scalar_prefetch=2, grid=(n // tn, rows // tm),
            in_specs=[pl.BlockSpec((tm, k), lambda j, i, te, nu: (used(i, nu), 0))] + [w_spec] * len(weights),
            out_specs=pl.BlockSpec((tm, tn), lambda j, i, te, nu: (i, j)),
            scratch_shapes=[pltpu.VMEM((k, tn), BF16)] * len(weights)),
        compiler_params=_cparams(("arbitrary", "arbitrary")), name=name,
    )(tile_expert, n_used, a, *weights)


def _moe_combine_kernel(*refs, alpha, emit_h):
    it = iter(refs)
    cur_ref, nxt_ref, x_ref, wts_ref, gate_ref, g_ref, b_ref = (next(it) for _ in range(7))
    if emit_h:
        sc_ref, sh_ref = next(it), next(it)
    y_hbm, xo_ref = next(it), next(it)
    ho_ref = next(it) if emit_h else None
    buf, sem = next(it), next(it)
    s, n_steps = pl.program_id(0), pl.num_programs(0)
    n = buf.shape[1]
    tm = n // TOP_K

    @pl.when(s == 0)
    def _():
        _gather_rows(cur_ref, y_hbm, buf, sem, 0, n)

    @pl.when(s + 1 < n_steps)
    def _():
        _gather_rows(nxt_ref, y_hbm, buf, sem, (s + 1) % 2, n)

    slot = s % 2
    _wait_rows(y_hbm, buf, sem, slot, n)
    f = wts_ref[:, 0:1] * buf[slot, 0:tm, :] + wts_ref[:, 1:2] * buf[slot, tm:2 * tm, :]
    xn = _layer_norm(alpha * x_ref[...] + gate_ref[0] * f, g_ref[...], b_ref[...])
    xo_ref[...] = xn
    if emit_h:
        ho_ref[...] = (xn * (1.0 + sc_ref[0]) + sh_ref[0]).astype(ho_ref.dtype)


def _moe_combine_ln(x, y_sorted, dest, wts, mods, ln_g, ln_b, *, alpha, rows_per_group, g0, gate_chunk,
                    next_mods, scale_chunk, shift_chunk, name):
    m, d = x.shape
    tm = _tile(rows_per_group or m, (MOE_COMBINE_TILE, 64, 32, 16, 8))
    steps = m // tm
    idx = jnp.swapaxes(dest.reshape(steps, tm, TOP_K), 1, 2).reshape(steps, 1, TOP_K * tm)
    emit_h = next_mods is not None
    row = pl.BlockSpec((tm, d), lambda s: (s, 0))
    vec = pl.BlockSpec((1, d), lambda s: (0, 0))
    smem = lambda f: pl.BlockSpec((None, 1, TOP_K * tm), f, memory_space=pltpu.SMEM)
    in_specs = [smem(lambda s: (s, 0, 0)), smem(lambda s: (jnp.minimum(s + 1, steps - 1), 0, 0)), row,
                pl.BlockSpec((tm, LANES), lambda s: (s, 0)),
                pl.BlockSpec((1, 1, d), _group_map(rows_per_group, tm, g0, gate_chunk)), vec, vec]
    args = [idx, idx, x, wts, mods, ln_g.reshape(1, d), ln_b.reshape(1, d)]
    out_shape = [jax.ShapeDtypeStruct((m, d), F32)]
    out_specs = [row]
    if emit_h:
        in_specs += [pl.BlockSpec((1, 1, d), _group_map(rows_per_group, tm, g0, scale_chunk)),
                     pl.BlockSpec((1, 1, d), _group_map(rows_per_group, tm, g0, shift_chunk))]
        args += [next_mods, next_mods]
        out_shape.append(jax.ShapeDtypeStruct((m, d), BF16))
        out_specs.append(row)
    in_specs.append(pl.BlockSpec(memory_space=pl.ANY))
    args.append(y_sorted)
    out = pl.pallas_call(
        functools.partial(_moe_combine_kernel, alpha=alpha, emit_h=emit_h),
        out_shape=out_shape, grid=(steps,), in_specs=in_specs, out_specs=out_specs,
        scratch_shapes=[pltpu.VMEM((2, TOP_K * tm, d), F32), pltpu.SemaphoreType.DMA((2,))],
        compiler_params=_cparams(("arbitrary",)), name=name,
    )(*args)
    return out if emit_h else (out[0], None)


def _moe_ln(x, h, route, mods, w1, w3, w2, ln_g, ln_b, *, alpha, rows_per_group, g0, next_mods, name):
    n_experts = w1.shape[0]
    expert_idx = route[:, TOP_K:2 * TOP_K].astype(jnp.int32)
    dest, row_src, tile_expert, n_used = _moe_plan(expert_idx, n_experts, MOE_ROW_TILE)
    xs = _moe_gather(h, row_src)
    up = _moe_grouped(xs, (w1, w3), tile_expert, n_used, tn=_tile(w1.shape[-1], (512, 256, 128)), out_dtype=BF16,
                      name=name + "_up")
    ys = _moe_grouped(up, (w2,), tile_expert, n_used, tn=_tile(w2.shape[-1], (1024, 512, 256, 128)), out_dtype=F32,
                      name=name + "_dn")
    return _moe_combine_ln(x, ys, dest, route, mods, ln_g, ln_b, alpha=alpha, rows_per_group=rows_per_group, g0=g0,
                           gate_chunk=5, next_mods=next_mods, scale_chunk=1, shift_chunk=0, name=name + "_ln")


def kernel(x, c, ctx, c_ctx, w_ada, b_ada, w_in, b_in, attn_q_norm, attn_k_norm, gla_w2, gla_b2, gla_norm,
           s5_lambda_re, s5_lambda_im, s5_log_step, s5_b_re, s5_b_im, s5_c_re, s5_c_im, s5_d, s5_glu_w, s5_glu_b,
           w_branch, w_out, ln1_g, ln1_b, ln2_g, ln2_b, ffn_w1, ffn_w3, ffn_w2, moe_router, moe_w1, moe_w3, moe_w2):
    batch, t, d = x.shape
    tc = ctx.shape[1]
    depth = w_ada.shape[0]
    assert batch + 1 <= MOD_ROWS and t % CHUNK == 0 and tc % CHUNK == 0 and t % GRID_W == 0
    lay = _Layout(d)
    assert lay.in_width == w_in.shape[-1]
    alpha = (2 * depth) ** 0.25
    rope = _rope_tables(t)
    log_gamma = jnp.log1p(-jnp.exp2(-5.0 - jnp.arange(REC_HEADS, dtype=F32)))
    lg_tab = jnp.broadcast_to(log_gamma[:, None, None], (REC_HEADS, 1, HEAD_DIM))
    n_blk = (s5_lambda_re.shape[2] * S5_STATE) // S5_LANE_BLOCK

    xl = x.reshape(batch * t, d)
    xc = ctx.reshape(batch * tc, d)
    cvec = jnp.concatenate([c, c_ctx[None], jnp.zeros((MOD_ROWS - batch - 1, d), F32)], axis=0)

    def mods_of(l):
        return _matmul(cvec, w_ada, b_ada[l], a_silu=True, w_index=l, name=f"ada{l}").reshape(MOD_ROWS, 1, 6 * d)

    mods = mods_of(0)
    hl = _modulate(xl, mods, rows_per_group=t, g0=0, scale_chunk=1, shift_chunk=0, name="mod_lat")
    hc = _modulate(xc, mods, rows_per_group=0, g0=batch, scale_chunk=1, shift_chunk=0, name="mod_ctx")

    for l in range(depth):
        ctx_out = l < depth - 1
        w_in_p = _permute_w_in(w_in, l, lay)
        b_in_p = lay.permute(b_in[l])
        zl = _matmul(hl, w_in_p, b_in_p, out_dtype=BF16, w_rows=True, name=f"in_lat{l}")
        zc = _matmul(hc, w_in_p, b_in_p, n_out=None if ctx_out else lay.small, out_dtype=BF16, w_rows=True,
                     name=f"in_ctx{l}")

        y_att = _attention(zl, zc, zl, lay, attn_q_norm[l], attn_k_norm[l], rope, batch=batch, tq_rows=t, tc=tc, t=t,
                           name=f"attn_lat{l}")
        yc_att = (_attention(zc, zc, None, lay, attn_q_norm[l], attn_k_norm[l], None, batch=batch, tq_rows=tc, tc=tc,
                             t=t, name=f"attn_ctx{l}") if ctx_out else None)

        w2h = gla_w2[l].reshape(2, GLA_RANK, REC_HEADS, HEAD_DIM)
        w2cat = jnp.zeros((REC_HEADS, LANES, 2 * HEAD_DIM), F32)
        for r in range(2):
            w2cat = w2cat.at[:, r * GLA_RANK:(r + 1) * GLA_RANK, r * HEAD_DIM:(r + 1) * HEAD_DIM].set(
                jnp.transpose(w2h[r], (1, 0, 2)))
        b2cat = jnp.transpose(gla_b2[l].reshape(2, REC_HEADS, HEAD_DIM), (1, 0, 2)).reshape(REC_HEADS, 1, 2 * HEAD_DIM)
        y_gla, yc_gla = _recurrence(zl, zc, lay, "gla", (w2cat.astype(BF16), b2cat, gla_norm[l].reshape(1, -1)),
                                    batch=batch, t=t, tc=tc, ctx_out=ctx_out, name=f"gla{l}")
        y_ret, yc_ret = _recurrence(zl, zc, lay, "ret", (lg_tab, rope), batch=batch, t=t, tc=tc, ctx_out=ctx_out,
                                    name=f"ret{l}")

        mw = lay.src["s5_u"][1]
        u0 = lay.off["s5_u"]
        u_tm = jnp.concatenate([jnp.swapaxes(zc[:, u0:u0 + mw].reshape(batch, tc, mw), 0, 1),
                                jnp.swapaxes(zl[:, u0:u0 + mw].reshape(batch, t, mw), 0, 1)], axis=0)
        u_tm = u_tm.reshape((tc + t) * batch, mw)
        tabs = _s5_prep(s5_lambda_re[l], s5_lambda_im[l], s5_log_step[l], s5_b_re[l], s5_b_im[l], batch)
        c_re_bd = _block_diag(jnp.swapaxes(s5_c_re[l], -1, -2), n_blk).astype(BF16)
        c_im_bd = _block_diag(jnp.swapaxes(s5_c_im[l], -1, -2), n_blk).astype(BF16)
        yf, yb = _s5_scan(u_tm, tabs, c_re_bd, c_im_bd, batch=batch, t=t, tc=tc)
        y_s5_tm = _s5_glu(u_tm, yf, yb, s5_d[l], s5_glu_w[l].astype(BF16), s5_glu_b[l])
        y_s5_bm = jnp.swapaxes(y_s5_tm.reshape(tc + t, batch, mw), 0, 1)
        y_s5 = y_s5_bm[:, tc:].reshape(batch * t, mw)
        yc_s5 = y_s5_bm[:, :tc].reshape(batch * tc, mw)

        wb = w_branch[l].astype(BF16)
        wo = w_out[l].astype(BF16)
        i = l // 2
        is_moe = l % 2 == 1
        ln1_extra = dict(router=moe_router[i]) if is_moe else {}
        ml = _matmul(_merge((y_gla, y_att, y_ret, y_s5), zl, lay, wb, name=f"merge_lat{l}"), wo, out_dtype=BF16,
                     name=f"out_lat{l}")
        xl, hl, *route_l = _ln_res(xl, ml, mods, ln1_g[l], ln1_b[l], alpha=alpha, rows_per_group=t, g0=0, gate_chunk=2,
                                   next_mods=mods, scale_chunk=4, shift_chunk=3, name=f"ln1_lat{l}", **ln1_extra)
        if ctx_out:
            mc = _matmul(_merge((yc_gla, yc_att, yc_ret, yc_s5), zc, lay, wb, name=f"merge_ctx{l}"), wo,
                         out_dtype=BF16, name=f"out_ctx{l}")
            xc, hc, *route_c = _ln_res(xc, mc, mods, ln1_g[l], ln1_b[l], alpha=alpha, rows_per_group=0, g0=batch,
                                       gate_chunk=2, next_mods=mods, scale_chunk=4, shift_chunk=3,
                                       name=f"ln1_ctx{l}", **ln1_extra)

        next_mods = mods_of(l + 1) if l + 1 < depth else None
        if is_moe:
            w1, w3, w2 = moe_w1[i], moe_w3[i], moe_w2[i]
            xl, hl = _moe_ln(xl, hl, route_l[0], mods, w1, w3, w2, ln2_g[l], ln2_b[l], alpha=alpha,
                             rows_per_group=t, g0=0, next_mods=next_mods, name=f"moe_lat{l}")
            if ctx_out:
                xc, hc = _moe_ln(xc, hc, route_c[0], mods, w1, w3, w2, ln2_g[l], ln2_b[l], alpha=alpha,
                                 rows_per_group=0, g0=batch, next_mods=next_mods, name=f"moe_ctx{l}")
        else:
            w1, w3, w2 = ffn_w1[i].astype(BF16), ffn_w3[i].astype(BF16), ffn_w2[i].astype(BF16)
            fl = _matmul(_ffn_up(hl, w1, w3, name=f"ffn_up_lat{l}"), w2, out_dtype=BF16, name=f"ffn_dn_lat{l}")
            xl, hl = _ln_res(xl, fl, mods, ln2_g[l], ln2_b[l], alpha=alpha, rows_per_group=t, g0=0, gate_chunk=5,
                             next_mods=next_mods, scale_chunk=1, shift_chunk=0, name=f"ln2_lat{l}")
            if ctx_out:
                fc = _matmul(_ffn_up(hc, w1, w3, name=f"ffn_up_ctx{l}"), w2, out_dtype=BF16, name=f"ffn_dn_ctx{l}")
                xc, hc = _ln_res(xc, fc, mods, ln2_g[l], ln2_b[l], alpha=alpha, rows_per_group=0, g0=batch,
                                 gate_chunk=5, next_mods=next_mods, scale_chunk=1, shift_chunk=0, name=f"ln2_ctx{l}")
        mods = next_mods

    return xl.reshape(batch, t, d)
```
